```python
import jax, jax.numpy as jnp
from jax import lax
import numpy as np

D_MODEL = 1024
BATCH = 8
SEQ = 2048
DEPTH = 2
DEC_BATCH = 128
DEC_SEQ = 8
PAST_LEN = 16384
PAGE_SIZE = 128

HEAD_DIM = 64
D_RWKV = 768
N_HEADS = D_RWKV // HEAD_DIM
D_POOL = 256
POOL_WINDOWS = (2, 4, 8, 16)
N_POOL_GROUPS = len(POOL_WINDOWS)
POOL_GROUP_DIM = D_POOL // N_POOL_GROUPS
POOL_BUF = max(POOL_WINDOWS) - 1
LORA_DECAY = 64
LORA_ICLR = 64
LORA_VRES = 32
LORA_GATE = 128
D_SHIFT = 3 * D_RWKV + LORA_DECAY + LORA_ICLR + LORA_GATE
D_GATES = 2 * D_MODEL
D_IN = D_SHIFT + D_POOL + D_GATES
D_FF = -(-8 * D_MODEL // (3 * 256)) * 256
RMS_EPS = 1e-6
GN_EPS = 1e-5 * HEAD_DIM

kernel_name = 'rwkv7_pool_gated_hybrid_step'


def _rmsnorm(x, g):
    xf = x.astype(jnp.float32)
    y = xf * lax.rsqrt(jnp.mean(xf * xf, axis=-1, keepdims=True) + RMS_EPS)
    return (y * g.astype(jnp.float32)).astype(x.dtype)


def _wkv_recurrence(s0, r, w, k, v, kk, a):
    def step(s, inp):
        r_t, w_t, k_t, v_t, kk_t, a_t = inp
        sa = jnp.einsum('bhvk,bhk->bhv', s, -kk_t)
        s = (s * w_t[:, :, None, :]
             + sa[..., None] * (kk_t * a_t)[:, :, None, :]
             + v_t[..., None] * k_t[:, :, None, :])
        return s, jnp.einsum('bhvk,bhk->bhv', s, r_t)
    xs = tuple(jnp.swapaxes(z, 0, 1) for z in (r, w, k, v, kk, a))
    s_final, o = lax.scan(step, s0, xs)
    return s_final, jnp.swapaxes(o, 0, 1)


def _pool_mixer(u, buf, start_pos, pool_map, pool_scale):
    b, t, _ = u.shape
    uf = u.astype(jnp.float32)
    full = jnp.concatenate([buf.astype(jnp.float32), uf], axis=1)
    csum = jnp.concatenate([jnp.zeros_like(full[:, :1]), jnp.cumsum(full, axis=1)], axis=1)
    upto = csum[:, POOL_BUF + 1:]
    pos = start_pos + jnp.arange(t, dtype=jnp.int32)
    means = []
    for gi, win in enumerate(POOL_WINDOWS):
        c0 = gi * POOL_GROUP_DIM
        c1 = c0 + POOL_GROUP_DIM
        before = csum[:, POOL_BUF + 1 - win:POOL_BUF + 1 - win + t, c0:c1]
        count = jnp.minimum(pos + 1, win).astype(jnp.float32)[None, :, None]
        means.append((upto[..., c0:c1] - before) / count)
    diff = (jnp.concatenate(means, axis=-1) - uf).reshape(b, t, N_POOL_GROUPS, POOL_GROUP_DIM)
    out = jnp.einsum('btgc,gcd->btgd', diff, pool_map.astype(jnp.float32)).reshape(b, t, D_POOL)
    out = out * pool_scale.astype(jnp.float32)
    return out, full[:, -POOL_BUF:].astype(buf.dtype)


def _block(x, shift_prev, wkv_prev, pool_prev, start_pos, v_first, vres, lw):
    (g_mix_pre, g_mix_post, g_ffn_pre, g_ffn_post, w_in, mu_shift, decay_bias, w_decay_up,
     iclr_bias, w_iclr_up, w_gate_up, k_k, k_a, r_k, ln_x_w, ln_x_b, pool_map, pool_scale,
     w_branch_rwkv, w_branch_pool, w_out, w_ffn_in, w_ffn_out) = lw
    f32 = jnp.float32
    b, t, _ = x.shape
    xn = _rmsnorm(x, g_mix_pre)
    proj = jnp.einsum('btd,de->bte', xn, w_in)
    ps = proj[..., :D_SHIFT]
    ps_prev = jnp.concatenate([shift_prev[:, None, :].astype(ps.dtype), ps[:, :-1]], axis=1)
    mixed = (ps + (ps_prev - ps) * mu_shift).astype(f32)
    o1, o2, o3 = D_RWKV, 2 * D_RWKV, 3 * D_RWKV
    o4 = o3 + LORA_DECAY
    o5 = o4 + LORA_ICLR
    r, k, v = mixed[..., :o1], mixed[..., o1:o2], mixed[..., o2:o3]
    xw, xa, xg = mixed[..., o3:o4], mixed[..., o4:o5], mixed[..., o5:]
    u = proj[..., D_SHIFT:D_SHIFT + D_POOL]
    gate_rwkv, gate_pool = jnp.split(jax.nn.sigmoid(proj[..., D_SHIFT + D_POOL:]), 2, axis=-1)

    log_w = -jax.nn.softplus(-(decay_bias + jnp.tanh(xw) @ w_decay_up)) - 0.5
    decay = jnp.exp(-jnp.exp(log_w))
    a = jax.nn.sigmoid(iclr_bias + xa @ w_iclr_up)
    g = jax.nn.sigmoid(xg) @ w_gate_up
    if vres is None:
        v_first = v
    else:
        vres_bias, w_vres_down, w_vres_up = vres
        v = v + (v_first - v) * jax.nn.sigmoid(vres_bias + (xn @ w_vres_down) @ w_vres_up)
    heads = lambda z: z.reshape(b, t, N_HEADS, HEAD_DIM)
    kk = heads(k * k_k)
    kk = kk / jnp.maximum(jnp.sqrt(jnp.sum(kk * kk, axis=-1, keepdims=True)), 1e-12)
    k = k * (1.0 + (a - 1.0) * k_a)
    rh, kh, vh, ah, wh = heads(r), heads(k), heads(v), heads(a), heads(decay)
    s_new, o = _wkv_recurrence(wkv_prev.astype(f32), rh, wh, kh, vh, kk, ah)
    mean = jnp.mean(o, axis=-1, keepdims=True)
    var = jnp.mean(jnp.square(o - mean), axis=-1, keepdims=True)
    o = ((o - mean) * lax.rsqrt(var + GN_EPS)).reshape(b, t, D_RWKV) * ln_x_w + ln_x_b
    bonus = jnp.sum(rh * kh * r_k, axis=-1, keepdims=True) * vh
    o_rwkv = ((o + bonus.reshape(b, t, D_RWKV)) * g).astype(x.dtype)

    o_pool, new_pool = _pool_mixer(u, pool_prev, start_pos, pool_map, pool_scale)

    merged = (gate_rwkv * (o_rwkv @ w_branch_rwkv)
              + gate_pool * (o_pool.astype(x.dtype) @ w_branch_pool))
    x = x + _rmsnorm(merged @ w_out, g_mix_post)
    h = _rmsnorm(x, g_ffn_pre)
    gt, up = jnp.split(h @ w_ffn_in, 2, axis=-1)
    x = x + _rmsnorm((jax.nn.silu(gt) * up) @ w_ffn_out, g_ffn_post)
    return x, v_first, ps[:, -1].astype(shift_prev.dtype), s_new.astype(wkv_prev.dtype), new_pool


def setup_inputs(seed: int = 0) -> dict:
    key = jax.random.key(seed)
    ks = jax.random.split(key, 32)
    f32 = jnp.float32
    nrm = lambda kk, shape, scale: jax.random.normal(kk, shape, f32) * scale
    gain = lambda kk, shape: 1.0 + 0.05 * jax.random.normal(kk, shape, f32)
    return {
        'x_prompt': nrm(ks[0], (BATCH, SEQ, D_MODEL), 1.0),
        'x_sample': nrm(ks[1], (DEC_BATCH, DEC_SEQ, D_MODEL), 1.0),
        'state_wkv': nrm(ks[2], (DEPTH, DEC_BATCH, N_HEADS, HEAD_DIM, HEAD_DIM), 0.3),
        'state_shift': nrm(ks[3], (DEPTH, DEC_BATCH, D_SHIFT), 1.0),
        'state_pool': nrm(ks[4], (DEPTH, DEC_BATCH, POOL_BUF, D_POOL), 1.0),
        'norm_mix_pre': gain(ks[5], (DEPTH, D_MODEL)),
        'norm_mix_post': gain(ks[6], (DEPTH, D_MODEL)),
        'norm_ffn_pre': gain(ks[7], (DEPTH, D_MODEL)),
        'norm_ffn_post': gain(ks[8], (DEPTH, D_MODEL)),
        'w_in': nrm(ks[9], (DEPTH, D_MODEL, D_IN), D_MODEL ** -0.5),
        'mu_shift': jax.random.uniform(ks[10], (DEPTH, D_SHIFT), f32),
        'decay_bias': jax.random.uniform(ks[11], (DEPTH, D_RWKV), f32, -6.0, -1.0),
        'w_decay_up': nrm(ks[12], (DEPTH, LORA_DECAY, D_RWKV), 0.1),
        'iclr_bias': nrm(ks[13], (DEPTH, D_RWKV), 0.5),
        'w_iclr_up': nrm(ks[14], (DEPTH, LORA_ICLR, D_RWKV), LORA_ICLR ** -0.5),
        'w_gate_up': nrm(ks[15], (DEPTH, LORA_GATE, D_RWKV), LORA_GATE ** -0.5),
        'k_k': 0.85 + nrm(ks[16], (DEPTH, D_RWKV), 0.05),
        'k_a': gain(ks[17], (DEPTH, D_RWKV)),
        'r_k': nrm(ks[18], (DEPTH, N_HEADS, HEAD_DIM), 0.1),
        'ln_x_w': gain(ks[19], (DEPTH, D_RWKV)),
        'ln_x_b': nrm(ks[20], (DEPTH, D_RWKV), 0.02),
        'vres_bias': nrm(ks[21], (DEPTH - 1, D_RWKV), 0.5),
        'w_vres_down': nrm(ks[22], (DEPTH - 1, D_MODEL, LORA_VRES), D_MODEL ** -0.5),
        'w_vres_up': nrm(ks[23], (DEPTH - 1, LORA_VRES, D_RWKV), LORA_VRES ** -0.5),
        'pool_map': nrm(ks[24], (DEPTH, N_POOL_GROUPS, POOL_GROUP_DIM, POOL_GROUP_DIM), POOL_GROUP_DIM ** -0.5),
        'pool_scale': gain(ks[25], (DEPTH, D_POOL)),
        'w_branch_rwkv': nrm(ks[26], (DEPTH, D_RWKV, D_MODEL), D_RWKV ** -0.5),
        'w_branch_pool': nrm(ks[27], (DEPTH, D_POOL, D_MODEL), D_POOL ** -0.5),
        'w_out': nrm(ks[28], (DEPTH, D_MODEL, D_MODEL), D_MODEL ** -0.5),
        'w_ffn_in': nrm(ks[29], (DEPTH, D_MODEL, 2 * D_FF), D_MODEL ** -0.5),
        'w_ffn_out': nrm(ks[30], (DEPTH, D_FF, D_MODEL), D_FF ** -0.5),
    }


def reference(x_prompt, x_sample, state_wkv, state_shift, state_pool,
              norm_mix_pre, norm_mix_post, norm_ffn_pre, norm_ffn_post,
              w_in, mu_shift, decay_bias, w_decay_up, iclr_bias, w_iclr_up, w_gate_up,
              k_k, k_a, r_k, ln_x_w, ln_x_b, vres_bias, w_vres_down, w_vres_up,
              pool_map, pool_scale, w_branch_rwkv, w_branch_pool, w_out, w_ffn_in, w_ffn_out):
    b_p = x_prompt.shape[0]
    wkv0 = jnp.zeros((b_p, N_HEADS, HEAD_DIM, HEAD_DIM), state_wkv.dtype)
    shift0 = jnp.zeros((b_p, D_SHIFT), state_shift.dtype)
    pool0 = jnp.zeros((b_p, POOL_BUF, D_POOL), state_pool.dtype)
    y_p, y_s = x_prompt, x_sample
    vf_p, vf_s = None, None
    wkv_p, shift_p, pool_p = [], [], []
    wkv_s, shift_s, pool_s = [], [], []
    for l in range(DEPTH):
        lw = (norm_mix_pre[l], norm_mix_post[l], norm_ffn_pre[l], norm_ffn_post[l], w_in[l],
              mu_shift[l], decay_bias[l], w_decay_up[l], iclr_bias[l], w_iclr_up[l], w_gate_up[l],
              k_k[l], k_a[l], r_k[l], ln_x_w[l], ln_x_b[l], pool_map[l], pool_scale[l],
              w_branch_rwkv[l], w_branch_pool[l], w_out[l], w_ffn_in[l], w_ffn_out[l])
        vres = None if l == 0 else (vres_bias[l - 1], w_vres_down[l - 1], w_vres_up[l - 1])
        y_p, vf_p, sh, wk, pl = _block(y_p, shift0, wkv0, pool0, 0, vf_p, vres, lw)
        wkv_p.append(wk)
        shift_p.append(sh)
        pool_p.append(pl)
        y_s, vf_s, sh, wk, pl = _block(y_s, state_shift[l], state_wkv[l], state_pool[l],
                                       PAST_LEN, vf_s, vres, lw)
        wkv_s.append(wk)
        shift_s.append(sh)
        pool_s.append(pl)
    return (y_p, y_s,
            jnp.stack(wkv_p), jnp.stack(shift_p), jnp.stack(pool_p),
            jnp.stack(wkv_s), jnp.stack(shift_s), jnp.stack(pool_s))
```

```python
import functools

import numpy as np
import jax
import jax.numpy as jnp
from jax.experimental import pallas as pl
from jax.experimental.pallas import tpu as pltpu

F32 = jnp.float32
BF16 = jnp.bfloat16

HEAD = 64
D_RWKV = 768
N_HEADS = D_RWKV // HEAD
PAIR = 2 * HEAD
N_PAIRS = D_RWKV // PAIR
D_POOL = 256
POOL_WINDOWS = (2, 4, 8, 16)
POOL_GROUP = D_POOL // len(POOL_WINDOWS)
POOL_BUF = max(POOL_WINDOWS) - 1
POOL_SLOTS = POOL_BUF + 1
LORA_DECAY = 64
LORA_ICLR = 64
LORA_GATE = 128
LORA_VRES = 32
D_SHIFT = 3 * D_RWKV + LORA_DECAY + LORA_ICLR + LORA_GATE
SAMPLE_START_POS = 16384
RMS_EPS = 1e-6
GN_EPS = 1e-5 * HEAD
CHUNK = 64
CARRY = 8
LANES = 128
VMEM_LIMIT = 56 * 1024 * 1024


def _dot(a, b):
    return jnp.dot(a.astype(BF16), b.astype(BF16), preferred_element_type=F32)


def _dot_nt(a, b):
    return jax.lax.dot_general(a.astype(BF16), b.astype(BF16), (((1,), (1,)), ((), ())),
                               preferred_element_type=F32)


def _split(x, parts):
    out = []
    for _ in range(parts - 1):
        hi = x.astype(BF16)
        out.append(hi)
        x = x - hi.astype(F32)
    out.append(x.astype(BF16))
    return out


def _dot_exact_lhs(sel, x, parts):
    acc = None
    for p in _split(x, parts):
        t = jnp.dot(sel, p, preferred_element_type=F32)
        acc = t if acc is None else acc + t
    return acc


def _dot_exact_rhs(x, sel, parts):
    acc = None
    for p in _split(x, parts):
        t = jnp.dot(p, sel, preferred_element_type=F32)
        acc = t if acc is None else acc + t
    return acc


def _sigmoid(x):
    return 1.0 / (1.0 + jnp.exp(-x))


def _softplus(x):
    return jnp.maximum(x, 0.0) + jnp.log(1.0 + jnp.exp(-jnp.abs(x)))


def _rms(x, g):
    return x * jax.lax.rsqrt(jnp.mean(x * x, axis=-1, keepdims=True) + RMS_EPS) * g


def _proj_kernel(*refs, tm, seq_len, has_vres, has_first):
    it = iter(refs)
    x_ref, gpre_ref, win_ref, mu_ref, lora_ref, dbias_ref, ibias_ref, wgate_ref = (next(it) for _ in range(8))
    kk_ref, ka_ref, rk_ref, ones_ref = (next(it) for _ in range(4))
    if has_vres:
        vfirst_ref, vbias_ref, wvup_ref = (next(it) for _ in range(3))
    if has_first:
        first_ref = next(it)
    (r_out, kp_out, v_out, kkn_out, a_out, ld_out, g_out, bonus_out, u_out, gates_out,
     last_out, p_scr) = (next(it) for _ in range(12))

    gates_end = D_SHIFT + D_POOL + 2 * x_ref.shape[1]
    i = pl.program_id(0)
    if seq_len >= tm:
        tiles_per_seq = seq_len // tm
        new_seq = (i % tiles_per_seq) == 0
    else:
        new_seq = i == 0

    @pl.when(new_seq)
    def _():
        p_scr[0:CARRY, :] = jnp.zeros((CARRY, D_SHIFT), F32)

    xn = _rms(x_ref[...], gpre_ref[...]).astype(BF16)
    p_scr[CARRY:CARRY + tm, :] = jnp.dot(xn, win_ref[:, 0:D_SHIFT], preferred_element_type=F32)

    if has_first:
        row = jax.lax.broadcasted_iota(jnp.int32, (tm, 1), 0)
        seq_start = jax.lax.rem(row, seq_len) == 0

    def mixed(c0, c1):
        ps = p_scr[CARRY:CARRY + tm, c0:c1]
        prev = p_scr[CARRY - 1:CARRY - 1 + tm, c0:c1]
        if has_first:
            prev = jnp.where(seq_start, first_ref[:, c0:c1], prev)
        return ps + (prev - ps) * mu_ref[:, c0:c1]

    o1, o2, o3 = D_RWKV, 2 * D_RWKV, 3 * D_RWKV
    o5 = o3 + LORA_DECAY + LORA_ICLR
    r = mixed(0, o1)
    k = mixed(o1, o2)
    v = mixed(o2, o3)
    xwa = mixed(o3, o5)
    xg = mixed(o5, D_SHIFT)

    lane = jax.lax.broadcasted_iota(jnp.int32, (1, LANES), 1)
    lora_in = jnp.where(lane < LORA_DECAY, jnp.tanh(xwa), xwa)
    lora = _dot(lora_in, lora_ref[...])
    log_w = -_softplus(-(dbias_ref[...] + lora[:, 0:D_RWKV])) - 0.5
    ld_out[...] = -jnp.exp(log_w)
    a = _sigmoid(ibias_ref[...] + lora[:, D_RWKV:2 * D_RWKV])
    a_out[...] = a
    g_out[...] = _dot(_sigmoid(xg), wgate_ref[...])

    if has_vres:
        vdown = jnp.dot(xn, win_ref[:, gates_end:], preferred_element_type=F32)
        v = v + (vfirst_ref[...] - v) * _sigmoid(vbias_ref[...] + _dot(vdown, wvup_ref[...]))
    v_out[...] = v

    kk = k * kk_ref[...]
    norm = jnp.sqrt(_dot_exact_rhs(kk * kk, ones_ref[...], 2))
    kkn_out[...] = kk / jnp.maximum(norm, 1e-12)
    kp = k * (1.0 + (a - 1.0) * ka_ref[...])
    kp_out[...] = kp
    r_out[...] = r
    bonus_out[...] = _dot_exact_rhs(r * kp * rk_ref[...], ones_ref[...], 2) * v

    u_out[...] = jnp.dot(xn, win_ref[:, D_SHIFT:D_SHIFT + D_POOL], preferred_element_type=F32)
    gates_out[...] = _sigmoid(jnp.dot(xn, win_ref[:, D_SHIFT + D_POOL:gates_end],
                                      preferred_element_type=F32))

    if has_first:
        last_out[...] = p_scr[CARRY:CARRY + tm, :]
    else:
        tail = p_scr[tm:tm + CARRY, :]
        last_out[...] = tail
        p_scr[0:CARRY, :] = tail


def _full(shape):
    nd = len(shape)
    return pl.BlockSpec(shape, lambda *_: (0,) * nd)


def _proj_call(x2d, seq_len, tm, lw, vfirst, first):
    n = x2d.shape[0]
    d_model = x2d.shape[1]
    has_vres = vfirst is not None
    has_first = first is not None
    nt = n // tm
    row = lambda w: pl.BlockSpec((tm, w), lambda i: (i, 0))
    ins = [x2d, lw['g_mix_pre'], lw['w_in'], lw['mu'], lw['lora'], lw['decay_bias'], lw['iclr_bias'],
           lw['w_gate_up'], lw['k_k'], lw['k_a'], lw['r_k'], lw['ones']]
    specs = [row(d_model)] + [_full(a.shape) for a in ins[1:]]
    if has_vres:
        ins += [vfirst, lw['vres_bias'], lw['w_vres_up']]
        specs += [row(D_RWKV), _full(lw['vres_bias'].shape), _full(lw['w_vres_up'].shape)]
    if has_first:
        ins.append(first)
        specs.append(row(D_SHIFT))
    sds = lambda w: jax.ShapeDtypeStruct((n, w), F32)
    last_rows = tm if has_first else CARRY
    out_shape = [sds(D_RWKV)] * 8 + [sds(D_POOL), sds(2 * d_model),
                                     jax.ShapeDtypeStruct((nt * last_rows, D_SHIFT), F32)]
    out_specs = [row(D_RWKV)] * 8 + [row(D_POOL), row(2 * d_model),
                                     pl.BlockSpec((last_rows, D_SHIFT), lambda i: (i, 0))]
    return pl.pallas_call(
        functools.partial(_proj_kernel, tm=tm, seq_len=seq_len, has_vres=has_vres, has_first=has_first),
        grid=(nt,), in_specs=specs, out_specs=out_specs, out_shape=out_shape,
        scratch_shapes=[pltpu.VMEM((tm + CARRY, D_SHIFT), F32)],
        compiler_params=pltpu.CompilerParams(dimension_semantics=("arbitrary",),
                                             vmem_limit_bytes=VMEM_LIMIT),
    )(*ins)


def _wkv_kernel(*refs, has_h0):
    it = iter(refs)
    r_ref, kp_ref, v_ref, kk_ref, a_ref, ld_ref = (next(it) for _ in range(6))
    if has_h0:
        h0_ref = next(it)
    o_ref, hout_ref, h_scr = (next(it) for _ in range(3))
    c_idx = pl.program_id(1)
    n_chunks = pl.num_programs(1)
    C = CHUNK

    lane = jax.lax.broadcasted_iota(jnp.int32, (1, PAIR), 1)
    left = lane < HEAD
    ri = jax.lax.broadcasted_iota(jnp.int32, (PAIR, PAIR), 0)
    ci = jax.lax.broadcasted_iota(jnp.int32, (PAIR, PAIR), 1)
    same = (ri >= HEAD) == (ci >= HEAD)
    strict = same & (ci < ri)
    incl = same & (ci <= ri)
    eye = ri == ci
    eye_f = jnp.where(eye, 1.0, 0.0).astype(F32)

    def stack(x):
        return jnp.concatenate([jnp.where(left, x, 0.0), jnp.where(left, 0.0, x)], axis=0)

    def twice(x):
        return jnp.concatenate([x, x], axis=0)

    @pl.when(c_idx == 0)
    def _():
        if has_h0:
            for p in range(N_PAIRS):
                h_scr[p] = stack(h0_ref[0, p])
        else:
            h_scr[...] = jnp.zeros(h_scr.shape, F32)

    ld = ld_ref[...]
    tr = jax.lax.broadcasted_iota(jnp.int32, (C, C), 0)
    tc = jax.lax.broadcasted_iota(jnp.int32, (C, C), 1)
    tri = jnp.where(tr >= tc, 1.0, 0.0).astype(BF16)
    cum = _dot_exact_lhs(tri, ld, 3)
    clast = cum[C - 1:C, :]
    e_cum = jnp.exp(cum)
    e_prev = jnp.exp(cum - ld)
    e_neg = jnp.exp(-cum)
    e_rem = jnp.exp(clast - cum)
    w_last = jnp.exp(clast)

    kk = kk_ref[...]
    b = kk * a_ref[...]
    kp = kp_ref[...]
    a_t = -(kk * e_prev)
    r_t = r_ref[...] * e_cum
    b_t = b * e_neg
    k_t = kp * e_neg
    b_h = b * e_rem
    k_h = kp * e_rem
    v = v_ref[...]

    for p in range(N_PAIRS):
        sl = slice(p * PAIR, (p + 1) * PAIR)
        v_st = stack(v[:, sl])
        sc = _dot_nt(jnp.concatenate([a_t[:, sl], r_t[:, sl]], axis=0),
                     jnp.concatenate([stack(b_t[:, sl]), stack(k_t[:, sl])], axis=0))
        n_ab = jnp.where(strict, twice(sc[0:C, 0:PAIR]), 0.0)
        n_ak = jnp.where(strict, twice(sc[0:C, PAIR:2 * PAIR]), 0.0)
        n_rb = jnp.where(incl, twice(sc[C:2 * C, 0:PAIR]), 0.0)
        n_rk = jnp.where(incl, twice(sc[C:2 * C, PAIR:2 * PAIR]), 0.0)

        inv = eye_f + n_ab
        power = n_ab
        for _ in range(int(np.log2(C)) - 1):
            power = _dot(power, power)
            inv = inv + _dot(inv, power)

        y = _dot(n_ak, v_st)
        au = _dot(inv, jnp.concatenate([stack(a_t[:, sl]), y], axis=1))
        rhs = jnp.concatenate(
            [au, jnp.concatenate([jnp.zeros((PAIR, PAIR), F32), v_st], axis=1)], axis=0)
        lhs = jnp.concatenate(
            [jnp.concatenate([n_rb, n_rk], axis=1),
             jnp.concatenate([stack(b_h[:, sl]).T, stack(k_h[:, sl]).T], axis=1)], axis=0)
        big = _dot(lhs, rhs)
        r_hat = stack(r_t[:, sl]) + big[0:PAIR, 0:PAIR]
        o_bar = big[0:PAIR, PAIR:2 * PAIR]
        m_low = big[PAIR:2 * PAIR, 0:PAIR]
        g_inc = big[PAIR:2 * PAIR, PAIR:2 * PAIR]

        h = h_scr[p]
        sd = _dot(jnp.concatenate([r_hat, m_low], axis=0), h)
        o_st = sd[0:PAIR] + o_bar
        o_ref[:, sl] = o_st[0:C] + o_st[C:2 * C]
        w_col = jnp.sum(jnp.where(eye, jnp.broadcast_to(w_last[:, sl], (PAIR, PAIR)), 0.0),
                        axis=1, keepdims=True)
        h_new = w_col * h + sd[PAIR:2 * PAIR] + g_inc
        h_scr[p] = h_new

        @pl.when(c_idx == n_chunks - 1)
        def _():
            hout_ref[0, p] = h_new[0:HEAD] + h_new[HEAD:PAIR]


def _wkv_call(ops, n_seq, seq_len, h0):
    nc = seq_len // CHUNK
    has_h0 = h0 is not None
    blk = pl.BlockSpec((CHUNK, D_RWKV), lambda b, c: (b * nc + c, 0))
    hspec = pl.BlockSpec((1, N_PAIRS, HEAD, PAIR), lambda b, c: (b, 0, 0, 0))
    ins = list(ops)
    specs = [blk] * 6
    if has_h0:
        ins.append(h0)
        specs.append(hspec)
    return pl.pallas_call(
        functools.partial(_wkv_kernel, has_h0=has_h0),
        grid=(n_seq, nc), in_specs=specs, out_specs=[blk, hspec],
        out_shape=[jax.ShapeDtypeStruct((n_seq * seq_len, D_RWKV), F32),
                   jax.ShapeDtypeStruct((n_seq, N_PAIRS, HEAD, PAIR), F32)],
        scratch_shapes=[pltpu.VMEM((N_PAIRS, PAIR, PAIR), F32)],
        compiler_params=pltpu.CompilerParams(dimension_semantics=("arbitrary", "arbitrary"),
                                             vmem_limit_bytes=VMEM_LIMIT),
    )(*ins)


def _merge_kernel(*refs, tm, tiles_per_seq, zero_first_hist):
    (o_ref, bonus_ref, g_ref, u_ref, hist_ref, gates_ref, x_ref, band_ref, invc_ref, lnw_ref, lnb_ref,
     ones_ref, pmap_ref, pscale_ref, wba_ref, wbb_ref, wout_ref, gpost_ref, y_ref) = refs
    d_model = x_ref.shape[1]
    inv_h = 1.0 / HEAD

    o = o_ref[...]
    mean = _dot_exact_rhs(o, ones_ref[...], 2) * inv_h
    cen = o - mean
    var = _dot_exact_rhs(cen * cen, ones_ref[...], 2) * inv_h
    o_n = cen * jax.lax.rsqrt(var + GN_EPS) * lnw_ref[...] + lnb_ref[...]
    o_rwkv = (o_n + bonus_ref[...]) * g_ref[...]

    u = u_ref[...]
    hist = hist_ref[...]
    if zero_first_hist:
        keep = jnp.where((pl.program_id(0) % tiles_per_seq) == 0, 0.0, 1.0)
        hist = hist * keep
    full = jnp.concatenate([hist, u], axis=0)
    lane = jax.lax.broadcasted_iota(jnp.int32, (1, D_POOL), 1)
    win_sum = jnp.zeros((tm, D_POOL), F32)
    for gi in range(len(POOL_WINDOWS)):
        s = _dot_exact_lhs(band_ref[gi], full, 2)
        in_group = (lane >= gi * POOL_GROUP) & (lane < (gi + 1) * POOL_GROUP)
        win_sum = jnp.where(in_group, s, win_sum)
    diff = win_sum * invc_ref[...] - u
    o_pool = _dot(diff, pmap_ref[...]) * pscale_ref[...]

    gates = gates_ref[...]
    merged = (gates[:, 0:d_model] * _dot(o_rwkv, wba_ref[...])
              + gates[:, d_model:2 * d_model] * _dot(o_pool, wbb_ref[...]))
    y_ref[...] = x_ref[...] + _rms(_dot(merged, wout_ref[...]), gpost_ref[...])


def _merge_call(o, bonus, g, u, hist, hist_rows, gates, x2d, seq_len, tm, band, invc, lw, zero_first_hist):
    n, d_model = x2d.shape
    nt = n // tm
    tiles_per_seq = max(seq_len // tm, 1)
    row = lambda w: pl.BlockSpec((tm, w), lambda i: (i, 0))
    if zero_first_hist:
        per = tm // hist_rows
        hist_spec = pl.BlockSpec((hist_rows, D_POOL), lambda i: (jnp.maximum(i * per - 1, 0), 0))
        invc_spec = pl.BlockSpec((tm, D_POOL), lambda i: (i % tiles_per_seq, 0))
    else:
        hist_spec = pl.BlockSpec((hist_rows, D_POOL), lambda i: (i, 0))
        invc_spec = _full(invc.shape)
    consts = [lw['ln_x_w'], lw['ln_x_b'], lw['ones'], lw['pool_map'], lw['pool_scale'],
              lw['w_branch_rwkv'], lw['w_branch_pool'], lw['w_out'], lw['g_mix_post']]
    ins = [o, bonus, g, u, hist, gates, x2d, band, invc] + consts
    specs = ([row(D_RWKV)] * 3 + [row(D_POOL), hist_spec, row(2 * d_model), row(d_model),
                                  _full(band.shape), invc_spec] + [_full(a.shape) for a in consts])
    return pl.pallas_call(
        functools.partial(_merge_kernel, tm=tm, tiles_per_seq=tiles_per_seq, zero_first_hist=zero_first_hist),
        grid=(nt,), in_specs=specs, out_specs=row(d_model),
        out_shape=jax.ShapeDtypeStruct((n, d_model), F32),
        compiler_params=pltpu.CompilerParams(dimension_semantics=("arbitrary",),
                                             vmem_limit_bytes=VMEM_LIMIT),
    )(*ins)


def _ffn_kernel(x_ref, gpre_ref, win_ref, wout_ref, gpost_ref, y_ref):
    x = x_ref[...]
    d_ff = wout_ref.shape[0]
    h = _rms(x, gpre_ref[...]).astype(BF16)
    gu = jnp.dot(h, win_ref[...], preferred_element_type=F32)
    gt = gu[:, 0:d_ff]
    act = gt * _sigmoid(gt) * gu[:, d_ff:2 * d_ff]
    y_ref[...] = x + _rms(_dot(act, wout_ref[...]), gpost_ref[...])


def _ffn_call(x2d, tm, lw):
    n, d_model = x2d.shape
    row = pl.BlockSpec((tm, d_model), lambda i: (i, 0))
    consts = [lw['g_ffn_pre'], lw['w_ffn_in'], lw['w_ffn_out'], lw['g_ffn_post']]
    return pl.pallas_call(
        _ffn_kernel, grid=(n // tm,), in_specs=[row] + [_full(a.shape) for a in consts],
        out_specs=row, out_shape=jax.ShapeDtypeStruct((n, d_model), F32),
        compiler_params=pltpu.CompilerParams(dimension_semantics=("arbitrary",),
                                             vmem_limit_bytes=VMEM_LIMIT),
    )(x2d, *consts)


def _prompt_pool_consts(tm, seq_len, hist_rows):
    t = np.arange(tm)[:, None]
    j = np.arange(hist_rows + tm)[None, :]
    dist = t + hist_rows - j
    band = np.stack([(dist >= 0) & (dist < w) for w in POOL_WINDOWS]).astype(np.float32)
    pos = np.arange(seq_len)[:, None]
    win = np.repeat(np.array(POOL_WINDOWS), POOL_GROUP)[None, :]
    invc = 1.0 / np.minimum(pos + 1, win).astype(np.float32)
    return jnp.asarray(band, BF16), jnp.asarray(invc, F32)


def _sample_pool_consts(tm, seq_len):
    ns = tm // seq_len
    rs, rt = np.divmod(np.arange(tm), seq_len)
    hs, hj = np.divmod(np.arange(ns * POOL_SLOTS), POOL_SLOTS)
    cs = np.concatenate([hs, rs])[None, :]
    cpos = np.concatenate([hj - 1, POOL_BUF + rt])[None, :]
    dist = (POOL_BUF + rt)[:, None] - cpos
    same = (rs[:, None] == cs) & (cpos >= 0)
    band = np.stack([same & (dist >= 0) & (dist < w) for w in POOL_WINDOWS]).astype(np.float32)
    win = np.repeat(np.array(POOL_WINDOWS), POOL_GROUP)[None, :]
    pos = (SAMPLE_START_POS + rt)[:, None]
    invc = 1.0 / np.minimum(pos + 1, win).astype(np.float32)
    return jnp.asarray(band, BF16), jnp.asarray(invc, F32)


def _block_diag(blocks):
    n = len(blocks)
    rows = []
    for i, blk in enumerate(blocks):
        rows.append(jnp.concatenate(
            [blk if j == i else jnp.zeros((blk.shape[0], blocks[j].shape[1]), blk.dtype) for j in range(n)],
            axis=1))
    return jnp.concatenate(rows, axis=0)


def _layer_weights(l, w):
    d_model = w['w_in'].shape[1]
    row = lambda a: a.reshape(1, -1).astype(F32)
    w_in = w['w_in'][l]
    lw = {
        'g_mix_pre': row(w['norm_mix_pre'][l]), 'g_mix_post': row(w['norm_mix_post'][l]),
        'g_ffn_pre': row(w['norm_ffn_pre'][l]), 'g_ffn_post': row(w['norm_ffn_post'][l]),
        'mu': row(w['mu_shift'][l]), 'decay_bias': row(w['decay_bias'][l]), 'iclr_bias': row(w['iclr_bias'][l]),
        'lora': _block_diag([w['w_decay_up'][l], w['w_iclr_up'][l]]).astype(BF16),
        'w_gate_up': w['w_gate_up'][l].astype(BF16),
        'k_k': row(w['k_k'][l]), 'k_a': row(w['k_a'][l]), 'r_k': row(w['r_k'][l]),
        'ln_x_w': row(w['ln_x_w'][l]), 'ln_x_b': row(w['ln_x_b'][l]),
        'pool_map': _block_diag([w['pool_map'][l, gi] for gi in range(len(POOL_WINDOWS))]).astype(BF16),
        'pool_scale': row(w['pool_scale'][l]),
        'w_branch_rwkv': w['w_branch_rwkv'][l].astype(BF16), 'w_branch_pool': w['w_branch_pool'][l].astype(BF16),
        'w_out': w['w_out'][l].astype(BF16),
        'w_ffn_in': w['w_ffn_in'][l].astype(BF16), 'w_ffn_out': w['w_ffn_out'][l].astype(BF16),
        'ones': jnp.asarray(np.kron(np.eye(N_HEADS), np.ones((HEAD, HEAD))), BF16),
    }
    if l > 0:
        pad = jnp.zeros((d_model, LANES - LORA_VRES), F32)
        w_in = jnp.concatenate([w_in, w['w_vres_down'][l - 1], pad], axis=1)
        lw['vres_bias'] = row(w['vres_bias'][l - 1])
        lw['w_vres_up'] = jnp.concatenate(
            [w['w_vres_up'][l - 1], jnp.zeros((LANES - LORA_VRES, D_RWKV), F32)], axis=0).astype(BF16)
    lw['w_in'] = w_in.astype(BF16)
    return lw


def _to_pairs(s):
    b = s.shape[0]
    return s.reshape(b, N_PAIRS, 2, HEAD, HEAD).transpose(0, 1, 4, 2, 3).reshape(b, N_PAIRS, HEAD, PAIR)


def _from_pairs(h):
    b = h.shape[0]
    return h.reshape(b, N_PAIRS, HEAD, 2, HEAD).transpose(0, 1, 3, 4, 2).reshape(b, N_HEADS, HEAD, HEAD)


def _tile_rows(n_seq, seq_len, target):
    if seq_len >= target:
        return target
    return min(n_seq * seq_len, target)


def kernel(x_prompt, x_sample, state_wkv, state_shift, state_pool, norm_mix_pre, norm_mix_post, norm_ffn_pre, norm_ffn_post, w_in, mu_shift, decay_bias, w_decay_up, iclr_bias, w_iclr_up, w_gate_up, k_k, k_a, r_k, ln_x_w, ln_x_b, vres_bias, w_vres_down, w_vres_up, pool_map, pool_scale, w_branch_rwkv, w_branch_pool, w_out, w_ffn_in, w_ffn_out):
    weights = dict(norm_mix_pre=norm_mix_pre, norm_mix_post=norm_mix_post, norm_ffn_pre=norm_ffn_pre,
                   norm_ffn_post=norm_ffn_post, w_in=w_in, mu_shift=mu_shift, decay_bias=decay_bias,
                   w_decay_up=w_decay_up, iclr_bias=iclr_bias, w_iclr_up=w_iclr_up, w_gate_up=w_gate_up,
                   k_k=k_k, k_a=k_a, r_k=r_k, ln_x_w=ln_x_w, ln_x_b=ln_x_b, vres_bias=vres_bias,
                   w_vres_down=w_vres_down, w_vres_up=w_vres_up, pool_map=pool_map, pool_scale=pool_scale,
                   w_branch_rwkv=w_branch_rwkv, w_branch_pool=w_branch_pool, w_out=w_out,
                   w_ffn_in=w_ffn_in, w_ffn_out=w_ffn_out)
    depth = w_in.shape[0]
    bp, tp, d_model = x_prompt.shape
    bs, ts, _ = x_sample.shape
    tm_p = _tile_rows(bp, tp, 256)
    tm_s = _tile_rows(bs, ts, 256)
    hist_p = LANES
    band_p, invc_p = _prompt_pool_consts(tm_p, tp, hist_p)
    band_s, invc_s = _sample_pool_consts(tm_s, ts)
    ts_pad = -(-ts // CHUNK) * CHUNK

    yp = x_prompt.reshape(bp * tp, d_model)
    ys = x_sample.reshape(bs * ts, d_model)
    vf_p = vf_s = None
    outs = {k: [] for k in ('wkv_p', 'shift_p', 'pool_p', 'wkv_s', 'shift_s', 'pool_s')}
    for l in range(depth):
        lw = _layer_weights(l, weights)

        (r, kp, v, kkn, a, ld, g, bonus, u, gates, last) = _proj_call(yp, tp, tm_p, lw, vf_p, None)
        if l == 0:
            vf_p = v
        o, hout = _wkv_call((r, kp, v, kkn, a, ld), bp, tp, None)
        x1 = _merge_call(o, bonus, g, u, u, hist_p, gates, yp, tp, tm_p, band_p, invc_p, lw, True)
        yp = _ffn_call(x1, tm_p, lw)
        outs['wkv_p'].append(_from_pairs(hout))
        tiles = tp // tm_p
        outs['shift_p'].append(last.reshape(bp, tiles, CARRY, D_SHIFT)[:, -1, -1])
        outs['pool_p'].append(u.reshape(bp, tp, D_POOL)[:, -POOL_BUF:])

        first = jnp.zeros((bs, ts, D_SHIFT), F32).at[:, 0].set(state_shift[l]).reshape(bs * ts, D_SHIFT)
        (r, kp, v, kkn, a, ld, g, bonus, u, gates, ps_all) = _proj_call(ys, ts, tm_s, lw, vf_s, first)
        if l == 0:
            vf_s = v
        pad = lambda z: jnp.pad(z.reshape(bs, ts, D_RWKV), ((0, 0), (0, ts_pad - ts), (0, 0))).reshape(
            bs * ts_pad, D_RWKV)
        o_pad, hout = _wkv_call(tuple(pad(z) for z in (r, kp, v, kkn, a, ld)), bs, ts_pad,
                                _to_pairs(state_wkv[l]))
        o = o_pad.reshape(bs, ts_pad, D_RWKV)[:, :ts].reshape(bs * ts, D_RWKV)
        hist = jnp.pad(state_pool[l], ((0, 0), (1, 0), (0, 0))).reshape(bs * POOL_SLOTS, D_POOL)
        x1 = _merge_call(o, bonus, g, u, hist, (tm_s // ts) * POOL_SLOTS, gates, ys, ts, tm_s, band_s, invc_s,
                         lw, False)
        ys = _ffn_call(x1, tm_s, lw)
        outs['wkv_s'].append(_from_pairs(hout))
        outs['shift_s'].append(ps_all.reshape(bs, ts, D_SHIFT)[:, -1])
        outs['pool_s'].append(jnp.concatenate(
            [state_pool[l], u.reshape(bs, ts, D_POOL)], axis=1)[:, -POOL_BUF:])

    return (yp.reshape(bp, tp, d_model), ys.reshape(bs, ts, d_model),
            jnp.stack(outs['wkv_p']), jnp.stack(outs['shift_p']), jnp.stack(outs['pool_p']),
            jnp.stack(outs['wkv_s']), jnp.stack(outs['shift_s']), jnp.stack(outs['pool_s']))
```

```python
import functools

import numpy as np
import jax
import jax.numpy as jnp
from jax.experimental import pallas as pl
from jax.experimental.pallas import tpu as pltpu

F32 = jnp.float32
BF16 = jnp.bfloat16

HEAD = 64
D_RWKV = 768
N_HEADS = D_RWKV // HEAD
PAIR = 2 * HEAD
N_PAIRS = D_RWKV // PAIR
D_POOL = 256
POOL_WINDOWS = (2, 4, 8, 16)
POOL_GROUP = D_POOL // len(POOL_WINDOWS)
POOL_BUF = max(POOL_WINDOWS) - 1
POOL_SLOTS = POOL_BUF + 1
LORA_DECAY = 64
LORA_ICLR = 64
LORA_GATE = 128
LORA_VRES = 32
D_SHIFT = 3 * D_RWKV + LORA_DECAY + LORA_ICLR + LORA_GATE
SAMPLE_START_POS = 16384
RMS_EPS = 1e-6
GN_EPS = 1e-5 * HEAD
CHUNK = 64
CARRY = 8
LANES = 128
VMEM_LIMIT = 56 * 1024 * 1024


def _dot(a, b):
    return jnp.dot(a.astype(BF16), b.astype(BF16), preferred_element_type=F32)


def _dot_nt(a, b):
    return jax.lax.dot_general(a.astype(BF16), b.astype(BF16), (((1,), (1,)), ((), ())),
                               preferred_element_type=F32)


def _split(x, parts):
    out = []
    for _ in range(parts - 1):
        hi = x.astype(BF16)
        out.append(hi)
        x = x - hi.astype(F32)
    out.append(x.astype(BF16))
    return out


def _dot_exact_lhs(sel, x, parts):
    acc = None
    for p in _split(x, parts):
        t = jnp.dot(sel, p, preferred_element_type=F32)
        acc = t if acc is None else acc + t
    return acc


def _dot_exact_rhs(x, sel, parts):
    acc = None
    for p in _split(x, parts):
        t = jnp.dot(p, sel, preferred_element_type=F32)
        acc = t if acc is None else acc + t
    return acc


def _sigmoid(x):
    return 1.0 / (1.0 + jnp.exp(-x))


def _softplus(x):
    return jnp.maximum(x, 0.0) + jnp.log(1.0 + jnp.exp(-jnp.abs(x)))


def _rms(x, g):
    return x * jax.lax.rsqrt(jnp.mean(x * x, axis=-1, keepdims=True) + RMS_EPS) * g


def _proj_kernel(*refs, tm, seq_len, has_vres, has_first):
    it = iter(refs)
    x_ref, gpre_ref, win_ref, mu_ref, lora_ref, dbias_ref, ibias_ref, wgate_ref = (next(it) for _ in range(8))
    kk_ref, ka_ref, rk_ref, ones_ref = (next(it) for _ in range(4))
    if has_vres:
        vfirst_ref, vbias_ref, wvup_ref = (next(it) for _ in range(3))
    if has_first:
        first_ref = next(it)
    (r_out, kp_out, v_out, kkn_out, a_out, ld_out, g_out, bonus_out, u_out, gates_out,
     last_out, p_scr) = (next(it) for _ in range(12))

    gates_end = D_SHIFT + D_POOL + 2 * x_ref.shape[1]
    i = pl.program_id(0)
    if seq_len >= tm:
        tiles_per_seq = seq_len // tm
        new_seq = (i % tiles_per_seq) == 0
    else:
        new_seq = i == 0

    @pl.when(new_seq)
    def _():
        p_scr[0:CARRY, :] = jnp.zeros((CARRY, D_SHIFT), F32)

    xn = _rms(x_ref[...], gpre_ref[...]).astype(BF16)
    p_scr[CARRY:CARRY + tm, :] = jnp.dot(xn, win_ref[:, 0:D_SHIFT], preferred_element_type=F32)

    if has_first:
        row = jax.lax.broadcasted_iota(jnp.int32, (tm, 1), 0)
        seq_start = jax.lax.rem(row, seq_len) == 0

    def mixed(c0, c1):
        ps = p_scr[CARRY:CARRY + tm, c0:c1]
        prev = p_scr[CARRY - 1:CARRY - 1 + tm, c0:c1]
        if has_first:
            prev = jnp.where(seq_start, first_ref[:, c0:c1], prev)
        return ps + (prev - ps) * mu_ref[:, c0:c1]

    o1, o2, o3 = D_RWKV, 2 * D_RWKV, 3 * D_RWKV
    o5 = o3 + LORA_DECAY + LORA_ICLR
    r = mixed(0, o1)
    k = mixed(o1, o2)
    v = mixed(o2, o3)
    xwa = mixed(o3, o5)
    xg = mixed(o5, D_SHIFT)

    lane = jax.lax.broadcasted_iota(jnp.int32, (1, LANES), 1)
    lora_in = jnp.where(lane < LORA_DECAY, jnp.tanh(xwa), xwa)
    lora = _dot(lora_in, lora_ref[...])
    log_w = -_softplus(-(dbias_ref[...] + lora[:, 0:D_RWKV])) - 0.5
    ld_out[...] = -jnp.exp(log_w)
    a = _sigmoid(ibias_ref[...] + lora[:, D_RWKV:2 * D_RWKV])
    a_out[...] = a
    g_out[...] = _dot(_sigmoid(xg), wgate_ref[...])

    if has_vres:
        vdown = jnp.dot(xn, win_ref[:, gates_end:], preferred_element_type=F32)
        v = v + (vfirst_ref[...] - v) * _sigmoid(vbias_ref[...] + _dot(vdown, wvup_ref[...]))
    v_out[...] = v

    kk = k * kk_ref[...]
    norm = jnp.sqrt(_dot_exact_rhs(kk * kk, ones_ref[...], 2))
    kkn_out[...] = kk / jnp.maximum(norm, 1e-12)
    kp = k * (1.0 + (a - 1.0) * ka_ref[...])
    kp_out[...] = kp
    r_out[...] = r
    bonus_out[...] = _dot_exact_rhs(r * kp * rk_ref[...], ones_ref[...], 2) * v

    u_out[...] = jnp.dot(xn, win_ref[:, D_SHIFT:D_SHIFT + D_POOL], preferred_element_type=F32)
    gates_out[...] = _sigmoid(jnp.dot(xn, win_ref[:, D_SHIFT + D_POOL:gates_end],
                                      preferred_element_type=F32))

    if has_first:
        last_out[...] = p_scr[CARRY:CARRY + tm, :]
    else:
        tail = p_scr[tm:tm + CARRY, :]
        last_out[...] = tail
        p_scr[0:CARRY, :] = tail


def _full(shape):
    nd = len(shape)
    return pl.BlockSpec(shape, lambda *_: (0,) * nd)


def _proj_call(x2d, seq_len, tm, lw, vfirst, first):
    n = x2d.shape[0]
    d_model = x2d.shape[1]
    has_vres = vfirst is not None
    has_first = first is not None
    nt = n // tm
    row = lambda w: pl.BlockSpec((tm, w), lambda i: (i, 0))
    ins = [x2d, lw['g_mix_pre'], lw['w_in'], lw['mu'], lw['lora'], lw['decay_bias'], lw['iclr_bias'],
           lw['w_gate_up'], lw['k_k'], lw['k_a'], lw['r_k'], lw['ones']]
    specs = [row(d_model)] + [_full(a.shape) for a in ins[1:]]
    if has_vres:
        ins += [vfirst, lw['vres_bias'], lw['w_vres_up']]
        specs += [row(D_RWKV), _full(lw['vres_bias'].shape), _full(lw['w_vres_up'].shape)]
    if has_first:
        ins.append(first)
        specs.append(row(D_SHIFT))
    sds = lambda w: jax.ShapeDtypeStruct((n, w), F32)
    last_rows = tm if has_first else CARRY
    out_shape = [sds(D_RWKV)] * 8 + [sds(D_POOL), sds(2 * d_model),
                                     jax.ShapeDtypeStruct((nt * last_rows, D_SHIFT), F32)]
    out_specs = [row(D_RWKV)] * 8 + [row(D_POOL), row(2 * d_model),
                                     pl.BlockSpec((last_rows, D_SHIFT), lambda i: (i, 0))]
    return pl.pallas_call(
        functools.partial(_proj_kernel, tm=tm, seq_len=seq_len, has_vres=has_vres, has_first=has_first),
        grid=(nt,), in_specs=specs, out_specs=out_specs, out_shape=out_shape,
        scratch_shapes=[pltpu.VMEM((tm + CARRY, D_SHIFT), F32)],
        compiler_params=pltpu.CompilerParams(dimension_semantics=("arbitrary",),
                                             vmem_limit_bytes=VMEM_LIMIT),
    )(*ins)


def _wkv_kernel(*refs, has_h0):
    it = iter(refs)
    r_ref, kp_ref, v_ref, kk_ref, a_ref, ld_ref = (next(it) for _ in range(6))
    if has_h0:
        h0_ref = next(it)
    o_ref, hout_ref, h_scr = (next(it) for _ in range(3))
    c_idx = pl.program_id(1)
    n_chunks = pl.num_programs(1)
    C = CHUNK

    lane = jax.lax.broadcasted_iota(jnp.int32, (1, PAIR), 1)
    left = lane < HEAD
    ri = jax.lax.broadcasted_iota(jnp.int32, (PAIR, PAIR), 0)
    ci = jax.lax.broadcasted_iota(jnp.int32, (PAIR, PAIR), 1)
    same = (ri >= HEAD) == (ci >= HEAD)
    strict = same & (ci < ri)
    incl = same & (ci <= ri)
    eye = ri == ci
    eye_f = jnp.where(eye, 1.0, 0.0).astype(F32)

    def stack(x):
        return jnp.concatenate([jnp.where(left, x, 0.0), jnp.where(left, 0.0, x)], axis=0)

    def twice(x):
        return jnp.concatenate([x, x], axis=0)

    @pl.when(c_idx == 0)
    def _():
        if has_h0:
            for p in range(N_PAIRS):
                h_scr[p] = stack(h0_ref[0, p])
        else:
            h_scr[...] = jnp.zeros(h_scr.shape, F32)

    ld = ld_ref[...]
    tr = jax.lax.broadcasted_iota(jnp.int32, (C, C), 0)
    tc = jax.lax.broadcasted_iota(jnp.int32, (C, C), 1)
    tri = jnp.where(tr >= tc, 1.0, 0.0).astype(BF16)
    cum = _dot_exact_lhs(tri, ld, 3)
    clast = cum[C - 1:C, :]
    e_cum = jnp.exp(cum)
    e_prev = jnp.exp(cum - ld)
    e_neg = jnp.exp(-cum)
    e_rem = jnp.exp(clast - cum)
    w_last = jnp.exp(clast)

    kk = kk_ref[...]
    b = kk * a_ref[...]
    kp = kp_ref[...]
    a_t = -(kk * e_prev)
    r_t = r_ref[...] * e_cum
    b_t = b * e_neg
    k_t = kp * e_neg
    b_h = b * e_rem
    k_h = kp * e_rem
    v = v_ref[...]

    pairs = range(N_PAIRS)
    sls = [slice(p * PAIR, (p + 1) * PAIR) for p in pairs]
    v_st = [stack(v[:, sl]) for sl in sls]
    sc = [_dot_nt(jnp.concatenate([a_t[:, sl], r_t[:, sl]], axis=0),
                  jnp.concatenate([stack(b_t[:, sl]), stack(k_t[:, sl])], axis=0)) for sl in sls]
    n_ab = [jnp.where(strict, twice(s[0:C, 0:PAIR]), 0.0) for s in sc]
    n_ak = [jnp.where(strict, twice(s[0:C, PAIR:2 * PAIR]), 0.0) for s in sc]
    n_rb = [jnp.where(incl, twice(s[C:2 * C, 0:PAIR]), 0.0) for s in sc]
    n_rk = [jnp.where(incl, twice(s[C:2 * C, PAIR:2 * PAIR]), 0.0) for s in sc]

    inv = [eye_f + n for n in n_ab]
    power = n_ab
    y = [_dot(n_ak[p], v_st[p]) for p in pairs]
    for _ in range(int(np.log2(C)) - 1):
        power = [_dot(m, m) for m in power]
        inv = [inv[p] + _dot(inv[p], power[p]) for p in pairs]

    au = [_dot(inv[p], jnp.concatenate([stack(a_t[:, sls[p]]), y[p]], axis=1)) for p in pairs]
    zero = jnp.zeros((PAIR, PAIR), F32)
    big = []
    for p in pairs:
        rhs = jnp.concatenate([au[p], jnp.concatenate([zero, v_st[p]], axis=1)], axis=0)
        lhs = jnp.concatenate(
            [jnp.concatenate([n_rb[p], n_rk[p]], axis=1),
             jnp.concatenate([stack(b_h[:, sls[p]]).T, stack(k_h[:, sls[p]]).T], axis=1)], axis=0)
        big.append(_dot(lhs, rhs))

    h = [h_scr[p] for p in pairs]
    sd = []
    for p in pairs:
        r_hat = stack(r_t[:, sls[p]]) + big[p][0:PAIR, 0:PAIR]
        m_low = big[p][PAIR:2 * PAIR, 0:PAIR]
        sd.append(_dot(jnp.concatenate([r_hat, m_low], axis=0), h[p]))
    for p in pairs:
        o_st = sd[p][0:PAIR] + big[p][0:PAIR, PAIR:2 * PAIR]
        o_ref[:, sls[p]] = o_st[0:C] + o_st[C:2 * C]
        w_col = jnp.sum(jnp.where(eye, jnp.broadcast_to(w_last[:, sls[p]], (PAIR, PAIR)), 0.0),
                        axis=1, keepdims=True)
        h_scr[p] = w_col * h[p] + sd[p][PAIR:2 * PAIR] + big[p][PAIR:2 * PAIR, PAIR:2 * PAIR]

    @pl.when(c_idx == n_chunks - 1)
    def _():
        for p in range(N_PAIRS):
            hout_ref[0, p] = h_scr[p, 0:HEAD, :] + h_scr[p, HEAD:PAIR, :]


def _wkv_call(ops, n_seq, seq_len, h0):
    nc = seq_len // CHUNK
    has_h0 = h0 is not None
    blk = pl.BlockSpec((CHUNK, D_RWKV), lambda b, c: (b * nc + c, 0))
    hspec = pl.BlockSpec((1, N_PAIRS, HEAD, PAIR), lambda b, c: (b, 0, 0, 0))
    ins = list(ops)
    specs = [blk] * 6
    if has_h0:
        ins.append(h0)
        specs.append(hspec)
    return pl.pallas_call(
        functools.partial(_wkv_kernel, has_h0=has_h0),
        grid=(n_seq, nc), in_specs=specs, out_specs=[blk, hspec],
        out_shape=[jax.ShapeDtypeStruct((n_seq * seq_len, D_RWKV), F32),
                   jax.ShapeDtypeStruct((n_seq, N_PAIRS, HEAD, PAIR), F32)],
        scratch_shapes=[pltpu.VMEM((N_PAIRS, PAIR, PAIR), F32)],
        compiler_params=pltpu.CompilerParams(dimension_semantics=("arbitrary", "arbitrary"),
                                             vmem_limit_bytes=VMEM_LIMIT),
    )(*ins)


def _merge_kernel(*refs, tm, tiles_per_seq, zero_first_hist):
    (o_ref, bonus_ref, g_ref, u_ref, hist_ref, gates_ref, x_ref, band_ref, invc_ref, lnw_ref, lnb_ref,
     ones_ref, pmap_ref, pscale_ref, wba_ref, wbb_ref, wout_ref, gpost_ref, y_ref) = refs
    d_model = x_ref.shape[1]
    inv_h = 1.0 / HEAD

    o = o_ref[...]
    mean = _dot_exact_rhs(o, ones_ref[...], 2) * inv_h
    cen = o - mean
    var = _dot_exact_rhs(cen * cen, ones_ref[...], 2) * inv_h
    o_n = cen * jax.lax.rsqrt(var + GN_EPS) * lnw_ref[...] + lnb_ref[...]
    o_rwkv = (o_n + bonus_ref[...]) * g_ref[...]

    u = u_ref[...]
    hist = hist_ref[...]
    if zero_first_hist:
        keep = jnp.where((pl.program_id(0) % tiles_per_seq) == 0, 0.0, 1.0)
        hist = hist * keep
    full = jnp.concatenate([hist, u], axis=0)
    lane = jax.lax.broadcasted_iota(jnp.int32, (1, D_POOL), 1)
    win_sum = jnp.zeros((tm, D_POOL), F32)
    for gi in range(len(POOL_WINDOWS)):
        s = _dot_exact_lhs(band_ref[gi], full, 2)
        in_group = (lane >= gi * POOL_GROUP) & (lane < (gi + 1) * POOL_GROUP)
        win_sum = jnp.where(in_group, s, win_sum)
    diff = win_sum * invc_ref[...] - u
    o_pool = _dot(diff, pmap_ref[...]) * pscale_ref[...]

    gates = gates_ref[...]
    merged = (gates[:, 0:d_model] * _dot(o_rwkv, wba_ref[...])
              + gates[:, d_model:2 * d_model] * _dot(o_pool, wbb_ref[...]))
    y_ref[...] = x_ref[...] + _rms(_dot(merged, wout_ref[...]), gpost_ref[...])


def _merge_call(o, bonus, g, u, hist, hist_rows, gates, x2d, seq_len, tm, band, invc, lw, zero_first_hist):
    n, d_model = x2d.shape
    nt = n // tm
    tiles_per_seq = max(seq_len // tm, 1)
    row = lambda w: pl.BlockSpec((tm, w), lambda i: (i, 0))
    if zero_first_hist:
        per = tm // hist_rows
        hist_spec = pl.BlockSpec((hist_rows, D_POOL), lambda i: (jnp.maximum(i * per - 1, 0), 0))
        invc_spec = pl.BlockSpec((tm, D_POOL), lambda i: (i % tiles_per_seq, 0))
    else:
        hist_spec = pl.BlockSpec((hist_rows, D_POOL), lambda i: (i, 0))
        invc_spec = _full(invc.shape)
    consts = [lw['ln_x_w'], lw['ln_x_b'], lw['ones'], lw['pool_map'], lw['pool_scale'],
              lw['w_branch_rwkv'], lw['w_branch_pool'], lw['w_out'], lw['g_mix_post']]
    ins = [o, bonus, g, u, hist, gates, x2d, band, invc] + consts
    specs = ([row(D_RWKV)] * 3 + [row(D_POOL), hist_spec, row(2 * d_model), row(d_model),
                                  _full(band.shape), invc_spec] + [_full(a.shape) for a in consts])
    return pl.pallas_call(
        functools.partial(_merge_kernel, tm=tm, tiles_per_seq=tiles_per_seq, zero_first_hist=zero_first_hist),
        grid=(nt,), in_specs=specs, out_specs=row(d_model),
        out_shape=jax.ShapeDtypeStruct((n, d_model), F32),
        compiler_params=pltpu.CompilerParams(dimension_semantics=("arbitrary",),
                                             vmem_limit_bytes=VMEM_LIMIT),
    )(*ins)


def _ffn_kernel(x_ref, gpre_ref, win_ref, wout_ref, gpost_ref, y_ref):
    x = x_ref[...]
    d_ff = wout_ref.shape[0]
    h = _rms(x, gpre_ref[...]).astype(BF16)
    gu = jnp.dot(h, win_ref[...], preferred_element_type=F32)
    gt = gu[:, 0:d_ff]
    act = gt * _sigmoid(gt) * gu[:, d_ff:2 * d_ff]
    y_ref[...] = x + _rms(_dot(act, wout_ref[...]), gpost_ref[...])


def _ffn_call(x2d, tm, lw):
    n, d_model = x2d.shape
    row = pl.BlockSpec((tm, d_model), lambda i: (i, 0))
    consts = [lw['g_ffn_pre'], lw['w_ffn_in'], lw['w_ffn_out'], lw['g_ffn_post']]
    return pl.pallas_call(
        _ffn_kernel, grid=(n // tm,), in_specs=[row] + [_full(a.shape) for a in consts],
        out_specs=row, out_shape=jax.ShapeDtypeStruct((n, d_model), F32),
        compiler_params=pltpu.CompilerParams(dimension_semantics=("arbitrary",),
                                             vmem_limit_bytes=VMEM_LIMIT),
    )(x2d, *consts)


def _prompt_pool_consts(tm, seq_len, hist_rows):
    t = np.arange(tm)[:, None]
    j = np.arange(hist_rows + tm)[None, :]
    dist = t + hist_rows - j
    band = np.stack([(dist >= 0) & (dist < w) for w in POOL_WINDOWS]).astype(np.float32)
    pos = np.arange(seq_len)[:, None]
    win = np.repeat(np.array(POOL_WINDOWS), POOL_GROUP)[None, :]
    invc = 1.0 / np.minimum(pos + 1, win).astype(np.float32)
    return jnp.asarray(band, BF16), jnp.asarray(invc, F32)


def _sample_pool_consts(tm, seq_len):
    ns = tm // seq_len
    rs, rt = np.divmod(np.arange(tm), seq_len)
    hs, hj = np.divmod(np.arange(ns * POOL_SLOTS), POOL_SLOTS)
    cs = np.concatenate([hs, rs])[None, :]
    cpos = np.concatenate([hj - 1, POOL_BUF + rt])[None, :]
    dist = (POOL_BUF + rt)[:, None] - cpos
    same = (rs[:, None] == cs) & (cpos >= 0)
    band = np.stack([same & (dist >= 0) & (dist < w) for w in POOL_WINDOWS]).astype(np.float32)
    win = np.repeat(np.array(POOL_WINDOWS), POOL_GROUP)[None, :]
    pos = (SAMPLE_START_POS + rt)[:, None]
    invc = 1.0 / np.minimum(pos + 1, win).astype(np.float32)
    return jnp.asarray(band, BF16), jnp.asarray(invc, F32)


def _block_diag(blocks):
    n = len(blocks)
    rows = []
    for i, blk in enumerate(blocks):
        rows.append(jnp.concatenate(
            [blk if j == i else jnp.zeros((blk.shape[0], blocks[j].shape[1]), blk.dtype) for j in range(n)],
            axis=1))
    return jnp.concatenate(rows, axis=0)


def _layer_weights(l, w):
    d_model = w['w_in'].shape[1]
    row = lambda a: a.reshape(1, -1).astype(F32)
    w_in = w['w_in'][l]
    lw = {
        'g_mix_pre': row(w['norm_mix_pre'][l]), 'g_mix_post': row(w['norm_mix_post'][l]),
        'g_ffn_pre': row(w['norm_ffn_pre'][l]), 'g_ffn_post': row(w['norm_ffn_post'][l]),
        'mu': row(w['mu_shift'][l]), 'decay_bias': row(w['decay_bias'][l]), 'iclr_bias': row(w['iclr_bias'][l]),
        'lora': _block_diag([w['w_decay_up'][l], w['w_iclr_up'][l]]).astype(BF16),
        'w_gate_up': w['w_gate_up'][l].astype(BF16),
        'k_k': row(w['k_k'][l]), 'k_a': row(w['k_a'][l]), 'r_k': row(w['r_k'][l]),
        'ln_x_w': row(w['ln_x_w'][l]), 'ln_x_b': row(w['ln_x_b'][l]),
        'pool_map': _block_diag([w['pool_map'][l, gi] for gi in range(len(POOL_WINDOWS))]).astype(BF16),
        'pool_scale': row(w['pool_scale'][l]),
        'w_branch_rwkv': w['w_branch_rwkv'][l].astype(BF16), 'w_branch_pool': w['w_branch_pool'][l].astype(BF16),
        'w_out': w['w_out'][l].astype(BF16),
        'w_ffn_in': w['w_ffn_in'][l].astype(BF16), 'w_ffn_out': w['w_ffn_out'][l].astype(BF16),
        'ones': jnp.asarray(np.kron(np.eye(N_HEADS), np.ones((HEAD, HEAD))), BF16),
    }
    if l > 0:
        pad = jnp.zeros((d_model, LANES - LORA_VRES), F32)
        w_in = jnp.concatenate([w_in, w['w_vres_down'][l - 1], pad], axis=1)
        lw['vres_bias'] = row(w['vres_bias'][l - 1])
        lw['w_vres_up'] = jnp.concatenate(
            [w['w_vres_up'][l - 1], jnp.zeros((LANES - LORA_VRES, D_RWKV), F32)], axis=0).astype(BF16)
    lw['w_in'] = w_in.astype(BF16)
    return lw


def _to_pairs(s):
    b = s.shape[0]
    return s.reshape(b, N_PAIRS, 2, HEAD, HEAD).transpose(0, 1, 4, 2, 3).reshape(b, N_PAIRS, HEAD, PAIR)


def _from_pairs(h):
    b = h.shape[0]
    return h.reshape(b, N_PAIRS, HEAD, 2, HEAD).transpose(0, 1, 3, 4, 2).reshape(b, N_HEADS, HEAD, HEAD)


def _tile_rows(n_seq, seq_len, target):
    if seq_len >= target:
        return target
    return min(n_seq * seq_len, target)


def kernel(x_prompt, x_sample, state_wkv, state_shift, state_pool, norm_mix_pre, norm_mix_post, norm_ffn_pre, norm_ffn_post, w_in, mu_shift, decay_bias, w_decay_up, iclr_bias, w_iclr_up, w_gate_up, k_k, k_a, r_k, ln_x_w, ln_x_b, vres_bias, w_vres_down, w_vres_up, pool_map, pool_scale, w_branch_rwkv, w_branch_pool, w_out, w_ffn_in, w_ffn_out):
    weights = dict(norm_mix_pre=norm_mix_pre, norm_mix_post=norm_mix_post, norm_ffn_pre=norm_ffn_pre,
                   norm_ffn_post=norm_ffn_post, w_in=w_in, mu_shift=mu_shift, decay_bias=decay_bias,
                   w_decay_up=w_decay_up, iclr_bias=iclr_bias, w_iclr_up=w_iclr_up, w_gate_up=w_gate_up,
                   k_k=k_k, k_a=k_a, r_k=r_k, ln_x_w=ln_x_w, ln_x_b=ln_x_b, vres_bias=vres_bias,
                   w_vres_down=w_vres_down, w_vres_up=w_vres_up, pool_map=pool_map, pool_scale=pool_scale,
                   w_branch_rwkv=w_branch_rwkv, w_branch_pool=w_branch_pool, w_out=w_out,
                   w_ffn_in=w_ffn_in, w_ffn_out=w_ffn_out)
    depth = w_in.shape[0]
    bp, tp, d_model = x_prompt.shape
    bs, ts, _ = x_sample.shape
    tm_p = _tile_rows(bp, tp, 256)
    tm_s = _tile_rows(bs, ts, 256)
    hist_p = LANES
    band_p, invc_p = _prompt_pool_consts(tm_p, tp, hist_p)
    band_s, invc_s = _sample_pool_consts(tm_s, ts)
    ts_pad = -(-ts // CHUNK) * CHUNK

    yp = x_prompt.reshape(bp * tp, d_model)
    ys = x_sample.reshape(bs * ts, d_model)
    vf_p = vf_s = None
    outs = {k: [] for k in ('wkv_p', 'shift_p', 'pool_p', 'wkv_s', 'shift_s', 'pool_s')}
    for l in range(depth):
        lw = _layer_weights(l, weights)

        (r, kp, v, kkn, a, ld, g, bonus, u, gates, last) = _proj_call(yp, tp, tm_p, lw, vf_p, None)
        if l == 0:
            vf_p = v
        o, hout = _wkv_call((r, kp, v, kkn, a, ld), bp, tp, None)
        x1 = _merge_call(o, bonus, g, u, u, hist_p, gates, yp, tp, tm_p, band_p, invc_p, lw, True)
        yp = _ffn_call(x1, tm_p, lw)
        outs['wkv_p'].append(_from_pairs(hout))
        tiles = tp // tm_p
        outs['shift_p'].append(last.reshape(bp, tiles, CARRY, D_SHIFT)[:, -1, -1])
        outs['pool_p'].append(u.reshape(bp, tp, D_POOL)[:, -POOL_BUF:])

        first = jnp.zeros((bs, ts, D_SHIFT), F32).at[:, 0].set(state_shift[l]).reshape(bs * ts, D_SHIFT)
        (r, kp, v, kkn, a, ld, g, bonus, u, gates, ps_all) = _proj_call(ys, ts, tm_s, lw, vf_s, first)
        if l == 0:
            vf_s = v
        pad = lambda z: jnp.pad(z.reshape(bs, ts, D_RWKV), ((0, 0), (0, ts_pad - ts), (0, 0))).reshape(
            bs * ts_pad, D_RWKV)
        o_pad, hout = _wkv_call(tuple(pad(z) for z in (r, kp, v, kkn, a, ld)), bs, ts_pad,
                                _to_pairs(state_wkv[l]))
        o = o_pad.reshape(bs, ts_pad, D_RWKV)[:, :ts].reshape(bs * ts, D_RWKV)
        hist = jnp.pad(state_pool[l], ((0, 0), (1, 0), (0, 0))).reshape(bs * POOL_SLOTS, D_POOL)
        x1 = _merge_call(o, bonus, g, u, hist, (tm_s // ts) * POOL_SLOTS, gates, ys, ts, tm_s, band_s, invc_s,
                         lw, False)
        ys = _ffn_call(x1, tm_s, lw)
        outs['wkv_s'].append(_from_pairs(hout))
        outs['shift_s'].append(ps_all.reshape(bs, ts, D_SHIFT)[:, -1])
        outs['pool_s'].append(jnp.concatenate(
            [state_pool[l], u.reshape(bs, ts, D_POOL)], axis=1)[:, -POOL_BUF:])

    return (yp.reshape(bp, tp, d_model), ys.reshape(bs, ts, d_model),
            jnp.stack(outs['wkv_p']), jnp.stack(outs['shift_p']), jnp.stack(outs['pool_p']),
            jnp.stack(outs['wkv_s']), jnp.stack(outs['shift_s']), jnp.stack(outs['pool_s']))
```

```python
import functools

import numpy as np
import jax
import jax.numpy as jnp
from jax.experimental import pallas as pl
from jax.experimental.pallas import tpu as pltpu

F32 = jnp.float32
BF16 = jnp.bfloat16

HEAD = 64
D_RWKV = 768
N_HEADS = D_RWKV // HEAD
PAIR = 2 * HEAD
N_PAIRS = D_RWKV // PAIR
D_POOL = 256
POOL_WINDOWS = (2, 4, 8, 16)
POOL_GROUP = D_POOL // len(POOL_WINDOWS)
POOL_BUF = max(POOL_WINDOWS) - 1
POOL_SLOTS = POOL_BUF + 1
LORA_DECAY = 64
LORA_ICLR = 64
LORA_GATE = 128
LORA_VRES = 32
D_SHIFT = 3 * D_RWKV + LORA_DECAY + LORA_ICLR + LORA_GATE
SAMPLE_START_POS = 16384
RMS_EPS = 1e-6
GN_EPS = 1e-5 * HEAD
CHUNK = 64
WKV_SUB = 4
POOL_HIST = 32
CARRY = 8
LANES = 128
VMEM_LIMIT = 56 * 1024 * 1024


def _dot(a, b):
    return jnp.dot(a.astype(BF16), b.astype(BF16), preferred_element_type=F32)


def _dot_nt(a, b):
    return jax.lax.dot_general(a.astype(BF16), b.astype(BF16), (((1,), (1,)), ((), ())),
                               preferred_element_type=F32)


def _split(x, parts):
    out = []
    for _ in range(parts - 1):
        hi = x.astype(BF16)
        out.append(hi)
        x = x - hi.astype(F32)
    out.append(x.astype(BF16))
    return out


def _dot_exact_lhs(sel, x, parts):
    acc = None
    for p in _split(x, parts):
        t = jnp.dot(sel, p, preferred_element_type=F32)
        acc = t if acc is None else acc + t
    return acc


def _sigmoid(x):
    return 1.0 / (1.0 + jnp.exp(-x))


def _softplus(x):
    return jnp.maximum(x, 0.0) + jnp.log(1.0 + jnp.exp(-jnp.abs(x)))


def _rms(x, g):
    return x * jax.lax.rsqrt(jnp.mean(x * x, axis=-1, keepdims=True) + RMS_EPS) * g


def _proj_kernel(*refs, tm, seq_len, has_vres, has_first):
    it = iter(refs)
    x_ref, gpre_ref, win_ref, mu_ref, lora_ref, dbias_ref, ibias_ref, wgate_ref = (next(it) for _ in range(8))
    kk_ref, ka_ref, rk_ref, ones_ref = (next(it) for _ in range(4))
    if has_vres:
        vfirst_ref, vbias_ref, wvup_ref = (next(it) for _ in range(3))
    if has_first:
        first_ref = next(it)
    (r_out, kp_out, v_out, kkn_out, a_out, ld_out, g_out, bonus_out, u_out, gates_out,
     last_out, p_scr) = (next(it) for _ in range(12))

    gates_end = D_SHIFT + D_POOL + 2 * x_ref.shape[1]
    i = pl.program_id(0)
    if seq_len >= tm:
        tiles_per_seq = seq_len // tm
        new_seq = (i % tiles_per_seq) == 0
    else:
        new_seq = i == 0

    @pl.when(new_seq)
    def _():
        p_scr[0:CARRY, :] = jnp.zeros((CARRY, D_SHIFT), F32)

    xn = _rms(x_ref[...], gpre_ref[...]).astype(BF16)
    p_scr[CARRY:CARRY + tm, :] = jnp.dot(xn, win_ref[:, 0:D_SHIFT], preferred_element_type=F32)

    if has_first:
        row = jax.lax.broadcasted_iota(jnp.int32, (tm, 1), 0)
        seq_start = jax.lax.rem(row, seq_len) == 0

    def mixed(c0, c1):
        ps = p_scr[CARRY:CARRY + tm, c0:c1]
        prev = p_scr[CARRY - 1:CARRY - 1 + tm, c0:c1]
        if has_first:
            prev = jnp.where(seq_start, first_ref[:, c0:c1], prev)
        return ps + (prev - ps) * mu_ref[:, c0:c1]

    o1, o2, o3 = D_RWKV, 2 * D_RWKV, 3 * D_RWKV
    o5 = o3 + LORA_DECAY + LORA_ICLR
    r = mixed(0, o1)
    k = mixed(o1, o2)
    v = mixed(o2, o3)
    xwa = mixed(o3, o5)
    xg = mixed(o5, D_SHIFT)

    lane = jax.lax.broadcasted_iota(jnp.int32, (1, LANES), 1)
    lora_in = jnp.where(lane < LORA_DECAY, jnp.tanh(xwa), xwa)
    lora = _dot(lora_in, lora_ref[...])
    log_w = -_softplus(-(dbias_ref[...] + lora[:, 0:D_RWKV])) - 0.5
    ld_out[...] = -jnp.exp(log_w)
    a = _sigmoid(ibias_ref[...] + lora[:, D_RWKV:2 * D_RWKV])
    a_out[...] = a
    g_out[...] = _dot(_sigmoid(xg), wgate_ref[...])

    if has_vres:
        vdown = jnp.dot(xn, win_ref[:, gates_end:], preferred_element_type=F32)
        v = v + (vfirst_ref[...] - v) * _sigmoid(vbias_ref[...] + _dot(vdown, wvup_ref[...]))
    v_out[...] = v

    kk = k * kk_ref[...]
    norm = jnp.sqrt(_dot(kk * kk, ones_ref[...]))
    kkn_out[...] = kk / jnp.maximum(norm, 1e-12)
    kp = k * (1.0 + (a - 1.0) * ka_ref[...])
    kp_out[...] = kp
    r_out[...] = r
    bonus_out[...] = _dot(r * kp * rk_ref[...], ones_ref[...]) * v

    u_out[...] = jnp.dot(xn, win_ref[:, D_SHIFT:D_SHIFT + D_POOL], preferred_element_type=F32)
    gates_out[...] = _sigmoid(jnp.dot(xn, win_ref[:, D_SHIFT + D_POOL:gates_end],
                                      preferred_element_type=F32))

    if has_first:
        last_out[...] = p_scr[CARRY:CARRY + tm, :]
    else:
        tail = p_scr[tm:tm + CARRY, :]
        last_out[...] = tail
        p_scr[0:CARRY, :] = tail


def _full(shape):
    nd = len(shape)
    return pl.BlockSpec(shape, lambda *_: (0,) * nd)


def _proj_call(x2d, seq_len, tm, lw, vfirst, first):
    n = x2d.shape[0]
    d_model = x2d.shape[1]
    has_vres = vfirst is not None
    has_first = first is not None
    nt = n // tm
    row = lambda w: pl.BlockSpec((tm, w), lambda i: (i, 0))
    ins = [x2d, lw['g_mix_pre'], lw['w_in'], lw['mu'], lw['lora'], lw['decay_bias'], lw['iclr_bias'],
           lw['w_gate_up'], lw['k_k'], lw['k_a'], lw['r_k'], lw['ones']]
    specs = [row(d_model)] + [_full(a.shape) for a in ins[1:]]
    if has_vres:
        ins += [vfirst, lw['vres_bias'], lw['w_vres_up']]
        specs += [row(D_RWKV), _full(lw['vres_bias'].shape), _full(lw['w_vres_up'].shape)]
    if has_first:
        ins.append(first)
        specs.append(row(D_SHIFT))
    sds = lambda w: jax.ShapeDtypeStruct((n, w), F32)
    last_rows = tm if has_first else CARRY
    out_shape = [sds(D_RWKV)] * 8 + [sds(D_POOL), sds(2 * d_model),
                                     jax.ShapeDtypeStruct((nt * last_rows, D_SHIFT), F32)]
    out_specs = [row(D_RWKV)] * 8 + [row(D_POOL), row(2 * d_model),
                                     pl.BlockSpec((last_rows, D_SHIFT), lambda i: (i, 0))]
    return pl.pallas_call(
        functools.partial(_proj_kernel, tm=tm, seq_len=seq_len, has_vres=has_vres, has_first=has_first),
        grid=(nt,), in_specs=specs, out_specs=out_specs, out_shape=out_shape,
        scratch_shapes=[pltpu.VMEM((tm + CARRY, D_SHIFT), F32)],
        compiler_params=pltpu.CompilerParams(dimension_semantics=("arbitrary",),
                                             vmem_limit_bytes=VMEM_LIMIT),
    )(*ins)


def _wkv_kernel(*refs, has_h0, n_sub):
    it = iter(refs)
    r_ref, kp_ref, v_ref, kk_ref, a_ref, ld_ref = (next(it) for _ in range(6))
    if has_h0:
        h0_ref = next(it)
    o_ref, hout_ref, h_scr = (next(it) for _ in range(3))
    c_idx = pl.program_id(1)
    n_steps = pl.num_programs(1)
    C = CHUNK

    lane = jax.lax.broadcasted_iota(jnp.int32, (1, PAIR), 1)
    left = lane < HEAD
    ri = jax.lax.broadcasted_iota(jnp.int32, (PAIR, PAIR), 0)
    ci = jax.lax.broadcasted_iota(jnp.int32, (PAIR, PAIR), 1)
    same = (ri >= HEAD) == (ci >= HEAD)
    strict = same & (ci < ri)
    incl = same & (ci <= ri)
    eye = ri == ci
    eye_f = jnp.where(eye, 1.0, 0.0).astype(F32)

    def stack(x):
        return jnp.concatenate([jnp.where(left, x, 0.0), jnp.where(left, 0.0, x)], axis=0)

    def twice(x):
        return jnp.concatenate([x, x], axis=0)

    @pl.when(c_idx == 0)
    def _():
        if has_h0:
            for p in range(N_PAIRS):
                h_scr[p] = stack(h0_ref[0, p])
        else:
            h_scr[...] = jnp.zeros(h_scr.shape, F32)

    tr = jax.lax.broadcasted_iota(jnp.int32, (C, C), 0)
    tc = jax.lax.broadcasted_iota(jnp.int32, (C, C), 1)
    tri = jnp.where(tr >= tc, 1.0, 0.0).astype(BF16)

    a_t, r_t, b_t, k_t, b_h, k_h, v_st, w_last = ([] for _ in range(8))
    for j in range(n_sub):
        rows = slice(j * C, (j + 1) * C)
        ld = ld_ref[rows, :]
        cum = _dot_exact_lhs(tri, ld, 3)
        clast = cum[C - 1:C, :]
        e_cum = jnp.exp(cum)
        e_prev = jnp.exp(cum - ld)
        e_neg = jnp.exp(-cum)
        e_rem = jnp.exp(clast - cum)
        wl = jnp.exp(clast)
        kk = kk_ref[rows, :]
        b = kk * a_ref[rows, :]
        kp = kp_ref[rows, :]
        full = (-(kk * e_prev), r_ref[rows, :] * e_cum, b * e_neg, kp * e_neg, b * e_rem, kp * e_rem)
        v = v_ref[rows, :]
        for p in range(N_PAIRS):
            sl = slice(p * PAIR, (p + 1) * PAIR)
            for dst, src in zip((a_t, r_t, b_t, k_t, b_h, k_h), full):
                dst.append(src[:, sl])
            v_st.append(stack(v[:, sl]))
            w_last.append(wl[:, sl])

    units = range(n_sub * N_PAIRS)
    sc = [_dot_nt(jnp.concatenate([a_t[q], r_t[q]], axis=0),
                  jnp.concatenate([stack(b_t[q]), stack(k_t[q])], axis=0)) for q in units]
    n_ab = [jnp.where(strict, twice(s[0:C, 0:PAIR]), 0.0) for s in sc]
    n_ak = [jnp.where(strict, twice(s[0:C, PAIR:2 * PAIR]), 0.0) for s in sc]
    n_rb = [jnp.where(incl, twice(s[C:2 * C, 0:PAIR]), 0.0) for s in sc]
    n_rk = [jnp.where(incl, twice(s[C:2 * C, PAIR:2 * PAIR]), 0.0) for s in sc]

    inv = [eye_f + n for n in n_ab]
    power = n_ab
    y = [_dot(n_ak[q], v_st[q]) for q in units]
    for _ in range(int(np.log2(C)) - 1):
        power = [_dot(m, m) for m in power]
        inv = [inv[q] + _dot(inv[q], power[q]) for q in units]

    au = [_dot(inv[q], jnp.concatenate([stack(a_t[q]), y[q]], axis=1)) for q in units]
    zero = jnp.zeros((PAIR, PAIR), F32)
    big = []
    for q in units:
        rhs = jnp.concatenate([au[q], jnp.concatenate([zero, v_st[q]], axis=1)], axis=0)
        lhs = jnp.concatenate(
            [jnp.concatenate([n_rb[q], n_rk[q]], axis=1),
             jnp.concatenate([stack(b_h[q]).T, stack(k_h[q]).T], axis=1)], axis=0)
        big.append(_dot(lhs, rhs))

    h = [h_scr[p] for p in range(N_PAIRS)]
    for j in range(n_sub):
        rows = slice(j * C, (j + 1) * C)
        qs = [j * N_PAIRS + p for p in range(N_PAIRS)]
        sd = []
        for p, q in enumerate(qs):
            r_hat = stack(r_t[q]) + big[q][0:PAIR, 0:PAIR]
            m_low = big[q][PAIR:2 * PAIR, 0:PAIR]
            sd.append(_dot(jnp.concatenate([r_hat, m_low], axis=0), h[p]))
        for p, q in enumerate(qs):
            o_st = sd[p][0:PAIR] + big[q][0:PAIR, PAIR:2 * PAIR]
            o_ref[rows, p * PAIR:(p + 1) * PAIR] = o_st[0:C] + o_st[C:2 * C]
            w_col = jnp.sum(jnp.where(eye, jnp.broadcast_to(w_last[q], (PAIR, PAIR)), 0.0),
                            axis=1, keepdims=True)
            h[p] = w_col * h[p] + sd[p][PAIR:2 * PAIR] + big[q][PAIR:2 * PAIR, PAIR:2 * PAIR]
    for p in range(N_PAIRS):
        h_scr[p] = h[p]

    @pl.when(c_idx == n_steps - 1)
    def _():
        for p in range(N_PAIRS):
            hout_ref[0, p] = h_scr[p, 0:HEAD, :] + h_scr[p, HEAD:PAIR, :]


def _wkv_call(ops, n_seq, seq_len, h0):
    n_sub = WKV_SUB if seq_len % (WKV_SUB * CHUNK) == 0 else 1
    rows = n_sub * CHUNK
    ns = seq_len // rows
    has_h0 = h0 is not None
    blk = pl.BlockSpec((rows, D_RWKV), lambda b, c: (b * ns + c, 0))
    hspec = pl.BlockSpec((1, N_PAIRS, HEAD, PAIR), lambda b, c: (b, 0, 0, 0))
    ins = list(ops)
    specs = [blk] * 6
    if has_h0:
        ins.append(h0)
        specs.append(hspec)
    return pl.pallas_call(
        functools.partial(_wkv_kernel, has_h0=has_h0, n_sub=n_sub),
        grid=(n_seq, ns), in_specs=specs, out_specs=[blk, hspec],
        out_shape=[jax.ShapeDtypeStruct((n_seq * seq_len, D_RWKV), F32),
                   jax.ShapeDtypeStruct((n_seq, N_PAIRS, HEAD, PAIR), F32)],
        scratch_shapes=[pltpu.VMEM((N_PAIRS, PAIR, PAIR), F32)],
        compiler_params=pltpu.CompilerParams(dimension_semantics=("arbitrary", "arbitrary"),
                                             vmem_limit_bytes=VMEM_LIMIT),
    )(*ins)


def _merge_kernel(*refs, tm, tiles_per_seq, banded):
    it = iter(refs)
    o_ref, bonus_ref, g_ref, u_ref, hist_ref, gates_ref, x_ref = (next(it) for _ in range(7))
    if banded:
        band_ref = next(it)
    (invc_ref, lnw_ref, lnb_ref, ones_ref, pmap_ref, pscale_ref, wba_ref, wbb_ref, wout_ref, gpost_ref,
     y_ref) = (next(it) for _ in range(11))
    if not banded:
        s_a, s_b = next(it), next(it)
    d_model = x_ref.shape[1]
    inv_h = 1.0 / HEAD

    o = o_ref[...]
    mean = _dot(o, ones_ref[...]) * inv_h
    cen = o - mean
    var = _dot(cen * cen, ones_ref[...]) * inv_h
    o_n = cen * jax.lax.rsqrt(var + GN_EPS) * lnw_ref[...] + lnb_ref[...]
    o_rwkv = (o_n + bonus_ref[...]) * g_ref[...]

    u = u_ref[...]
    lane = jax.lax.broadcasted_iota(jnp.int32, (1, D_POOL), 1)
    win_sum = jnp.zeros((tm, D_POOL), F32)
    if banded:
        full = jnp.concatenate([hist_ref[...], u], axis=0)
        for gi in range(len(POOL_WINDOWS)):
            s = _dot_exact_lhs(band_ref[gi], full, 2)
            in_group = (lane >= gi * POOL_GROUP) & (lane < (gi + 1) * POOL_GROUP)
            win_sum = jnp.where(in_group, s, win_sum)
    else:
        keep = jnp.where((pl.program_id(0) % tiles_per_seq) == 0, 0.0, 1.0)
        s_a[0:POOL_HIST, :] = hist_ref[...] * keep
        s_a[POOL_HIST:POOL_HIST + tm, :] = u
        src, dst = s_a, s_b
        total = POOL_HIST + tm
        w = 1
        for gi, win in enumerate(POOL_WINDOWS):
            while w < win:
                lo = CARRY * int(np.log2(2 * w))
                dst[lo:total, :] = src[lo:total, :] + src[lo - w:total - w, :]
                src, dst = dst, src
                w *= 2
            in_group = (lane >= gi * POOL_GROUP) & (lane < (gi + 1) * POOL_GROUP)
            win_sum = jnp.where(in_group, src[POOL_HIST:total, :], win_sum)
    diff = win_sum * invc_ref[...] - u
    o_pool = _dot(diff, pmap_ref[...]) * pscale_ref[...]

    gates = gates_ref[...]
    merged = (gates[:, 0:d_model] * _dot(o_rwkv, wba_ref[...])
              + gates[:, d_model:2 * d_model] * _dot(o_pool, wbb_ref[...]))
    y_ref[...] = x_ref[...] + _rms(_dot(merged, wout_ref[...]), gpost_ref[...])


def _merge_call(o, bonus, g, u, hist, hist_rows, gates, x2d, seq_len, tm, band, invc, lw):
    n, d_model = x2d.shape
    nt = n // tm
    tiles_per_seq = max(seq_len // tm, 1)
    banded = band is not None
    row = lambda w: pl.BlockSpec((tm, w), lambda i: (i, 0))
    if banded:
        hist_spec = pl.BlockSpec((hist_rows, D_POOL), lambda i: (i, 0))
        invc_spec = _full(invc.shape)
        scratch = []
    else:
        per = tm // hist_rows
        hist_spec = pl.BlockSpec((hist_rows, D_POOL), lambda i: (jnp.maximum(i * per - 1, 0), 0))
        invc_spec = pl.BlockSpec((tm, D_POOL), lambda i: (i % tiles_per_seq, 0))
        scratch = [pltpu.VMEM((hist_rows + tm, D_POOL), F32)] * 2
    consts = [lw['ln_x_w'], lw['ln_x_b'], lw['ones'], lw['pool_map'], lw['pool_scale'],
              lw['w_branch_rwkv'], lw['w_branch_pool'], lw['w_out'], lw['g_mix_post']]
    ins = [o, bonus, g, u, hist, gates, x2d] + ([band] if banded else []) + [invc] + consts
    specs = ([row(D_RWKV)] * 3 + [row(D_POOL), hist_spec, row(2 * d_model), row(d_model)]
             + ([_full(band.shape)] if banded else []) + [invc_spec] + [_full(a.shape) for a in consts])
    return pl.pallas_call(
        functools.partial(_merge_kernel, tm=tm, tiles_per_seq=tiles_per_seq, banded=banded),
        grid=(nt,), in_specs=specs, out_specs=row(d_model),
        out_shape=jax.ShapeDtypeStruct((n, d_model), F32), scratch_shapes=scratch,
        compiler_params=pltpu.CompilerParams(dimension_semantics=("arbitrary",),
                                             vmem_limit_bytes=VMEM_LIMIT),
    )(*ins)


def _ffn_kernel(x_ref, gpre_ref, win_ref, wout_ref, gpost_ref, y_ref):
    x = x_ref[...]
    d_ff = wout_ref.shape[0]
    h = _rms(x, gpre_ref[...]).astype(BF16)
    gu = jnp.dot(h, win_ref[...], preferred_element_type=F32)
    gt = gu[:, 0:d_ff]
    act = gt * _sigmoid(gt) * gu[:, d_ff:2 * d_ff]
    y_ref[...] = x + _rms(_dot(act, wout_ref[...]), gpost_ref[...])


def _ffn_call(x2d, tm, lw):
    n, d_model = x2d.shape
    row = pl.BlockSpec((tm, d_model), lambda i: (i, 0))
    consts = [lw['g_ffn_pre'], lw['w_ffn_in'], lw['w_ffn_out'], lw['g_ffn_post']]
    return pl.pallas_call(
        _ffn_kernel, grid=(n // tm,), in_specs=[row] + [_full(a.shape) for a in consts],
        out_specs=row, out_shape=jax.ShapeDtypeStruct((n, d_model), F32),
        compiler_params=pltpu.CompilerParams(dimension_semantics=("arbitrary",),
                                             vmem_limit_bytes=VMEM_LIMIT),
    )(x2d, *consts)


def _prompt_inv_count(seq_len):
    pos = np.arange(seq_len)[:, None]
    win = np.repeat(np.array(POOL_WINDOWS), POOL_GROUP)[None, :]
    return jnp.asarray(1.0 / np.minimum(pos + 1, win).astype(np.float32), F32)


def _sample_pool_consts(tm, seq_len):
    ns = tm // seq_len
    rs, rt = np.divmod(np.arange(tm), seq_len)
    hs, hj = np.divmod(np.arange(ns * POOL_SLOTS), POOL_SLOTS)
    cs = np.concatenate([hs, rs])[None, :]
    cpos = np.concatenate([hj - 1, POOL_BUF + rt])[None, :]
    dist = (POOL_BUF + rt)[:, None] - cpos
    same = (rs[:, None] == cs) & (cpos >= 0)
    band = np.stack([same & (dist >= 0) & (dist < w) for w in POOL_WINDOWS]).astype(np.float32)
    win = np.repeat(np.array(POOL_WINDOWS), POOL_GROUP)[None, :]
    pos = (SAMPLE_START_POS + rt)[:, None]
    invc = 1.0 / np.minimum(pos + 1, win).astype(np.float32)
    return jnp.asarray(band, BF16), jnp.asarray(invc, F32)


def _block_diag(blocks):
    n = len(blocks)
    rows = []
    for i, blk in enumerate(blocks):
        rows.append(jnp.concatenate(
            [blk if j == i else jnp.zeros((blk.shape[0], blocks[j].shape[1]), blk.dtype) for j in range(n)],
            axis=1))
    return jnp.concatenate(rows, axis=0)


def _layer_weights(l, w):
    d_model = w['w_in'].shape[1]
    row = lambda a: a.reshape(1, -1).astype(F32)
    w_in = w['w_in'][l]
    lw = {
        'g_mix_pre': row(w['norm_mix_pre'][l]), 'g_mix_post': row(w['norm_mix_post'][l]),
        'g_ffn_pre': row(w['norm_ffn_pre'][l]), 'g_ffn_post': row(w['norm_ffn_post'][l]),
        'mu': row(w['mu_shift'][l]), 'decay_bias': row(w['decay_bias'][l]), 'iclr_bias': row(w['iclr_bias'][l]),
        'lora': _block_diag([w['w_decay_up'][l], w['w_iclr_up'][l]]).astype(BF16),
        'w_gate_up': w['w_gate_up'][l].astype(BF16),
        'k_k': row(w['k_k'][l]), 'k_a': row(w['k_a'][l]), 'r_k': row(w['r_k'][l]),
        'ln_x_w': row(w['ln_x_w'][l]), 'ln_x_b': row(w['ln_x_b'][l]),
        'pool_map': _block_diag([w['pool_map'][l, gi] for gi in range(len(POOL_WINDOWS))]).astype(BF16),
        'pool_scale': row(w['pool_scale'][l]),
        'w_branch_rwkv': w['w_branch_rwkv'][l].astype(BF16), 'w_branch_pool': w['w_branch_pool'][l].astype(BF16),
        'w_out': w['w_out'][l].astype(BF16),
        'w_ffn_in': w['w_ffn_in'][l].astype(BF16), 'w_ffn_out': w['w_ffn_out'][l].astype(BF16),
        'ones': jnp.asarray(np.kron(np.eye(N_HEADS), np.ones((HEAD, HEAD))), BF16),
    }
    if l > 0:
        pad = jnp.zeros((d_model, LANES - LORA_VRES), F32)
        w_in = jnp.concatenate([w_in, w['w_vres_down'][l - 1], pad], axis=1)
        lw['vres_bias'] = row(w['vres_bias'][l - 1])
        lw['w_vres_up'] = jnp.concatenate(
            [w['w_vres_up'][l - 1], jnp.zeros((LANES - LORA_VRES, D_RWKV), F32)], axis=0).astype(BF16)
    lw['w_in'] = w_in.astype(BF16)
    return lw


def _to_pairs(s):
    b = s.shape[0]
    return s.reshape(b, N_PAIRS, 2, HEAD, HEAD).transpose(0, 1, 4, 2, 3).reshape(b, N_PAIRS, HEAD, PAIR)


def _from_pairs(h):
    b = h.shape[0]
    return h.reshape(b, N_PAIRS, HEAD, 2, HEAD).transpose(0, 1, 3, 4, 2).reshape(b, N_HEADS, HEAD, HEAD)


def _tile_rows(n_seq, seq_len, target):
    if seq_len >= target:
        return target
    return min(n_seq * seq_len, target)


def kernel(x_prompt, x_sample, state_wkv, state_shift, state_pool, norm_mix_pre, norm_mix_post, norm_ffn_pre, norm_ffn_post, w_in, mu_shift, decay_bias, w_decay_up, iclr_bias, w_iclr_up, w_gate_up, k_k, k_a, r_k, ln_x_w, ln_x_b, vres_bias, w_vres_down, w_vres_up, pool_map, pool_scale, w_branch_rwkv, w_branch_pool, w_out, w_ffn_in, w_ffn_out):
    weights = dict(norm_mix_pre=norm_mix_pre, norm_mix_post=norm_mix_post, norm_ffn_pre=norm_ffn_pre,
                   norm_ffn_post=norm_ffn_post, w_in=w_in, mu_shift=mu_shift, decay_bias=decay_bias,
                   w_decay_up=w_decay_up, iclr_bias=iclr_bias, w_iclr_up=w_iclr_up, w_gate_up=w_gate_up,
                   k_k=k_k, k_a=k_a, r_k=r_k, ln_x_w=ln_x_w, ln_x_b=ln_x_b, vres_bias=vres_bias,
                   w_vres_down=w_vres_down, w_vres_up=w_vres_up, pool_map=pool_map, pool_scale=pool_scale,
                   w_branch_rwkv=w_branch_rwkv, w_branch_pool=w_branch_pool, w_out=w_out,
                   w_ffn_in=w_ffn_in, w_ffn_out=w_ffn_out)
    depth = w_in.shape[0]
    bp, tp, d_model = x_prompt.shape
    bs, ts, _ = x_sample.shape
    tm_p = _tile_rows(bp, tp, 256)
    tm_s = _tile_rows(bs, ts, 256)
    invc_p = _prompt_inv_count(tp)
    band_s, invc_s = _sample_pool_consts(tm_s, ts)
    ts_pad = -(-ts // CHUNK) * CHUNK

    yp = x_prompt.reshape(bp * tp, d_model)
    ys = x_sample.reshape(bs * ts, d_model)
    vf_p = vf_s = None
    outs = {k: [] for k in ('wkv_p', 'shift_p', 'pool_p', 'wkv_s', 'shift_s', 'pool_s')}
    for l in range(depth):
        lw = _layer_weights(l, weights)

        (r, kp, v, kkn, a, ld, g, bonus, u, gates, last) = _proj_call(yp, tp, tm_p, lw, vf_p, None)
        if l == 0:
            vf_p = v
        o, hout = _wkv_call((r, kp, v, kkn, a, ld), bp, tp, None)
        x1 = _merge_call(o, bonus, g, u, u, POOL_HIST, gates, yp, tp, tm_p, None, invc_p, lw)
        yp = _ffn_call(x1, tm_p, lw)
        outs['wkv_p'].append(_from_pairs(hout))
        tiles = tp // tm_p
        outs['shift_p'].append(last.reshape(bp, tiles, CARRY, D_SHIFT)[:, -1, -1])
        outs['pool_p'].append(u.reshape(bp, tp, D_POOL)[:, -POOL_BUF:])

        first = jnp.zeros((bs, ts, D_SHIFT), F32).at[:, 0].set(state_shift[l]).reshape(bs * ts, D_SHIFT)
        (r, kp, v, kkn, a, ld, g, bonus, u, gates, ps_all) = _proj_call(ys, ts, tm_s, lw, vf_s, first)
        if l == 0:
            vf_s = v
        pad = lambda z: jnp.pad(z.reshape(bs, ts, D_RWKV), ((0, 0), (0, ts_pad - ts), (0, 0))).reshape(
            bs * ts_pad, D_RWKV)
        o_pad, hout = _wkv_call(tuple(pad(z) for z in (r, kp, v, kkn, a, ld)), bs, ts_pad,
                                _to_pairs(state_wkv[l]))
        o = o_pad.reshape(bs, ts_pad, D_RWKV)[:, :ts].reshape(bs * ts, D_RWKV)
        hist = jnp.pad(state_pool[l], ((0, 0), (1, 0), (0, 0))).reshape(bs * POOL_SLOTS, D_POOL)
        x1 = _merge_call(o, bonus, g, u, hist, (tm_s // ts) * POOL_SLOTS, gates, ys, ts, tm_s, band_s, invc_s,
                         lw)
        ys = _ffn_call(x1, tm_s, lw)
        outs['wkv_s'].append(_from_pairs(hout))
        outs['shift_s'].append(ps_all.reshape(bs, ts, D_SHIFT)[:, -1])
        outs['pool_s'].append(jnp.concatenate(
            [state_pool[l], u.reshape(bs, ts, D_POOL)], axis=1)[:, -POOL_BUF:])

    return (yp.reshape(bp, tp, d_model), ys.reshape(bs, ts, d_model),
            jnp.stack(outs['wkv_p']), jnp.stack(outs['shift_p']), jnp.stack(outs['pool_p']),
            jnp.stack(outs['wkv_s']), jnp.stack(outs['shift_s']), jnp.stack(outs['pool_s']))
```

```python
import functools

import numpy as np
import jax
import jax.numpy as jnp
from jax.experimental import pallas as pl
from jax.experimental.pallas import tpu as pltpu

F32 = jnp.float32
BF16 = jnp.bfloat16

HEAD = 64
D_RWKV = 768
N_HEADS = D_RWKV // HEAD
PAIR = 2 * HEAD
N_PAIRS = D_RWKV // PAIR
D_POOL = 256
POOL_WINDOWS = (2, 4, 8, 16)
POOL_GROUP = D_POOL // len(POOL_WINDOWS)
POOL_BUF = max(POOL_WINDOWS) - 1
POOL_SLOTS = POOL_BUF + 1
LORA_DECAY = 64
LORA_ICLR = 64
LORA_GATE = 128
LORA_VRES = 32
D_SHIFT = 3 * D_RWKV + LORA_DECAY + LORA_ICLR + LORA_GATE
SAMPLE_START_POS = 16384
RMS_EPS = 1e-6
GN_EPS = 1e-5 * HEAD
CHUNK = 64
WKV_SUB = 4
POOL_HIST = 32
CARRY = 8
LANES = 128
VMEM_LIMIT = 56 * 1024 * 1024


def _dot(a, b):
    return jnp.dot(a.astype(BF16), b.astype(BF16), preferred_element_type=F32)


def _dot_nt(a, b):
    return jax.lax.dot_general(a.astype(BF16), b.astype(BF16), (((1,), (1,)), ((), ())),
                               preferred_element_type=F32)


def _split(x, parts):
    out = []
    for _ in range(parts - 1):
        hi = x.astype(BF16)
        out.append(hi)
        x = x - hi.astype(F32)
    out.append(x.astype(BF16))
    return out


def _dot_exact_lhs(sel, x, parts):
    acc = None
    for p in _split(x, parts):
        t = jnp.dot(sel, p, preferred_element_type=F32)
        acc = t if acc is None else acc + t
    return acc


def _sigmoid(x):
    return 1.0 / (1.0 + jnp.exp(-x))


def _softplus(x):
    return jnp.maximum(x, 0.0) + jnp.log(1.0 + jnp.exp(-jnp.abs(x)))


def _rms(x, g):
    return x * jax.lax.rsqrt(jnp.mean(x * x, axis=-1, keepdims=True) + RMS_EPS) * g


def _proj_kernel(*refs, tm, seq_len, has_vres, has_first):
    it = iter(refs)
    x_ref, gpre_ref, win_ref, mu_ref, lora_ref, dbias_ref, ibias_ref, wgate_ref = (next(it) for _ in range(8))
    kk_ref, ka_ref, rk_ref, ones_ref = (next(it) for _ in range(4))
    if has_vres:
        vfirst_ref, vbias_ref, wvup_ref = (next(it) for _ in range(3))
    if has_first:
        first_ref = next(it)
    (r_out, kp_out, v_out, kkn_out, a_out, ld_out, g_out, bonus_out, u_out, gates_out,
     last_out, p_scr) = (next(it) for _ in range(12))

    gates_end = D_SHIFT + D_POOL + 2 * x_ref.shape[1]
    i = pl.program_id(0)
    if seq_len >= tm:
        tiles_per_seq = seq_len // tm
        new_seq = (i % tiles_per_seq) == 0
    else:
        new_seq = i == 0

    @pl.when(new_seq)
    def _():
        p_scr[0:CARRY, :] = jnp.zeros((CARRY, D_SHIFT), F32)

    xn = _rms(x_ref[...], gpre_ref[...]).astype(BF16)
    p_scr[CARRY:CARRY + tm, :] = jnp.dot(xn, win_ref[:, 0:D_SHIFT], preferred_element_type=F32)

    if has_first:
        row = jax.lax.broadcasted_iota(jnp.int32, (tm, 1), 0)
        seq_start = jax.lax.rem(row, seq_len) == 0

    def mixed(c0, c1):
        ps = p_scr[CARRY:CARRY + tm, c0:c1]
        prev = p_scr[CARRY - 1:CARRY - 1 + tm, c0:c1]
        if has_first:
            prev = jnp.where(seq_start, first_ref[:, c0:c1], prev)
        return ps + (prev - ps) * mu_ref[:, c0:c1]

    o1, o2, o3 = D_RWKV, 2 * D_RWKV, 3 * D_RWKV
    o5 = o3 + LORA_DECAY + LORA_ICLR
    r = mixed(0, o1)
    k = mixed(o1, o2)
    v = mixed(o2, o3)
    xwa = mixed(o3, o5)
    xg = mixed(o5, D_SHIFT)

    lane = jax.lax.broadcasted_iota(jnp.int32, (1, LANES), 1)
    lora_in = jnp.where(lane < LORA_DECAY, jnp.tanh(xwa), xwa)
    lora = _dot(lora_in, lora_ref[...])
    log_w = -_softplus(-(dbias_ref[...] + lora[:, 0:D_RWKV])) - 0.5
    ld_out[...] = -jnp.exp(log_w)
    a = _sigmoid(ibias_ref[...] + lora[:, D_RWKV:2 * D_RWKV])
    a_out[...] = a
    g_out[...] = _dot(_sigmoid(xg), wgate_ref[...])

    if has_vres:
        vdown = jnp.dot(xn, win_ref[:, gates_end:], preferred_element_type=F32)
        v = v + (vfirst_ref[...] - v) * _sigmoid(vbias_ref[...] + _dot(vdown, wvup_ref[...]))
    v_out[...] = v

    kk = k * kk_ref[...]
    norm = jnp.sqrt(_dot(kk * kk, ones_ref[...]))
    kkn_out[...] = kk / jnp.maximum(norm, 1e-12)
    kp = k * (1.0 + (a - 1.0) * ka_ref[...])
    kp_out[...] = kp
    r_out[...] = r
    bonus_out[...] = _dot(r * kp * rk_ref[...], ones_ref[...]) * v

    u_out[...] = jnp.dot(xn, win_ref[:, D_SHIFT:D_SHIFT + D_POOL], preferred_element_type=F32)
    gates_out[...] = _sigmoid(jnp.dot(xn, win_ref[:, D_SHIFT + D_POOL:gates_end],
                                      preferred_element_type=F32))

    if has_first:
        last_out[...] = p_scr[CARRY:CARRY + tm, :]
    else:
        tail = p_scr[tm:tm + CARRY, :]
        last_out[...] = tail
        p_scr[0:CARRY, :] = tail


def _full(shape):
    nd = len(shape)
    return pl.BlockSpec(shape, lambda *_: (0,) * nd)


def _proj_call(x2d, seq_len, tm, lw, vfirst, first):
    n = x2d.shape[0]
    d_model = x2d.shape[1]
    has_vres = vfirst is not None
    has_first = first is not None
    nt = n // tm
    row = lambda w: pl.BlockSpec((tm, w), lambda i: (i, 0))
    ins = [x2d, lw['g_mix_pre'], lw['w_in'], lw['mu'], lw['lora'], lw['decay_bias'], lw['iclr_bias'],
           lw['w_gate_up'], lw['k_k'], lw['k_a'], lw['r_k'], lw['ones']]
    specs = [row(d_model)] + [_full(a.shape) for a in ins[1:]]
    if has_vres:
        ins += [vfirst, lw['vres_bias'], lw['w_vres_up']]
        specs += [row(D_RWKV), _full(lw['vres_bias'].shape), _full(lw['w_vres_up'].shape)]
    if has_first:
        ins.append(first)
        specs.append(row(D_SHIFT))
    sds = lambda w: jax.ShapeDtypeStruct((n, w), F32)
    last_rows = tm if has_first else CARRY
    out_shape = [sds(D_RWKV)] * 8 + [sds(D_POOL), sds(2 * d_model),
                                     jax.ShapeDtypeStruct((nt * last_rows, D_SHIFT), F32)]
    out_specs = [row(D_RWKV)] * 8 + [row(D_POOL), row(2 * d_model),
                                     pl.BlockSpec((last_rows, D_SHIFT), lambda i: (i, 0))]
    return pl.pallas_call(
        functools.partial(_proj_kernel, tm=tm, seq_len=seq_len, has_vres=has_vres, has_first=has_first),
        grid=(nt,), in_specs=specs, out_specs=out_specs, out_shape=out_shape,
        scratch_shapes=[pltpu.VMEM((tm + CARRY, D_SHIFT), F32)],
        compiler_params=pltpu.CompilerParams(dimension_semantics=("arbitrary",),
                                             vmem_limit_bytes=VMEM_LIMIT),
    )(*ins)


def _wkv_kernel(r_ref, kp_ref, v_ref, kk_ref, a_ref, ld_ref, o_ref, hout_ref, h_scr, *, n_sub):
    c_idx = pl.program_id(1)
    n_steps = pl.num_programs(1)
    C = CHUNK

    lane = jax.lax.broadcasted_iota(jnp.int32, (1, PAIR), 1)
    left = lane < HEAD
    ri = jax.lax.broadcasted_iota(jnp.int32, (PAIR, PAIR), 0)
    ci = jax.lax.broadcasted_iota(jnp.int32, (PAIR, PAIR), 1)
    same = (ri >= HEAD) == (ci >= HEAD)
    strict = same & (ci < ri)
    incl = same & (ci <= ri)
    eye = ri == ci
    eye_f = jnp.where(eye, 1.0, 0.0).astype(F32)

    def stack(x):
        return jnp.concatenate([jnp.where(left, x, 0.0), jnp.where(left, 0.0, x)], axis=0)

    def twice(x):
        return jnp.concatenate([x, x], axis=0)

    @pl.when(c_idx == 0)
    def _():
        h_scr[...] = jnp.zeros(h_scr.shape, F32)

    tr = jax.lax.broadcasted_iota(jnp.int32, (C, C), 0)
    tc = jax.lax.broadcasted_iota(jnp.int32, (C, C), 1)
    tri = jnp.where(tr >= tc, 1.0, 0.0).astype(BF16)

    a_t, r_t, b_t, k_t, b_h, k_h, v_st, w_last = ([] for _ in range(8))
    for j in range(n_sub):
        rows = slice(j * C, (j + 1) * C)
        ld = ld_ref[rows, :]
        cum = _dot_exact_lhs(tri, ld, 3)
        clast = cum[C - 1:C, :]
        e_cum = jnp.exp(cum)
        e_prev = jnp.exp(cum - ld)
        e_neg = jnp.exp(-cum)
        e_rem = jnp.exp(clast - cum)
        wl = jnp.exp(clast)
        kk = kk_ref[rows, :]
        b = kk * a_ref[rows, :]
        kp = kp_ref[rows, :]
        full = (-(kk * e_prev), r_ref[rows, :] * e_cum, b * e_neg, kp * e_neg, b * e_rem, kp * e_rem)
        v = v_ref[rows, :]
        for p in range(N_PAIRS):
            sl = slice(p * PAIR, (p + 1) * PAIR)
            for dst, src in zip((a_t, r_t, b_t, k_t, b_h, k_h), full):
                dst.append(src[:, sl])
            v_st.append(stack(v[:, sl]))
            w_last.append(wl[:, sl])

    units = range(n_sub * N_PAIRS)
    sc = [_dot_nt(jnp.concatenate([a_t[q], r_t[q]], axis=0),
                  jnp.concatenate([stack(b_t[q]), stack(k_t[q])], axis=0)) for q in units]
    n_ab = [jnp.where(strict, twice(s[0:C, 0:PAIR]), 0.0) for s in sc]
    n_ak = [jnp.where(strict, twice(s[0:C, PAIR:2 * PAIR]), 0.0) for s in sc]
    n_rb = [jnp.where(incl, twice(s[C:2 * C, 0:PAIR]), 0.0) for s in sc]
    n_rk = [jnp.where(incl, twice(s[C:2 * C, PAIR:2 * PAIR]), 0.0) for s in sc]

    inv = [eye_f + n for n in n_ab]
    power = n_ab
    y = [_dot(n_ak[q], v_st[q]) for q in units]
    for _ in range(int(np.log2(C)) - 1):
        power = [_dot(m, m) for m in power]
        inv = [inv[q] + _dot(inv[q], power[q]) for q in units]

    au = [_dot(inv[q], jnp.concatenate([stack(a_t[q]), y[q]], axis=1)) for q in units]
    zero = jnp.zeros((PAIR, PAIR), F32)
    big = []
    for q in units:
        rhs = jnp.concatenate([au[q], jnp.concatenate([zero, v_st[q]], axis=1)], axis=0)
        lhs = jnp.concatenate(
            [jnp.concatenate([n_rb[q], n_rk[q]], axis=1),
             jnp.concatenate([stack(b_h[q]).T, stack(k_h[q]).T], axis=1)], axis=0)
        big.append(_dot(lhs, rhs))

    h = [h_scr[p] for p in range(N_PAIRS)]
    for j in range(n_sub):
        rows = slice(j * C, (j + 1) * C)
        qs = [j * N_PAIRS + p for p in range(N_PAIRS)]
        sd = []
        for p, q in enumerate(qs):
            r_hat = stack(r_t[q]) + big[q][0:PAIR, 0:PAIR]
            m_low = big[q][PAIR:2 * PAIR, 0:PAIR]
            sd.append(_dot(jnp.concatenate([r_hat, m_low], axis=0), h[p]))
        for p, q in enumerate(qs):
            o_st = sd[p][0:PAIR] + big[q][0:PAIR, PAIR:2 * PAIR]
            o_ref[rows, p * PAIR:(p + 1) * PAIR] = o_st[0:C] + o_st[C:2 * C]
            w_col = jnp.sum(jnp.where(eye, jnp.broadcast_to(w_last[q], (PAIR, PAIR)), 0.0),
                            axis=1, keepdims=True)
            h[p] = w_col * h[p] + sd[p][PAIR:2 * PAIR] + big[q][PAIR:2 * PAIR, PAIR:2 * PAIR]
    for p in range(N_PAIRS):
        h_scr[p] = h[p]

    @pl.when(c_idx == n_steps - 1)
    def _():
        for p in range(N_PAIRS):
            hout_ref[0, p] = h_scr[p, 0:HEAD, :] + h_scr[p, HEAD:PAIR, :]


def _wkv_call(ops, n_seq, seq_len):
    n_sub = WKV_SUB if seq_len % (WKV_SUB * CHUNK) == 0 else 1
    rows = n_sub * CHUNK
    assert seq_len % rows == 0
    ns = seq_len // rows
    blk = pl.BlockSpec((rows, D_RWKV), lambda b, c: (b * ns + c, 0))
    hspec = pl.BlockSpec((1, N_PAIRS, HEAD, PAIR), lambda b, c: (b, 0, 0, 0))
    return pl.pallas_call(
        functools.partial(_wkv_kernel, n_sub=n_sub),
        grid=(n_seq, ns), in_specs=[blk] * 6, out_specs=[blk, hspec],
        out_shape=[jax.ShapeDtypeStruct((n_seq * seq_len, D_RWKV), F32),
                   jax.ShapeDtypeStruct((n_seq, N_PAIRS, HEAD, PAIR), F32)],
        scratch_shapes=[pltpu.VMEM((N_PAIRS, PAIR, PAIR), F32)],
        compiler_params=pltpu.CompilerParams(dimension_semantics=("arbitrary", "arbitrary"),
                                             vmem_limit_bytes=VMEM_LIMIT),
    )(*ops)


def _wkv_short_kernel(r_ref, kp_ref, v_ref, kk_ref, a_ref, ld_ref, h0_ref, o_ref, hout_ref, *, seq_len):
    C = CHUNK
    L = seq_len
    G = C // L
    lg = int(np.log2(L))

    lane = jax.lax.broadcasted_iota(jnp.int32, (1, PAIR), 1)
    left = lane < HEAD
    ri = jax.lax.broadcasted_iota(jnp.int32, (PAIR, PAIR), 0)
    ci = jax.lax.broadcasted_iota(jnp.int32, (PAIR, PAIR), 1)
    same = jnp.right_shift(ri, lg) == jnp.right_shift(ci, lg)
    strict = same & (ci < ri)
    incl = same & (ci <= ri)
    eye = ri == ci
    eye_f = jnp.where(eye, 1.0, 0.0).astype(F32)
    row_seq = jnp.bitwise_and(jnp.right_shift(jax.lax.broadcasted_iota(jnp.int32, (PAIR, 1), 0), lg), G - 1)

    def stack(x):
        return jnp.concatenate([jnp.where(left, x, 0.0), jnp.where(left, 0.0, x)], axis=0)

    def twice(x):
        return jnp.concatenate([x, x], axis=0)

    tr = jax.lax.broadcasted_iota(jnp.int32, (C, C), 0)
    tc = jax.lax.broadcasted_iota(jnp.int32, (C, C), 1)
    same_t = jnp.right_shift(tr, lg) == jnp.right_shift(tc, lg)
    tri = jnp.where(same_t & (tr >= tc), 1.0, 0.0).astype(BF16)
    blk = jnp.where(same_t, 1.0, 0.0).astype(BF16)

    ld = ld_ref[...]
    cum = _dot_exact_lhs(tri, ld, 3)
    tot = _dot_exact_lhs(blk, ld, 3)
    e_cum = jnp.exp(cum)
    e_prev = jnp.exp(cum - ld)
    e_neg = jnp.exp(-cum)
    e_rem = jnp.exp(tot - cum)
    w_tot = jnp.exp(tot)
    kk = kk_ref[...]
    b = kk * a_ref[...]
    kp = kp_ref[...]
    v = v_ref[...]
    a_full, r_full, bt_full, kt_full = -(kk * e_prev), r_ref[...] * e_cum, b * e_neg, kp * e_neg
    bh_full, kh_full = b * e_rem, kp * e_rem

    pairs = range(N_PAIRS)
    sls = [slice(p * PAIR, (p + 1) * PAIR) for p in pairs]
    v_st = [stack(v[:, sl]) for sl in sls]
    sc = [_dot_nt(jnp.concatenate([a_full[:, sl], r_full[:, sl]], axis=0),
                  jnp.concatenate([stack(bt_full[:, sl]), stack(kt_full[:, sl])], axis=0)) for sl in sls]
    n_ab = [jnp.where(strict, twice(s[0:C, 0:PAIR]), 0.0) for s in sc]
    n_ak = [jnp.where(strict, twice(s[0:C, PAIR:2 * PAIR]), 0.0) for s in sc]
    n_rb = [jnp.where(incl, twice(s[C:2 * C, 0:PAIR]), 0.0) for s in sc]
    n_rk = [jnp.where(incl, twice(s[C:2 * C, PAIR:2 * PAIR]), 0.0) for s in sc]

    inv = [eye_f + n for n in n_ab]
    power = n_ab
    y = [_dot(n_ak[p], v_st[p]) for p in pairs]
    for _ in range(lg - 1):
        power = [_dot(m, m) for m in power]
        inv = [inv[p] + _dot(inv[p], power[p]) for p in pairs]
    au = [_dot(inv[p], jnp.concatenate([stack(a_full[:, sls[p]]), y[p]], axis=1)) for p in pairs]
    zero = jnp.zeros((PAIR, PAIR), F32)
    top = [_dot(jnp.concatenate([n_rb[p], n_rk[p]], axis=1),
                jnp.concatenate([au[p], jnp.concatenate([zero, v_st[p]], axis=1)], axis=0)) for p in pairs]
    r_hat = [stack(r_full[:, sls[p]]) + top[p][:, 0:PAIR] for p in pairs]
    bk_t = [jnp.concatenate([stack(bh_full[:, sls[p]]).T, stack(kh_full[:, sls[p]]).T], axis=1) for p in pairs]

    h0 = [[stack(h0_ref[s, p]) for s in range(G)] for p in pairs]
    sd = [[None] * G for _ in pairs]
    for p in pairs:
        a_hat = au[p][:, 0:PAIR]
        for s in range(G):
            r0, r1 = slice(s * L, (s + 1) * L), slice(C + s * L, C + (s + 1) * L)
            lhs = jnp.concatenate([a_hat[r0], a_hat[r1], r_hat[p][r0], r_hat[p][r1]], axis=0)
            sd[p][s] = _dot(lhs, h0[p][s])
    u_full = []
    for p in pairs:
        u_full.append(jnp.concatenate([sd[p][s][0:L] for s in range(G)]
                                      + [sd[p][s][L:2 * L] for s in range(G)], axis=0) + au[p][:, PAIR:2 * PAIR])
        o_st = jnp.concatenate([sd[p][s][2 * L:3 * L] for s in range(G)]
                               + [sd[p][s][3 * L:4 * L] for s in range(G)], axis=0) + top[p][:, PAIR:2 * PAIR]
        o_ref[:, sls[p]] = o_st[0:C] + o_st[C:2 * C]
    inc = [[None] * G for _ in pairs]
    for p in pairs:
        for s in range(G):
            mine = row_seq == s
            rhs = jnp.concatenate([jnp.where(mine, u_full[p], 0.0), jnp.where(mine, v_st[p], 0.0)], axis=0)
            inc[p][s] = _dot(bk_t[p], rhs)
    for p in pairs:
        for s in range(G):
            w_row = w_tot[s * L:s * L + 1, sls[p]]
            w_col = jnp.sum(jnp.where(eye, jnp.broadcast_to(w_row, (PAIR, PAIR)), 0.0), axis=1, keepdims=True)
            h1 = w_col * h0[p][s] + inc[p][s]
            hout_ref[s, p] = h1[0:HEAD] + h1[HEAD:PAIR]


def _wkv_short_call(ops, n_seq, seq_len, h0):
    assert CHUNK % seq_len == 0 and seq_len % CARRY == 0 and seq_len & (seq_len - 1) == 0
    group = CHUNK // seq_len
    assert n_seq % group == 0
    blk = pl.BlockSpec((CHUNK, D_RWKV), lambda i: (i, 0))
    hspec = pl.BlockSpec((group, N_PAIRS, HEAD, PAIR), lambda i: (i, 0, 0, 0))
    return pl.pallas_call(
        functools.partial(_wkv_short_kernel, seq_len=seq_len),
        grid=(n_seq // group,), in_specs=[blk] * 6 + [hspec], out_specs=[blk, hspec],
        out_shape=[jax.ShapeDtypeStruct((n_seq * seq_len, D_RWKV), F32),
                   jax.ShapeDtypeStruct((n_seq, N_PAIRS, HEAD, PAIR), F32)],
        compiler_params=pltpu.CompilerParams(dimension_semantics=("arbitrary",),
                                             vmem_limit_bytes=VMEM_LIMIT),
    )(*ops, h0)


def _merge_kernel(*refs, tm, tiles_per_seq, banded):
    it = iter(refs)
    o_ref, bonus_ref, g_ref, u_ref, hist_ref, gates_ref, x_ref = (next(it) for _ in range(7))
    if banded:
        band_ref = next(it)
    (invc_ref, lnw_ref, lnb_ref, ones_ref, pmap_ref, pscale_ref, wba_ref, wbb_ref, wout_ref, gpost_ref,
     y_ref) = (next(it) for _ in range(11))
    if not banded:
        s_a, s_b = next(it), next(it)
    d_model = x_ref.shape[1]
    inv_h = 1.0 / HEAD

    o = o_ref[...]
    mean = _dot(o, ones_ref[...]) * inv_h
    cen = o - mean
    var = _dot(cen * cen, ones_ref[...]) * inv_h
    o_n = cen * jax.lax.rsqrt(var + GN_EPS) * lnw_ref[...] + lnb_ref[...]
    o_rwkv = (o_n + bonus_ref[...]) * g_ref[...]

    u = u_ref[...]
    lane = jax.lax.broadcasted_iota(jnp.int32, (1, D_POOL), 1)
    win_sum = jnp.zeros((tm, D_POOL), F32)
    if banded:
        full = jnp.concatenate([hist_ref[...], u], axis=0)
        for gi in range(len(POOL_WINDOWS)):
            s = _dot_exact_lhs(band_ref[gi], full, 2)
            in_group = (lane >= gi * POOL_GROUP) & (lane < (gi + 1) * POOL_GROUP)
            win_sum = jnp.where(in_group, s, win_sum)
    else:
        keep = jnp.where((pl.program_id(0) % tiles_per_seq) == 0, 0.0, 1.0)
        s_a[0:POOL_HIST, :] = hist_ref[...] * keep
        s_a[POOL_HIST:POOL_HIST + tm, :] = u
        src, dst = s_a, s_b
        total = POOL_HIST + tm
        w = 1
        for gi, win in enumerate(POOL_WINDOWS):
            while w < win:
                lo = CARRY * int(np.log2(2 * w))
                dst[lo:total, :] = src[lo:total, :] + src[lo - w:total - w, :]
                src, dst = dst, src
                w *= 2
            in_group = (lane >= gi * POOL_GROUP) & (lane < (gi + 1) * POOL_GROUP)
            win_sum = jnp.where(in_group, src[POOL_HIST:total, :], win_sum)
    diff = win_sum * invc_ref[...] - u
    o_pool = _dot(diff, pmap_ref[...]) * pscale_ref[...]

    gates = gates_ref[...]
    merged = (gates[:, 0:d_model] * _dot(o_rwkv, wba_ref[...])
              + gates[:, d_model:2 * d_model] * _dot(o_pool, wbb_ref[...]))
    y_ref[...] = x_ref[...] + _rms(_dot(merged, wout_ref[...]), gpost_ref[...])


def _merge_call(o, bonus, g, u, hist, hist_rows, gates, x2d, seq_len, tm, band, invc, lw):
    n, d_model = x2d.shape
    nt = n // tm
    tiles_per_seq = max(seq_len // tm, 1)
    banded = band is not None
    row = lambda w: pl.BlockSpec((tm, w), lambda i: (i, 0))
    if banded:
        hist_spec = pl.BlockSpec((hist_rows, D_POOL), lambda i: (i, 0))
        invc_spec = _full(invc.shape)
        scratch = []
    else:
        per = tm // hist_rows
        hist_spec = pl.BlockSpec((hist_rows, D_POOL), lambda i: (jnp.maximum(i * per - 1, 0), 0))
        invc_spec = pl.BlockSpec((tm, D_POOL), lambda i: (i % tiles_per_seq, 0))
        scratch = [pltpu.VMEM((hist_rows + tm, D_POOL), F32)] * 2
    consts = [lw['ln_x_w'], lw['ln_x_b'], lw['ones'], lw['pool_map'], lw['pool_scale'],
              lw['w_branch_rwkv'], lw['w_branch_pool'], lw['w_out'], lw['g_mix_post']]
    ins = [o, bonus, g, u, hist, gates, x2d] + ([band] if banded else []) + [invc] + consts
    specs = ([row(D_RWKV)] * 3 + [row(D_POOL), hist_spec, row(2 * d_model), row(d_model)]
             + ([_full(band.shape)] if banded else []) + [invc_spec] + [_full(a.shape) for a in consts])
    return pl.pallas_call(
        functools.partial(_merge_kernel, tm=tm, tiles_per_seq=tiles_per_seq, banded=banded),
        grid=(nt,), in_specs=specs, out_specs=row(d_model),
        out_shape=jax.ShapeDtypeStruct((n, d_model), F32), scratch_shapes=scratch,
        compiler_params=pltpu.CompilerParams(dimension_semantics=("arbitrary",),
                                             vmem_limit_bytes=VMEM_LIMIT),
    )(*ins)


def _ffn_kernel(x_ref, gpre_ref, win_ref, wout_ref, gpost_ref, y_ref):
    x = x_ref[...]
    d_ff = wout_ref.shape[0]
    h = _rms(x, gpre_ref[...]).astype(BF16)
    gu = jnp.dot(h, win_ref[...], preferred_element_type=F32)
    gt = gu[:, 0:d_ff]
    act = gt * _sigmoid(gt) * gu[:, d_ff:2 * d_ff]
    y_ref[...] = x + _rms(_dot(act, wout_ref[...]), gpost_ref[...])


def _ffn_call(x2d, tm, lw):
    n, d_model = x2d.shape
    row = pl.BlockSpec((tm, d_model), lambda i: (i, 0))
    consts = [lw['g_ffn_pre'], lw['w_ffn_in'], lw['w_ffn_out'], lw['g_ffn_post']]
    return pl.pallas_call(
        _ffn_kernel, grid=(n // tm,), in_specs=[row] + [_full(a.shape) for a in consts],
        out_specs=row, out_shape=jax.ShapeDtypeStruct((n, d_model), F32),
        compiler_params=pltpu.CompilerParams(dimension_semantics=("arbitrary",),
                                             vmem_limit_bytes=VMEM_LIMIT),
    )(x2d, *consts)


def _prompt_inv_count(seq_len):
    pos = np.arange(seq_len)[:, None]
    win = np.repeat(np.array(POOL_WINDOWS), POOL_GROUP)[None, :]
    return jnp.asarray(1.0 / np.minimum(pos + 1, win).astype(np.float32), F32)


def _sample_pool_consts(tm, seq_len):
    ns = tm // seq_len
    rs, rt = np.divmod(np.arange(tm), seq_len)
    hs, hj = np.divmod(np.arange(ns * POOL_SLOTS), POOL_SLOTS)
    cs = np.concatenate([hs, rs])[None, :]
    cpos = np.concatenate([hj - 1, POOL_BUF + rt])[None, :]
    dist = (POOL_BUF + rt)[:, None] - cpos
    same = (rs[:, None] == cs) & (cpos >= 0)
    band = np.stack([same & (dist >= 0) & (dist < w) for w in POOL_WINDOWS]).astype(np.float32)
    win = np.repeat(np.array(POOL_WINDOWS), POOL_GROUP)[None, :]
    pos = (SAMPLE_START_POS + rt)[:, None]
    invc = 1.0 / np.minimum(pos + 1, win).astype(np.float32)
    return jnp.asarray(band, BF16), jnp.asarray(invc, F32)


def _block_diag(blocks):
    n = len(blocks)
    rows = []
    for i, blk in enumerate(blocks):
        rows.append(jnp.concatenate(
            [blk if j == i else jnp.zeros((blk.shape[0], blocks[j].shape[1]), blk.dtype) for j in range(n)],
            axis=1))
    return jnp.concatenate(rows, axis=0)


def _layer_weights(l, w):
    d_model = w['w_in'].shape[1]
    row = lambda a: a.reshape(1, -1).astype(F32)
    w_in = w['w_in'][l]
    lw = {
        'g_mix_pre': row(w['norm_mix_pre'][l]), 'g_mix_post': row(w['norm_mix_post'][l]),
        'g_ffn_pre': row(w['norm_ffn_pre'][l]), 'g_ffn_post': row(w['norm_ffn_post'][l]),
        'mu': row(w['mu_shift'][l]), 'decay_bias': row(w['decay_bias'][l]), 'iclr_bias': row(w['iclr_bias'][l]),
        'lora': _block_diag([w['w_decay_up'][l], w['w_iclr_up'][l]]).astype(BF16),
        'w_gate_up': w['w_gate_up'][l].astype(BF16),
        'k_k': row(w['k_k'][l]), 'k_a': row(w['k_a'][l]), 'r_k': row(w['r_k'][l]),
        'ln_x_w': row(w['ln_x_w'][l]), 'ln_x_b': row(w['ln_x_b'][l]),
        'pool_map': _block_diag([w['pool_map'][l, gi] for gi in range(len(POOL_WINDOWS))]).astype(BF16),
        'pool_scale': row(w['pool_scale'][l]),
        'w_branch_rwkv': w['w_branch_rwkv'][l].astype(BF16), 'w_branch_pool': w['w_branch_pool'][l].astype(BF16),
        'w_out': w['w_out'][l].astype(BF16),
        'w_ffn_in': w['w_ffn_in'][l].astype(BF16), 'w_ffn_out': w['w_ffn_out'][l].astype(BF16),
        'ones': jnp.asarray(np.kron(np.eye(N_HEADS), np.ones((HEAD, HEAD))), BF16),
    }
    if l > 0:
        pad = jnp.zeros((d_model, LANES - LORA_VRES), F32)
        w_in = jnp.concatenate([w_in, w['w_vres_down'][l - 1], pad], axis=1)
        lw['vres_bias'] = row(w['vres_bias'][l - 1])
        lw['w_vres_up'] = jnp.concatenate(
            [w['w_vres_up'][l - 1], jnp.zeros((LANES - LORA_VRES, D_RWKV), F32)], axis=0).astype(BF16)
    lw['w_in'] = w_in.astype(BF16)
    return lw


def _to_pairs(s):
    b = s.shape[0]
    return s.reshape(b, N_PAIRS, 2, HEAD, HEAD).transpose(0, 1, 4, 2, 3).reshape(b, N_PAIRS, HEAD, PAIR)


def _from_pairs(h):
    b = h.shape[0]
    return h.reshape(b, N_PAIRS, HEAD, 2, HEAD).transpose(0, 1, 3, 4, 2).reshape(b, N_HEADS, HEAD, HEAD)


def _tile_rows(n_seq, seq_len, target):
    if seq_len >= target:
        return target
    return min(n_seq * seq_len, target)


def kernel(x_prompt, x_sample, state_wkv, state_shift, state_pool, norm_mix_pre, norm_mix_post, norm_ffn_pre, norm_ffn_post, w_in, mu_shift, decay_bias, w_decay_up, iclr_bias, w_iclr_up, w_gate_up, k_k, k_a, r_k, ln_x_w, ln_x_b, vres_bias, w_vres_down, w_vres_up, pool_map, pool_scale, w_branch_rwkv, w_branch_pool, w_out, w_ffn_in, w_ffn_out):
    weights = dict(norm_mix_pre=norm_mix_pre, norm_mix_post=norm_mix_post, norm_ffn_pre=norm_ffn_pre,
                   norm_ffn_post=norm_ffn_post, w_in=w_in, mu_shift=mu_shift, decay_bias=decay_bias,
                   w_decay_up=w_decay_up, iclr_bias=iclr_bias, w_iclr_up=w_iclr_up, w_gate_up=w_gate_up,
                   k_k=k_k, k_a=k_a, r_k=r_k, ln_x_w=ln_x_w, ln_x_b=ln_x_b, vres_bias=vres_bias,
                   w_vres_down=w_vres_down, w_vres_up=w_vres_up, pool_map=pool_map, pool_scale=pool_scale,
                   w_branch_rwkv=w_branch_rwkv, w_branch_pool=w_branch_pool, w_out=w_out,
                   w_ffn_in=w_ffn_in, w_ffn_out=w_ffn_out)
    depth = w_in.shape[0]
    bp, tp, d_model = x_prompt.shape
    bs, ts, _ = x_sample.shape
    tm_p = _tile_rows(bp, tp, 256)
    tm_s = _tile_rows(bs, ts, 256)
    invc_p = _prompt_inv_count(tp)
    band_s, invc_s = _sample_pool_consts(tm_s, ts)

    yp = x_prompt.reshape(bp * tp, d_model)
    ys = x_sample.reshape(bs * ts, d_model)
    vf_p = vf_s = None
    outs = {k: [] for k in ('wkv_p', 'shift_p', 'pool_p', 'wkv_s', 'shift_s', 'pool_s')}
    for l in range(depth):
        lw = _layer_weights(l, weights)

        (r, kp, v, kkn, a, ld, g, bonus, u, gates, last) = _proj_call(yp, tp, tm_p, lw, vf_p, None)
        if l == 0:
            vf_p = v
        o, hout = _wkv_call((r, kp, v, kkn, a, ld), bp, tp)
        x1 = _merge_call(o, bonus, g, u, u, POOL_HIST, gates, yp, tp, 2 * tm_p, None, invc_p, lw)
        yp = _ffn_call(x1, tm_p, lw)
        outs['wkv_p'].append(_from_pairs(hout))
        tiles = tp // tm_p
        outs['shift_p'].append(last.reshape(bp, tiles, CARRY, D_SHIFT)[:, -1, -1])
        outs['pool_p'].append(u.reshape(bp, tp, D_POOL)[:, -POOL_BUF:])

        first = jnp.zeros((bs, ts, D_SHIFT), F32).at[:, 0].set(state_shift[l]).reshape(bs * ts, D_SHIFT)
        (r, kp, v, kkn, a, ld, g, bonus, u, gates, ps_all) = _proj_call(ys, ts, tm_s, lw, vf_s, first)
        if l == 0:
            vf_s = v
        o, hout = _wkv_short_call((r, kp, v, kkn, a, ld), bs, ts, _to_pairs(state_wkv[l]))
        hist = jnp.pad(state_pool[l], ((0, 0), (1, 0), (0, 0))).reshape(bs * POOL_SLOTS, D_POOL)
        x1 = _merge_call(o, bonus, g, u, hist, (tm_s // ts) * POOL_SLOTS, gates, ys, ts, tm_s, band_s, invc_s,
                         lw)
        ys = _ffn_call(x1, tm_s, lw)
        outs['wkv_s'].append(_from_pairs(hout))
        outs['shift_s'].append(ps_all.reshape(bs, ts, D_SHIFT)[:, -1])
        outs['pool_s'].append(jnp.concatenate(
            [state_pool[l], u.reshape(bs, ts, D_POOL)], axis=1)[:, -POOL_BUF:])

    return (yp.reshape(bp, tp, d_model), ys.reshape(bs, ts, d_model),
            jnp.stack(outs['wkv_p']), jnp.stack(outs['shift_p']), jnp.stack(outs['pool_p']),
            jnp.stack(outs['wkv_s']), jnp.stack(outs['shift_s']), jnp.stack(outs['pool_s']))
```

```python
import functools

import numpy as np
import jax
import jax.numpy as jnp
from jax.experimental import pallas as pl
from jax.experimental.pallas import tpu as pltpu

F32 = jnp.float32
BF16 = jnp.bfloat16

HEAD = 64
D_RWKV = 768
N_HEADS = D_RWKV // HEAD
PAIR = 2 * HEAD
N_PAIRS = D_RWKV // PAIR
D_POOL = 256
POOL_WINDOWS = (2, 4, 8, 16)
POOL_GROUP = D_POOL // len(POOL_WINDOWS)
POOL_BUF = max(POOL_WINDOWS) - 1
POOL_SLOTS = POOL_BUF + 1
LORA_DECAY = 64
LORA_ICLR = 64
LORA_GATE = 128
LORA_VRES = 32
D_SHIFT = 3 * D_RWKV + LORA_DECAY + LORA_ICLR + LORA_GATE
SAMPLE_START_POS = 16384
RMS_EPS = 1e-6
GN_EPS = 1e-5 * HEAD
DECAY_SCALE = float(np.exp(-0.5))
CHUNK = 64
WKV_SUB = 4
POOL_HIST = 32
CARRY = 8
LANES = 128
SUM_TILE = 256
VMEM_LIMIT = 56 * 1024 * 1024


def _dot(a, b):
    return jnp.dot(a.astype(BF16), b.astype(BF16), preferred_element_type=F32)


def _dot_nt(a, b):
    return jax.lax.dot_general(a.astype(BF16), b.astype(BF16), (((1,), (1,)), ((), ())),
                               preferred_element_type=F32)


def _split(x, parts):
    out = []
    for _ in range(parts - 1):
        hi = x.astype(BF16)
        out.append(hi)
        x = x - hi.astype(F32)
    out.append(x.astype(BF16))
    return out


def _dot_exact_lhs(sel, x, parts):
    acc = None
    for p in _split(x, parts):
        t = jnp.dot(sel, p, preferred_element_type=F32)
        acc = t if acc is None else acc + t
    return acc


def _head_sums(x, ones):
    return jnp.concatenate(
        [_dot(x[:, c:c + SUM_TILE], ones) for c in range(0, x.shape[1], SUM_TILE)], axis=1)


def _sigmoid(x):
    return 1.0 / (1.0 + jnp.exp(-x))


def _rms(x, g):
    return x * jax.lax.rsqrt(jnp.mean(x * x, axis=-1, keepdims=True) + RMS_EPS) * g


def _proj_kernel(*refs, tm, seq_len, has_vres, has_first):
    it = iter(refs)
    x_ref, gpre_ref, win_ref, mu_ref, lora_ref, dbias_ref, ibias_ref, wgate_ref = (next(it) for _ in range(8))
    kk_ref, ka_ref, rk_ref, ones_ref = (next(it) for _ in range(4))
    if has_vres:
        vfirst_ref, vbias_ref, wvup_ref = (next(it) for _ in range(3))
    if has_first:
        first_ref = next(it)
    (r_out, kp_out, v_out, kkn_out, a_out, ld_out, g_out, bonus_out, u_out, gates_out,
     last_out, p_scr) = (next(it) for _ in range(12))

    gates_end = D_SHIFT + D_POOL + 2 * x_ref.shape[1]
    i = pl.program_id(0)
    if seq_len >= tm:
        tiles_per_seq = seq_len // tm
        new_seq = (i % tiles_per_seq) == 0
    else:
        new_seq = i == 0

    @pl.when(new_seq)
    def _():
        p_scr[0:CARRY, :] = jnp.zeros((CARRY, D_SHIFT), F32)

    xn = _rms(x_ref[...], gpre_ref[...]).astype(BF16)
    p_scr[CARRY:CARRY + tm, :] = jnp.dot(xn, win_ref[:, 0:D_SHIFT], preferred_element_type=F32)

    if has_first:
        row = jax.lax.broadcasted_iota(jnp.int32, (tm, 1), 0)
        seq_start = jax.lax.rem(row, seq_len) == 0

    def mixed(c0, c1):
        ps = p_scr[CARRY:CARRY + tm, c0:c1]
        prev = p_scr[CARRY - 1:CARRY - 1 + tm, c0:c1]
        if has_first:
            prev = jnp.where(seq_start, first_ref[:, c0:c1], prev)
        return ps + (prev - ps) * mu_ref[:, c0:c1]

    o1, o2, o3 = D_RWKV, 2 * D_RWKV, 3 * D_RWKV
    o5 = o3 + LORA_DECAY + LORA_ICLR
    r = mixed(0, o1)
    k = mixed(o1, o2)
    v = mixed(o2, o3)
    xwa = mixed(o3, o5)
    xg = mixed(o5, D_SHIFT)

    lane = jax.lax.broadcasted_iota(jnp.int32, (1, LANES), 1)
    lora_in = jnp.where(lane < LORA_DECAY, jnp.tanh(xwa), xwa)
    lora = _dot(lora_in, lora_ref[...])
    ld_out[...] = -DECAY_SCALE * _sigmoid(dbias_ref[...] + lora[:, 0:D_RWKV])
    a = _sigmoid(ibias_ref[...] + lora[:, D_RWKV:2 * D_RWKV])
    a_out[...] = a
    g_out[...] = _dot(_sigmoid(xg), wgate_ref[...])

    if has_vres:
        vdown = jnp.dot(xn, win_ref[:, gates_end:], preferred_element_type=F32)
        v = v + (vfirst_ref[...] - v) * _sigmoid(vbias_ref[...] + _dot(vdown, wvup_ref[...]))
    v_out[...] = v

    kk = k * kk_ref[...]
    norm = jnp.sqrt(_head_sums(kk * kk, ones_ref[...]))
    kkn_out[...] = kk / jnp.maximum(norm, 1e-12)
    kp = k * (1.0 + (a - 1.0) * ka_ref[...])
    kp_out[...] = kp
    r_out[...] = r
    bonus_out[...] = _head_sums(r * kp * rk_ref[...], ones_ref[...]) * v

    u_out[...] = jnp.dot(xn, win_ref[:, D_SHIFT:D_SHIFT + D_POOL], preferred_element_type=F32)
    gates_out[...] = _sigmoid(jnp.dot(xn, win_ref[:, D_SHIFT + D_POOL:gates_end],
                                      preferred_element_type=F32))

    if has_first:
        last_out[...] = p_scr[CARRY:CARRY + tm, :]
    else:
        tail = p_scr[tm:tm + CARRY, :]
        last_out[...] = tail
        p_scr[0:CARRY, :] = tail


def _full(shape):
    nd = len(shape)
    return pl.BlockSpec(shape, lambda *_: (0,) * nd, pipeline_mode=pl.Buffered(1))


def _proj_call(x2d, seq_len, tm, lw, vfirst, first):
    n = x2d.shape[0]
    d_model = x2d.shape[1]
    has_vres = vfirst is not None
    has_first = first is not None
    nt = n // tm
    row = lambda w: pl.BlockSpec((tm, w), lambda i: (i, 0))
    ins = [x2d, lw['g_mix_pre'], lw['w_in'], lw['mu'], lw['lora'], lw['decay_bias'], lw['iclr_bias'],
           lw['w_gate_up'], lw['k_k'], lw['k_a'], lw['r_k'], lw['ones']]
    specs = [row(d_model)] + [_full(a.shape) for a in ins[1:]]
    if has_vres:
        ins += [vfirst, lw['vres_bias'], lw['w_vres_up']]
        specs += [row(D_RWKV), _full(lw['vres_bias'].shape), _full(lw['w_vres_up'].shape)]
    if has_first:
        ins.append(first)
        specs.append(row(D_SHIFT))
    sds = lambda w: jax.ShapeDtypeStruct((n, w), F32)
    last_rows = tm if has_first else CARRY
    out_shape = [sds(D_RWKV)] * 8 + [sds(D_POOL), sds(2 * d_model),
                                     jax.ShapeDtypeStruct((nt * last_rows, D_SHIFT), F32)]
    out_specs = [row(D_RWKV)] * 8 + [row(D_POOL), row(2 * d_model),
                                     pl.BlockSpec((last_rows, D_SHIFT), lambda i: (i, 0))]
    return pl.pallas_call(
        functools.partial(_proj_kernel, tm=tm, seq_len=seq_len, has_vres=has_vres, has_first=has_first),
        grid=(nt,), in_specs=specs, out_specs=out_specs, out_shape=out_shape,
        scratch_shapes=[pltpu.VMEM((tm + CARRY, D_SHIFT), F32)],
        compiler_params=pltpu.CompilerParams(dimension_semantics=("arbitrary",),
                                             vmem_limit_bytes=VMEM_LIMIT),
    )(*ins)


def _wkv_kernel(r_ref, kp_ref, v_ref, kk_ref, a_ref, ld_ref, o_ref, hout_ref, h_scr, *, n_sub):
    c_idx = pl.program_id(1)
    n_steps = pl.num_programs(1)
    C = CHUNK

    lane = jax.lax.broadcasted_iota(jnp.int32, (1, PAIR), 1)
    left = lane < HEAD
    ri = jax.lax.broadcasted_iota(jnp.int32, (PAIR, PAIR), 0)
    ci = jax.lax.broadcasted_iota(jnp.int32, (PAIR, PAIR), 1)
    same = (ri >= HEAD) == (ci >= HEAD)
    strict = same & (ci < ri)
    incl = same & (ci <= ri)
    eye = ri == ci
    eye_f = jnp.where(eye, 1.0, 0.0).astype(F32)

    def stack(x):
        return jnp.concatenate([jnp.where(left, x, 0.0), jnp.where(left, 0.0, x)], axis=0)

    def twice(x):
        return jnp.concatenate([x, x], axis=0)

    @pl.when(c_idx == 0)
    def _():
        h_scr[...] = jnp.zeros(h_scr.shape, F32)

    tr = jax.lax.broadcasted_iota(jnp.int32, (C, C), 0)
    tc = jax.lax.broadcasted_iota(jnp.int32, (C, C), 1)
    tri = jnp.where(tr >= tc, 1.0, 0.0).astype(BF16)

    a_t, r_t, b_t, k_t, b_h, k_h, v_st, w_last = ([] for _ in range(8))
    for j in range(n_sub):
        rows = slice(j * C, (j + 1) * C)
        ld = ld_ref[rows, :]
        cum = _dot_exact_lhs(tri, ld, 3)
        clast = cum[C - 1:C, :]
        e_cum = jnp.exp(cum)
        e_prev = jnp.exp(cum - ld)
        e_neg = jnp.exp(-cum)
        e_rem = jnp.exp(clast - cum)
        wl = jnp.exp(clast)
        kk = kk_ref[rows, :]
        b = kk * a_ref[rows, :]
        kp = kp_ref[rows, :]
        full = (-(kk * e_prev), r_ref[rows, :] * e_cum, b * e_neg, kp * e_neg, b * e_rem, kp * e_rem)
        v = v_ref[rows, :]
        for p in range(N_PAIRS):
            sl = slice(p * PAIR, (p + 1) * PAIR)
            for dst, src in zip((a_t, r_t, b_t, k_t, b_h, k_h), full):
                dst.append(src[:, sl])
            v_st.append(stack(v[:, sl]))
            w_last.append(wl[:, sl])

    units = range(n_sub * N_PAIRS)
    sc = [_dot_nt(jnp.concatenate([a_t[q], r_t[q]], axis=0),
                  jnp.concatenate([stack(b_t[q]), stack(k_t[q])], axis=0)) for q in units]
    n_ab = [jnp.where(strict, twice(s[0:C, 0:PAIR]), 0.0) for s in sc]
    n_ak = [jnp.where(strict, twice(s[0:C, PAIR:2 * PAIR]), 0.0) for s in sc]
    n_rb = [jnp.where(incl, twice(s[C:2 * C, 0:PAIR]), 0.0) for s in sc]
    n_rk = [jnp.where(incl, twice(s[C:2 * C, PAIR:2 * PAIR]), 0.0) for s in sc]

    inv = [eye_f + n for n in n_ab]
    power = n_ab
    y = [_dot(n_ak[q], v_st[q]) for q in units]
    for _ in range(int(np.log2(C)) - 1):
        power = [_dot(m, m) for m in power]
        inv = [inv[q] + _dot(inv[q], power[q]) for q in units]

    au = [_dot(inv[q], jnp.concatenate([stack(a_t[q]), y[q]], axis=1)) for q in units]
    zero = jnp.zeros((PAIR, PAIR), F32)
    big = []
    for q in units:
        rhs = jnp.concatenate([au[q], jnp.concatenate([zero, v_st[q]], axis=1)], axis=0)
        lhs = jnp.concatenate(
            [jnp.concatenate([n_rb[q], n_rk[q]], axis=1),
             jnp.concatenate([stack(b_h[q]).T, stack(k_h[q]).T], axis=1)], axis=0)
        big.append(_dot(lhs, rhs))

    h = [h_scr[p] for p in range(N_PAIRS)]
    for j in range(n_sub):
        rows = slice(j * C, (j + 1) * C)
        qs = [j * N_PAIRS + p for p in range(N_PAIRS)]
        sd = []
        for p, q in enumerate(qs):
            r_hat = stack(r_t[q]) + big[q][0:PAIR, 0:PAIR]
            m_low = big[q][PAIR:2 * PAIR, 0:PAIR]
            sd.append(_dot(jnp.concatenate([r_hat, m_low], axis=0), h[p]))
        for p, q in enumerate(qs):
            o_st = sd[p][0:PAIR] + big[q][0:PAIR, PAIR:2 * PAIR]
            o_ref[rows, p * PAIR:(p + 1) * PAIR] = o_st[0:C] + o_st[C:2 * C]
            w_col = jnp.sum(jnp.where(eye, jnp.broadcast_to(w_last[q], (PAIR, PAIR)), 0.0),
                            axis=1, keepdims=True)
            h[p] = w_col * h[p] + sd[p][PAIR:2 * PAIR] + big[q][PAIR:2 * PAIR, PAIR:2 * PAIR]
    for p in range(N_PAIRS):
        h_scr[p] = h[p]

    @pl.when(c_idx == n_steps - 1)
    def _():
        for p in range(N_PAIRS):
            hout_ref[0, p] = h_scr[p, 0:HEAD, :] + h_scr[p, HEAD:PAIR, :]


def _wkv_call(ops, n_seq, seq_len):
    n_sub = WKV_SUB if seq_len % (WKV_SUB * CHUNK) == 0 else 1
    rows = n_sub * CHUNK
    assert seq_len % rows == 0
    ns = seq_len // rows
    blk = pl.BlockSpec((rows, D_RWKV), lambda b, c: (b * ns + c, 0))
    hspec = pl.BlockSpec((1, N_PAIRS, HEAD, PAIR), lambda b, c: (b, 0, 0, 0))
    return pl.pallas_call(
        functools.partial(_wkv_kernel, n_sub=n_sub),
        grid=(n_seq, ns), in_specs=[blk] * 6, out_specs=[blk, hspec],
        out_shape=[jax.ShapeDtypeStruct((n_seq * seq_len, D_RWKV), F32),
                   jax.ShapeDtypeStruct((n_seq, N_PAIRS, HEAD, PAIR), F32)],
        scratch_shapes=[pltpu.VMEM((N_PAIRS, PAIR, PAIR), F32)],
        compiler_params=pltpu.CompilerParams(dimension_semantics=("arbitrary", "arbitrary"),
                                             vmem_limit_bytes=VMEM_LIMIT),
    )(*ops)


def _wkv_short_kernel(r_ref, kp_ref, v_ref, kk_ref, a_ref, ld_ref, h0_ref, o_ref, hout_ref, *, seq_len):
    C = CHUNK
    L = seq_len
    G = C // L
    lg = int(np.log2(L))

    lane = jax.lax.broadcasted_iota(jnp.int32, (1, PAIR), 1)
    left = lane < HEAD
    ri = jax.lax.broadcasted_iota(jnp.int32, (PAIR, PAIR), 0)
    ci = jax.lax.broadcasted_iota(jnp.int32, (PAIR, PAIR), 1)
    same = jnp.right_shift(ri, lg) == jnp.right_shift(ci, lg)
    strict = same & (ci < ri)
    incl = same & (ci <= ri)
    eye = ri == ci
    eye_f = jnp.where(eye, 1.0, 0.0).astype(F32)
    row_seq = jnp.bitwise_and(jnp.right_shift(jax.lax.broadcasted_iota(jnp.int32, (PAIR, 1), 0), lg), G - 1)

    def stack(x):
        return jnp.concatenate([jnp.where(left, x, 0.0), jnp.where(left, 0.0, x)], axis=0)

    def twice(x):
        return jnp.concatenate([x, x], axis=0)

    tr = jax.lax.broadcasted_iota(jnp.int32, (C, C), 0)
    tc = jax.lax.broadcasted_iota(jnp.int32, (C, C), 1)
    same_t = jnp.right_shift(tr, lg) == jnp.right_shift(tc, lg)
    tri = jnp.where(same_t & (tr >= tc), 1.0, 0.0).astype(BF16)
    blk = jnp.where(same_t, 1.0, 0.0).astype(BF16)

    ld = ld_ref[...]
    cum = _dot_exact_lhs(tri, ld, 3)
    tot = _dot_exact_lhs(blk, ld, 3)
    e_cum = jnp.exp(cum)
    e_prev = jnp.exp(cum - ld)
    e_neg = jnp.exp(-cum)
    e_rem = jnp.exp(tot - cum)
    w_tot = jnp.exp(tot)
    kk = kk_ref[...]
    b = kk * a_ref[...]
    kp = kp_ref[...]
    v = v_ref[...]
    a_full, r_full, bt_full, kt_full = -(kk * e_prev), r_ref[...] * e_cum, b * e_neg, kp * e_neg
    bh_full, kh_full = b * e_rem, kp * e_rem

    pairs = range(N_PAIRS)
    sls = [slice(p * PAIR, (p + 1) * PAIR) for p in pairs]
    v_st = [stack(v[:, sl]) for sl in sls]
    sc = [_dot_nt(jnp.concatenate([a_full[:, sl], r_full[:, sl]], axis=0),
                  jnp.concatenate([stack(bt_full[:, sl]), stack(kt_full[:, sl])], axis=0)) for sl in sls]
    n_ab = [jnp.where(strict, twice(s[0:C, 0:PAIR]), 0.0) for s in sc]
    n_ak = [jnp.where(strict, twice(s[0:C, PAIR:2 * PAIR]), 0.0) for s in sc]
    n_rb = [jnp.where(incl, twice(s[C:2 * C, 0:PAIR]), 0.0) for s in sc]
    n_rk = [jnp.where(incl, twice(s[C:2 * C, PAIR:2 * PAIR]), 0.0) for s in sc]

    inv = [eye_f + n for n in n_ab]
    power = n_ab
    y = [_dot(n_ak[p], v_st[p]) for p in pairs]
    for _ in range(lg - 1):
        power = [_dot(m, m) for m in power]
        inv = [inv[p] + _dot(inv[p], power[p]) for p in pairs]
    au = [_dot(inv[p], jnp.concatenate([stack(a_full[:, sls[p]]), y[p]], axis=1)) for p in pairs]
    zero = jnp.zeros((PAIR, PAIR), F32)
    top = [_dot(jnp.concatenate([n_rb[p], n_rk[p]], axis=1),
                jnp.concatenate([au[p], jnp.concatenate([zero, v_st[p]], axis=1)], axis=0)) for p in pairs]
    r_hat = [stack(r_full[:, sls[p]]) + top[p][:, 0:PAIR] for p in pairs]
    bk_t = [jnp.concatenate([stack(bh_full[:, sls[p]]).T, stack(kh_full[:, sls[p]]).T], axis=1) for p in pairs]

    h0 = [[stack(h0_ref[s, p]) for s in range(G)] for p in pairs]
    sd = [[None] * G for _ in pairs]
    for p in pairs:
        a_hat = au[p][:, 0:PAIR]
        for s in range(G):
            r0, r1 = slice(s * L, (s + 1) * L), slice(C + s * L, C + (s + 1) * L)
            lhs = jnp.concatenate([a_hat[r0], a_hat[r1], r_hat[p][r0], r_hat[p][r1]], axis=0)
            sd[p][s] = _dot(lhs, h0[p][s])
    u_full = []
    for p in pairs:
        u_full.append(jnp.concatenate([sd[p][s][0:L] for s in range(G)]
                                      + [sd[p][s][L:2 * L] for s in range(G)], axis=0) + au[p][:, PAIR:2 * PAIR])
        o_st = jnp.concatenate([sd[p][s][2 * L:3 * L] for s in range(G)]
                               + [sd[p][s][3 * L:4 * L] for s in range(G)], axis=0) + top[p][:, PAIR:2 * PAIR]
        o_ref[:, sls[p]] = o_st[0:C] + o_st[C:2 * C]
    inc = [[None] * G for _ in pairs]
    for p in pairs:
        for s in range(G):
            mine = row_seq == s
            rhs = jnp.concatenate([jnp.where(mine, u_full[p], 0.0), jnp.where(mine, v_st[p], 0.0)], axis=0)
            inc[p][s] = _dot(bk_t[p], rhs)
    for p in pairs:
        for s in range(G):
            w_row = w_tot[s * L:s * L + 1, sls[p]]
            w_col = jnp.sum(jnp.where(eye, jnp.broadcast_to(w_row, (PAIR, PAIR)), 0.0), axis=1, keepdims=True)
            h1 = w_col * h0[p][s] + inc[p][s]
            hout_ref[s, p] = h1[0:HEAD] + h1[HEAD:PAIR]


def _wkv_short_call(ops, n_seq, seq_len, h0):
    assert CHUNK % seq_len == 0 and seq_len % CARRY == 0 and seq_len & (seq_len - 1) == 0
    group = CHUNK // seq_len
    assert n_seq % group == 0
    blk = pl.BlockSpec((CHUNK, D_RWKV), lambda i: (i, 0))
    hspec = pl.BlockSpec((group, N_PAIRS, HEAD, PAIR), lambda i: (i, 0, 0, 0))
    return pl.pallas_call(
        functools.partial(_wkv_short_kernel, seq_len=seq_len),
        grid=(n_seq // group,), in_specs=[blk] * 6 + [hspec], out_specs=[blk, hspec],
        out_shape=[jax.ShapeDtypeStruct((n_seq * seq_len, D_RWKV), F32),
                   jax.ShapeDtypeStruct((n_seq, N_PAIRS, HEAD, PAIR), F32)],
        compiler_params=pltpu.CompilerParams(dimension_semantics=("arbitrary",),
                                             vmem_limit_bytes=VMEM_LIMIT),
    )(*ops, h0)


def _merge_kernel(*refs, tm, tiles_per_seq, banded):
    it = iter(refs)
    o_ref, bonus_ref, g_ref, u_ref, hist_ref, gates_ref, x_ref = (next(it) for _ in range(7))
    if banded:
        band_ref = next(it)
    (invc_ref, lnw_ref, lnb_ref, ones_ref, pmap_ref, pscale_ref, wba_ref, wbb_ref, wout_ref, gpost_ref,
     y_ref) = (next(it) for _ in range(11))
    if not banded:
        s_a, s_b = next(it), next(it)
    d_model = x_ref.shape[1]
    inv_h = 1.0 / HEAD

    o = o_ref[...]
    mean = _head_sums(o, ones_ref[...]) * inv_h
    cen = o - mean
    var = _head_sums(cen * cen, ones_ref[...]) * inv_h
    o_n = cen * jax.lax.rsqrt(var + GN_EPS) * lnw_ref[...] + lnb_ref[...]
    o_rwkv = (o_n + bonus_ref[...]) * g_ref[...]

    u = u_ref[...]
    lane = jax.lax.broadcasted_iota(jnp.int32, (1, D_POOL), 1)
    win_sum = jnp.zeros((tm, D_POOL), F32)
    if banded:
        full = jnp.concatenate([hist_ref[...], u], axis=0)
        for gi in range(len(POOL_WINDOWS)):
            s = _dot_exact_lhs(band_ref[gi], full, 2)
            in_group = (lane >= gi * POOL_GROUP) & (lane < (gi + 1) * POOL_GROUP)
            win_sum = jnp.where(in_group, s, win_sum)
    else:
        keep = jnp.where((pl.program_id(0) % tiles_per_seq) == 0, 0.0, 1.0)
        s_a[0:POOL_HIST, :] = hist_ref[...] * keep
        s_a[POOL_HIST:POOL_HIST + tm, :] = u
        src, dst = s_a, s_b
        total = POOL_HIST + tm
        w = 1
        for gi, win in enumerate(POOL_WINDOWS):
            while w < win:
                lo = CARRY * int(np.log2(2 * w))
                dst[lo:total, :] = src[lo:total, :] + src[lo - w:total - w, :]
                src, dst = dst, src
                w *= 2
            in_group = (lane >= gi * POOL_GROUP) & (lane < (gi + 1) * POOL_GROUP)
            win_sum = jnp.where(in_group, src[POOL_HIST:total, :], win_sum)
    diff = win_sum * invc_ref[...] - u
    o_pool = _dot(diff, pmap_ref[...]) * pscale_ref[...]

    gates = gates_ref[...]
    merged = (gates[:, 0:d_model] * _dot(o_rwkv, wba_ref[...])
              + gates[:, d_model:2 * d_model] * _dot(o_pool, wbb_ref[...]))
    y_ref[...] = x_ref[...] + _rms(_dot(merged, wout_ref[...]), gpost_ref[...])


def _merge_call(o, bonus, g, u, hist, hist_rows, gates, x2d, seq_len, tm, band, invc, lw):
    n, d_model = x2d.shape
    nt = n // tm
    tiles_per_seq = max(seq_len // tm, 1)
    banded = band is not None
    row = lambda w: pl.BlockSpec((tm, w), lambda i: (i, 0))
    if banded:
        hist_spec = pl.BlockSpec((hist_rows, D_POOL), lambda i: (i, 0))
        invc_spec = _full(invc.shape)
        scratch = []
    else:
        per = tm // hist_rows
        hist_spec = pl.BlockSpec((hist_rows, D_POOL), lambda i: (jnp.maximum(i * per - 1, 0), 0))
        invc_spec = pl.BlockSpec((tm, D_POOL), lambda i: (i % tiles_per_seq, 0))
        scratch = [pltpu.VMEM((hist_rows + tm, D_POOL), F32)] * 2
    consts = [lw['ln_x_w'], lw['ln_x_b'], lw['ones'], lw['pool_map'], lw['pool_scale'],
              lw['w_branch_rwkv'], lw['w_branch_pool'], lw['w_out'], lw['g_mix_post']]
    ins = [o, bonus, g, u, hist, gates, x2d] + ([band] if banded else []) + [invc] + consts
    specs = ([row(D_RWKV)] * 3 + [row(D_POOL), hist_spec, row(2 * d_model), row(d_model)]
             + ([_full(band.shape)] if banded else []) + [invc_spec] + [_full(a.shape) for a in consts])
    return pl.pallas_call(
        functools.partial(_merge_kernel, tm=tm, tiles_per_seq=tiles_per_seq, banded=banded),
        grid=(nt,), in_specs=specs, out_specs=row(d_model),
        out_shape=jax.ShapeDtypeStruct((n, d_model), F32), scratch_shapes=scratch,
        compiler_params=pltpu.CompilerParams(dimension_semantics=("arbitrary",),
                                             vmem_limit_bytes=VMEM_LIMIT),
    )(*ins)


def _ffn_kernel(x_ref, gpre_ref, win_ref, wout_ref, gpost_ref, y_ref):
    x = x_ref[...]
    d_ff = wout_ref.shape[0]
    h = _rms(x, gpre_ref[...]).astype(BF16)
    gu = jnp.dot(h, win_ref[...], preferred_element_type=F32)
    gt = gu[:, 0:d_ff]
    act = gt * _sigmoid(gt) * gu[:, d_ff:2 * d_ff]
    y_ref[...] = x + _rms(_dot(act, wout_ref[...]), gpost_ref[...])


def _ffn_call(x2d, tm, lw):
    n, d_model = x2d.shape
    row = pl.BlockSpec((tm, d_model), lambda i: (i, 0))
    consts = [lw['g_ffn_pre'], lw['w_ffn_in'], lw['w_ffn_out'], lw['g_ffn_post']]
    return pl.pallas_call(
        _ffn_kernel, grid=(n // tm,), in_specs=[row] + [_full(a.shape) for a in consts],
        out_specs=row, out_shape=jax.ShapeDtypeStruct((n, d_model), F32),
        compiler_params=pltpu.CompilerParams(dimension_semantics=("arbitrary",),
                                             vmem_limit_bytes=VMEM_LIMIT),
    )(x2d, *consts)


def _prompt_inv_count(seq_len):
    pos = np.arange(seq_len)[:, None]
    win = np.repeat(np.array(POOL_WINDOWS), POOL_GROUP)[None, :]
    return jnp.asarray(1.0 / np.minimum(pos + 1, win).astype(np.float32), F32)


def _sample_pool_consts(tm, seq_len):
    ns = tm // seq_len
    rs, rt = np.divmod(np.arange(tm), seq_len)
    hs, hj = np.divmod(np.arange(ns * POOL_SLOTS), POOL_SLOTS)
    cs = np.concatenate([hs, rs])[None, :]
    cpos = np.concatenate([hj - 1, POOL_BUF + rt])[None, :]
    dist = (POOL_BUF + rt)[:, None] - cpos
    same = (rs[:, None] == cs) & (cpos >= 0)
    band = np.stack([same & (dist >= 0) & (dist < w) for w in POOL_WINDOWS]).astype(np.float32)
    win = np.repeat(np.array(POOL_WINDOWS), POOL_GROUP)[None, :]
    pos = (SAMPLE_START_POS + rt)[:, None]
    invc = 1.0 / np.minimum(pos + 1, win).astype(np.float32)
    return jnp.asarray(band, BF16), jnp.asarray(invc, F32)


def _block_diag(blocks):
    n = len(blocks)
    rows = []
    for i, blk in enumerate(blocks):
        rows.append(jnp.concatenate(
            [blk if j == i else jnp.zeros((blk.shape[0], blocks[j].shape[1]), blk.dtype) for j in range(n)],
            axis=1))
    return jnp.concatenate(rows, axis=0)


def _layer_weights(l, w):
    d_model = w['w_in'].shape[1]
    row = lambda a: a.reshape(1, -1).astype(F32)
    w_in = w['w_in'][l]
    lw = {
        'g_mix_pre': row(w['norm_mix_pre'][l]), 'g_mix_post': row(w['norm_mix_post'][l]),
        'g_ffn_pre': row(w['norm_ffn_pre'][l]), 'g_ffn_post': row(w['norm_ffn_post'][l]),
        'mu': row(w['mu_shift'][l]), 'decay_bias': row(w['decay_bias'][l]), 'iclr_bias': row(w['iclr_bias'][l]),
        'lora': _block_diag([w['w_decay_up'][l], w['w_iclr_up'][l]]).astype(BF16),
        'w_gate_up': w['w_gate_up'][l].astype(BF16),
        'k_k': row(w['k_k'][l]), 'k_a': row(w['k_a'][l]), 'r_k': row(w['r_k'][l]),
        'ln_x_w': row(w['ln_x_w'][l]), 'ln_x_b': row(w['ln_x_b'][l]),
        'pool_map': _block_diag([w['pool_map'][l, gi] for gi in range(len(POOL_WINDOWS))]).astype(BF16),
        'pool_scale': row(w['pool_scale'][l]),
        'w_branch_rwkv': w['w_branch_rwkv'][l].astype(BF16), 'w_branch_pool': w['w_branch_pool'][l].astype(BF16),
        'w_out': w['w_out'][l].astype(BF16),
        'w_ffn_in': w['w_ffn_in'][l].astype(BF16), 'w_ffn_out': w['w_ffn_out'][l].astype(BF16),
        'ones': jnp.asarray(np.kron(np.eye(SUM_TILE // HEAD), np.ones((HEAD, HEAD))), BF16),
    }
    if l > 0:
        pad = jnp.zeros((d_model, LANES - LORA_VRES), F32)
        w_in = jnp.concatenate([w_in, w['w_vres_down'][l - 1], pad], axis=1)
        lw['vres_bias'] = row(w['vres_bias'][l - 1])
        lw['w_vres_up'] = jnp.concatenate(
            [w['w_vres_up'][l - 1], jnp.zeros((LANES - LORA_VRES, D_RWKV), F32)], axis=0).astype(BF16)
    lw['w_in'] = w_in.astype(BF16)
    return lw


def _to_pairs(s):
    lead = s.shape[:-3]
    n = len(lead)
    s = s.reshape(*lead, N_PAIRS, 2, HEAD, HEAD)
    return s.transpose(*range(n), n, n + 3, n + 1, n + 2).reshape(*lead, N_PAIRS, HEAD, PAIR)


def _from_pairs(h):
    lead = h.shape[:-3]
    n = len(lead)
    h = h.reshape(*lead, N_PAIRS, HEAD, 2, HEAD)
    return h.transpose(*range(n), n, n + 2, n + 3, n + 1).reshape(*lead, N_HEADS, HEAD, HEAD)


def _tile_rows(n_seq, seq_len, target):
    if seq_len >= target:
        return target
    return min(n_seq * seq_len, target)


def kernel(x_prompt, x_sample, state_wkv, state_shift, state_pool, norm_mix_pre, norm_mix_post, norm_ffn_pre, norm_ffn_post, w_in, mu_shift, decay_bias, w_decay_up, iclr_bias, w_iclr_up, w_gate_up, k_k, k_a, r_k, ln_x_w, ln_x_b, vres_bias, w_vres_down, w_vres_up, pool_map, pool_scale, w_branch_rwkv, w_branch_pool, w_out, w_ffn_in, w_ffn_out):
    weights = dict(norm_mix_pre=norm_mix_pre, norm_mix_post=norm_mix_post, norm_ffn_pre=norm_ffn_pre,
                   norm_ffn_post=norm_ffn_post, w_in=w_in, mu_shift=mu_shift, decay_bias=decay_bias,
                   w_decay_up=w_decay_up, iclr_bias=iclr_bias, w_iclr_up=w_iclr_up, w_gate_up=w_gate_up,
                   k_k=k_k, k_a=k_a, r_k=r_k, ln_x_w=ln_x_w, ln_x_b=ln_x_b, vres_bias=vres_bias,
                   w_vres_down=w_vres_down, w_vres_up=w_vres_up, pool_map=pool_map, pool_scale=pool_scale,
                   w_branch_rwkv=w_branch_rwkv, w_branch_pool=w_branch_pool, w_out=w_out,
                   w_ffn_in=w_ffn_in, w_ffn_out=w_ffn_out)
    depth = w_in.shape[0]
    bp, tp, d_model = x_prompt.shape
    bs, ts, _ = x_sample.shape
    tm_p = _tile_rows(bp, tp, 256)
    tm_s = _tile_rows(bs, ts, 256)
    invc_p = _prompt_inv_count(tp)
    band_s, invc_s = _sample_pool_consts(tm_s, ts)
    h0_all = _to_pairs(state_wkv)

    yp = x_prompt.reshape(bp * tp, d_model)
    ys = x_sample.reshape(bs * ts, d_model)
    vf_p = vf_s = None
    outs = {k: [] for k in ('wkv_p', 'shift_p', 'pool_p', 'wkv_s', 'shift_s', 'pool_s')}
    for l in range(depth):
        lw = _layer_weights(l, weights)

        (r, kp, v, kkn, a, ld, g, bonus, u, gates, last) = _proj_call(yp, tp, tm_p, lw, vf_p, None)
        if l == 0:
            vf_p = v
        o, hout = _wkv_call((r, kp, v, kkn, a, ld), bp, tp)
        x1 = _merge_call(o, bonus, g, u, u, POOL_HIST, gates, yp, tp, 2 * tm_p, None, invc_p, lw)
        yp = _ffn_call(x1, 2 * tm_p, lw)
        outs['wkv_p'].append(hout)
        tiles = tp // tm_p
        outs['shift_p'].append(last.reshape(bp, tiles, CARRY, D_SHIFT)[:, -1, -1])
        outs['pool_p'].append(u.reshape(bp, tp, D_POOL)[:, -POOL_BUF:])

        first = jnp.zeros((bs, ts, D_SHIFT), F32).at[:, 0].set(state_shift[l]).reshape(bs * ts, D_SHIFT)
        (r, kp, v, kkn, a, ld, g, bonus, u, gates, ps_all) = _proj_call(ys, ts, tm_s, lw, vf_s, first)
        if l == 0:
            vf_s = v
        o, hout = _wkv_short_call((r, kp, v, kkn, a, ld), bs, ts, h0_all[l])
        hist = jnp.pad(state_pool[l], ((0, 0), (1, 0), (0, 0))).reshape(bs * POOL_SLOTS, D_POOL)
        x1 = _merge_call(o, bonus, g, u, hist, (tm_s // ts) * POOL_SLOTS, gates, ys, ts, tm_s, band_s, invc_s,
                         lw)
        ys = _ffn_call(x1, tm_s, lw)
        outs['wkv_s'].append(hout)
        outs['shift_s'].append(ps_all.reshape(bs, ts, D_SHIFT)[:, -1])
        outs['pool_s'].append(jnp.concatenate(
            [state_pool[l], u.reshape(bs, ts, D_POOL)], axis=1)[:, -POOL_BUF:])

    return (yp.reshape(bp, tp, d_model), ys.reshape(bs, ts, d_model),
            _from_pairs(jnp.stack(outs['wkv_p'])), jnp.stack(outs['shift_p']), jnp.stack(outs['pool_p']),
            _from_pairs(jnp.stack(outs['wkv_s'])), jnp.stack(outs['shift_s']), jnp.stack(outs['pool_s']))
```

```python
import functools

import numpy as np
import jax
import jax.numpy as jnp
from jax.experimental import pallas as pl
from jax.experimental.pallas import tpu as pltpu

F32 = jnp.float32
BF16 = jnp.bfloat16

HEAD = 64
D_RWKV = 768
N_HEADS = D_RWKV // HEAD
PAIR = 2 * HEAD
N_PAIRS = D_RWKV // PAIR
D_POOL = 256
POOL_WINDOWS = (2, 4, 8, 16)
POOL_GROUP = D_POOL // len(POOL_WINDOWS)
POOL_BUF = max(POOL_WINDOWS) - 1
POOL_SLOTS = POOL_BUF + 1
LORA_DECAY = 64
LORA_ICLR = 64
LORA_GATE = 128
LORA_VRES = 32
D_SHIFT = 3 * D_RWKV + LORA_DECAY + LORA_ICLR + LORA_GATE
SAMPLE_START_POS = 16384
RMS_EPS = 1e-6
GN_EPS = 1e-5 * HEAD
DECAY_SCALE = float(np.exp(-0.5))
CHUNK = 64
WKV_SUB = 4
POOL_HIST = 32
CARRY = 8
LANES = 128
SUM_TILE = 256
VMEM_LIMIT = 56 * 1024 * 1024


def _dot(a, b):
    return jnp.dot(a.astype(BF16), b.astype(BF16), preferred_element_type=F32)


def _dot_nt(a, b):
    return jax.lax.dot_general(a.astype(BF16), b.astype(BF16), (((1,), (1,)), ((), ())),
                               preferred_element_type=F32)


def _split(x, parts):
    out = []
    for _ in range(parts - 1):
        hi = x.astype(BF16)
        out.append(hi)
        x = x - hi.astype(F32)
    out.append(x.astype(BF16))
    return out


def _dot_exact_lhs(sel, x, parts):
    acc = None
    for p in _split(x, parts):
        t = jnp.dot(sel, p, preferred_element_type=F32)
        acc = t if acc is None else acc + t
    return acc


def _head_sums(x, ones):
    return jnp.concatenate(
        [_dot(x[:, c:c + SUM_TILE], ones) for c in range(0, x.shape[1], SUM_TILE)], axis=1)


def _sigmoid(x):
    return 1.0 / (1.0 + jnp.exp(-x))


def _rms(x, g):
    return x * jax.lax.rsqrt(jnp.mean(x * x, axis=-1, keepdims=True) + RMS_EPS) * g


def _proj_kernel(*refs, tm, seq_len, has_vres, has_first):
    it = iter(refs)
    x_ref, gpre_ref, win_ref, mu_ref, lora_ref, dbias_ref, ibias_ref, wgate_ref = (next(it) for _ in range(8))
    kk_ref, ka_ref, rk_ref, ones_ref = (next(it) for _ in range(4))
    if has_vres:
        vfirst_ref, vbias_ref, wvdown_ref, wvup_ref = (next(it) for _ in range(4))
    if has_first:
        first_ref = next(it)
    (r_out, kp_out, v_out, kkn_out, a_out, ld_out, g_out, bonus_out, u_out, gates_out,
     last_out, p_scr) = (next(it) for _ in range(12))

    gates_end = D_SHIFT + D_POOL + 2 * x_ref.shape[1]
    i = pl.program_id(0)
    if seq_len >= tm:
        tiles_per_seq = seq_len // tm
        new_seq = (i % tiles_per_seq) == 0
    else:
        new_seq = i == 0

    @pl.when(new_seq)
    def _():
        p_scr[0:CARRY, :] = jnp.zeros((CARRY, D_SHIFT), F32)

    xn = _rms(x_ref[...], gpre_ref[...]).astype(BF16)
    p_scr[CARRY:CARRY + tm, :] = jnp.dot(xn, win_ref[:, 0:D_SHIFT], preferred_element_type=F32)

    if has_first:
        row = jax.lax.broadcasted_iota(jnp.int32, (tm, 1), 0)
        seq_start = jax.lax.rem(row, seq_len) == 0

    def mixed(c0, c1):
        ps = p_scr[CARRY:CARRY + tm, c0:c1]
        prev = p_scr[CARRY - 1:CARRY - 1 + tm, c0:c1]
        if has_first:
            prev = jnp.where(seq_start, first_ref[:, c0:c1], prev)
        return ps + (prev - ps) * mu_ref[:, c0:c1]

    o1, o2, o3 = D_RWKV, 2 * D_RWKV, 3 * D_RWKV
    o5 = o3 + LORA_DECAY + LORA_ICLR
    r = mixed(0, o1)
    k = mixed(o1, o2)
    v = mixed(o2, o3)
    xwa = mixed(o3, o5)
    xg = mixed(o5, D_SHIFT)

    lane = jax.lax.broadcasted_iota(jnp.int32, (1, LANES), 1)
    lora_in = jnp.where(lane < LORA_DECAY, jnp.tanh(xwa), xwa)
    lora = _dot(lora_in, lora_ref[...])
    ld_out[...] = -DECAY_SCALE * _sigmoid(dbias_ref[...] + lora[:, 0:D_RWKV])
    a = _sigmoid(ibias_ref[...] + lora[:, D_RWKV:2 * D_RWKV])
    a_out[...] = a
    g_out[...] = _dot(_sigmoid(xg), wgate_ref[...])

    if has_vres:
        vdown = jnp.dot(xn, wvdown_ref[...], preferred_element_type=F32)
        v = v + (vfirst_ref[...] - v) * _sigmoid(vbias_ref[...] + _dot(vdown, wvup_ref[...]))
    v_out[...] = v

    kk = k * kk_ref[...]
    norm = jnp.sqrt(_head_sums(kk * kk, ones_ref[...]))
    kkn_out[...] = kk / jnp.maximum(norm, 1e-12)
    kp = k * (1.0 + (a - 1.0) * ka_ref[...])
    kp_out[...] = kp
    r_out[...] = r
    bonus_out[...] = _head_sums(r * kp * rk_ref[...], ones_ref[...]) * v

    u_out[...] = jnp.dot(xn, win_ref[:, D_SHIFT:D_SHIFT + D_POOL], preferred_element_type=F32)
    gates_out[...] = _sigmoid(jnp.dot(xn, win_ref[:, D_SHIFT + D_POOL:gates_end],
                                      preferred_element_type=F32))

    if has_first:
        last_out[...] = p_scr[CARRY:CARRY + tm, :]
    else:
        tail = p_scr[tm:tm + CARRY, :]
        last_out[...] = tail
        p_scr[0:CARRY, :] = tail


def _full(shape):
    nd = len(shape)
    return pl.BlockSpec(shape, lambda *_: (0,) * nd, pipeline_mode=pl.Buffered(1))


def _const_spec(a):
    if isinstance(a, tuple):
        arr, l = a
        return pl.BlockSpec((None,) + arr.shape[1:], lambda *_: (l,) + (0,) * (arr.ndim - 1),
                            pipeline_mode=pl.Buffered(1))
    return _full(a.shape)


def _const_arg(a):
    return a[0] if isinstance(a, tuple) else a


def _proj_call(x2d, seq_len, tm, lw, vfirst, first):
    n = x2d.shape[0]
    d_model = x2d.shape[1]
    has_vres = vfirst is not None
    has_first = first is not None
    nt = n // tm
    row = lambda w: pl.BlockSpec((tm, w), lambda i: (i, 0))
    ins = [x2d, lw['g_mix_pre'], lw['w_in'], lw['mu'], lw['lora'], lw['decay_bias'], lw['iclr_bias'],
           lw['w_gate_up'], lw['k_k'], lw['k_a'], lw['r_k'], lw['ones']]
    specs = [row(d_model)] + [_const_spec(a) for a in ins[1:]]
    if has_vres:
        ins += [vfirst, lw['vres_bias'], lw['w_vres_down'], lw['w_vres_up']]
        specs += [row(D_RWKV)] + [_const_spec(lw[k]) for k in ('vres_bias', 'w_vres_down', 'w_vres_up')]
    if has_first:
        ins.append(first)
        specs.append(row(D_SHIFT))
    sds = lambda w: jax.ShapeDtypeStruct((n, w), F32)
    last_rows = tm if has_first else CARRY
    out_shape = [sds(D_RWKV)] * 8 + [sds(D_POOL), sds(2 * d_model),
                                     jax.ShapeDtypeStruct((nt * last_rows, D_SHIFT), F32)]
    out_specs = [row(D_RWKV)] * 8 + [row(D_POOL), row(2 * d_model),
                                     pl.BlockSpec((last_rows, D_SHIFT), lambda i: (i, 0))]
    return pl.pallas_call(
        functools.partial(_proj_kernel, tm=tm, seq_len=seq_len, has_vres=has_vres, has_first=has_first),
        grid=(nt,), in_specs=specs, out_specs=out_specs, out_shape=out_shape,
        scratch_shapes=[pltpu.VMEM((tm + CARRY, D_SHIFT), F32)],
        compiler_params=pltpu.CompilerParams(dimension_semantics=("arbitrary",),
                                             vmem_limit_bytes=VMEM_LIMIT),
    )(*[_const_arg(a) for a in ins])


def _wkv_kernel(r_ref, kp_ref, v_ref, kk_ref, a_ref, ld_ref, o_ref, hout_ref, h_scr, *, n_sub):
    c_idx = pl.program_id(1)
    n_steps = pl.num_programs(1)
    C = CHUNK

    lane = jax.lax.broadcasted_iota(jnp.int32, (1, PAIR), 1)
    left = lane < HEAD
    ri = jax.lax.broadcasted_iota(jnp.int32, (PAIR, PAIR), 0)
    ci = jax.lax.broadcasted_iota(jnp.int32, (PAIR, PAIR), 1)
    same = (ri >= HEAD) == (ci >= HEAD)
    strict = same & (ci < ri)
    incl = same & (ci <= ri)
    eye = ri == ci
    eye_f = jnp.where(eye, 1.0, 0.0).astype(F32)

    def stack(x):
        return jnp.concatenate([jnp.where(left, x, 0.0), jnp.where(left, 0.0, x)], axis=0)

    def twice(x):
        return jnp.concatenate([x, x], axis=0)

    @pl.when(c_idx == 0)
    def _():
        h_scr[...] = jnp.zeros(h_scr.shape, F32)

    tr = jax.lax.broadcasted_iota(jnp.int32, (C, C), 0)
    tc = jax.lax.broadcasted_iota(jnp.int32, (C, C), 1)
    tri = jnp.where(tr >= tc, 1.0, 0.0).astype(BF16)

    a_t, r_t, b_t, k_t, b_h, k_h, v_st, w_last = ([] for _ in range(8))
    for j in range(n_sub):
        rows = slice(j * C, (j + 1) * C)
        ld = ld_ref[rows, :]
        cum = _dot_exact_lhs(tri, ld, 3)
        clast = cum[C - 1:C, :]
        e_cum = jnp.exp(cum)
        e_prev = jnp.exp(cum - ld)
        e_neg = jnp.exp(-cum)
        e_rem = jnp.exp(clast - cum)
        wl = jnp.exp(clast)
        kk = kk_ref[rows, :]
        b = kk * a_ref[rows, :]
        kp = kp_ref[rows, :]
        full = (-(kk * e_prev), r_ref[rows, :] * e_cum, b * e_neg, kp * e_neg, b * e_rem, kp * e_rem)
        v = v_ref[rows, :]
        for p in range(N_PAIRS):
            sl = slice(p * PAIR, (p + 1) * PAIR)
            for dst, src in zip((a_t, r_t, b_t, k_t, b_h, k_h), full):
                dst.append(src[:, sl])
            v_st.append(stack(v[:, sl]))
            w_last.append(wl[:, sl])

    units = range(n_sub * N_PAIRS)
    sc = [_dot_nt(jnp.concatenate([a_t[q], r_t[q]], axis=0),
                  jnp.concatenate([stack(b_t[q]), stack(k_t[q])], axis=0)) for q in units]
    n_ab = [jnp.where(strict, twice(s[0:C, 0:PAIR]), 0.0) for s in sc]
    n_ak = [jnp.where(strict, twice(s[0:C, PAIR:2 * PAIR]), 0.0) for s in sc]
    n_rb = [jnp.where(incl, twice(s[C:2 * C, 0:PAIR]), 0.0) for s in sc]
    n_rk = [jnp.where(incl, twice(s[C:2 * C, PAIR:2 * PAIR]), 0.0) for s in sc]

    inv = [eye_f + n for n in n_ab]
    power = n_ab
    y = [_dot(n_ak[q], v_st[q]) for q in units]
    for _ in range(int(np.log2(C)) - 1):
        power = [_dot(m, m) for m in power]
        inv = [inv[q] + _dot(inv[q], power[q]) for q in units]

    au = [_dot(inv[q], jnp.concatenate([stack(a_t[q]), y[q]], axis=1)) for q in units]
    zero = jnp.zeros((PAIR, PAIR), F32)
    big = []
    for q in units:
        rhs = jnp.concatenate([au[q], jnp.concatenate([zero, v_st[q]], axis=1)], axis=0)
        lhs = jnp.concatenate(
            [jnp.concatenate([n_rb[q], n_rk[q]], axis=1),
             jnp.concatenate([stack(b_h[q]).T, stack(k_h[q]).T], axis=1)], axis=0)
        big.append(_dot(lhs, rhs))

    h = [h_scr[p] for p in range(N_PAIRS)]
    for j in range(n_sub):
        rows = slice(j * C, (j + 1) * C)
        qs = [j * N_PAIRS + p for p in range(N_PAIRS)]
        sd = []
        for p, q in enumerate(qs):
            r_hat = stack(r_t[q]) + big[q][0:PAIR, 0:PAIR]
            m_low = big[q][PAIR:2 * PAIR, 0:PAIR]
            sd.append(_dot(jnp.concatenate([r_hat, m_low], axis=0), h[p]))
        for p, q in enumerate(qs):
            o_st = sd[p][0:PAIR] + big[q][0:PAIR, PAIR:2 * PAIR]
            o_ref[rows, p * PAIR:(p + 1) * PAIR] = o_st[0:C] + o_st[C:2 * C]
            w_col = jnp.sum(jnp.where(eye, jnp.broadcast_to(w_last[q], (PAIR, PAIR)), 0.0),
                            axis=1, keepdims=True)
            h[p] = w_col * h[p] + sd[p][PAIR:2 * PAIR] + big[q][PAIR:2 * PAIR, PAIR:2 * PAIR]
    for p in range(N_PAIRS):
        h_scr[p] = h[p]

    @pl.when(c_idx == n_steps - 1)
    def _():
        for p in range(N_PAIRS):
            hout_ref[0, p] = h_scr[p, 0:HEAD, :] + h_scr[p, HEAD:PAIR, :]


def _wkv_call(ops, n_seq, seq_len):
    n_sub = WKV_SUB if seq_len % (WKV_SUB * CHUNK) == 0 else 1
    rows = n_sub * CHUNK
    assert seq_len % rows == 0
    ns = seq_len // rows
    blk = pl.BlockSpec((rows, D_RWKV), lambda b, c: (b * ns + c, 0))
    hspec = pl.BlockSpec((1, N_PAIRS, HEAD, PAIR), lambda b, c: (b, 0, 0, 0))
    return pl.pallas_call(
        functools.partial(_wkv_kernel, n_sub=n_sub),
        grid=(n_seq, ns), in_specs=[blk] * 6, out_specs=[blk, hspec],
        out_shape=[jax.ShapeDtypeStruct((n_seq * seq_len, D_RWKV), F32),
                   jax.ShapeDtypeStruct((n_seq, N_PAIRS, HEAD, PAIR), F32)],
        scratch_shapes=[pltpu.VMEM((N_PAIRS, PAIR, PAIR), F32)],
        compiler_params=pltpu.CompilerParams(dimension_semantics=("arbitrary", "arbitrary"),
                                             vmem_limit_bytes=VMEM_LIMIT),
    )(*ops)


def _wkv_short_kernel(r_ref, kp_ref, v_ref, kk_ref, a_ref, ld_ref, h0_ref, o_ref, hout_ref, *, seq_len):
    C = CHUNK
    L = seq_len
    G = C // L
    lg = int(np.log2(L))

    lane = jax.lax.broadcasted_iota(jnp.int32, (1, PAIR), 1)
    left = lane < HEAD
    ri = jax.lax.broadcasted_iota(jnp.int32, (PAIR, PAIR), 0)
    ci = jax.lax.broadcasted_iota(jnp.int32, (PAIR, PAIR), 1)
    same = jnp.right_shift(ri, lg) == jnp.right_shift(ci, lg)
    strict = same & (ci < ri)
    incl = same & (ci <= ri)
    eye = ri == ci
    eye_f = jnp.where(eye, 1.0, 0.0).astype(F32)
    row_seq = jnp.bitwise_and(jnp.right_shift(jax.lax.broadcasted_iota(jnp.int32, (PAIR, 1), 0), lg), G - 1)

    def stack(x):
        return jnp.concatenate([jnp.where(left, x, 0.0), jnp.where(left, 0.0, x)], axis=0)

    def twice(x):
        return jnp.concatenate([x, x], axis=0)

    tr = jax.lax.broadcasted_iota(jnp.int32, (C, C), 0)
    tc = jax.lax.broadcasted_iota(jnp.int32, (C, C), 1)
    same_t = jnp.right_shift(tr, lg) == jnp.right_shift(tc, lg)
    tri = jnp.where(same_t & (tr >= tc), 1.0, 0.0).astype(BF16)
    blk = jnp.where(same_t, 1.0, 0.0).astype(BF16)

    ld = ld_ref[...]
    cum = _dot_exact_lhs(tri, ld, 3)
    tot = _dot_exact_lhs(blk, ld, 3)
    e_cum = jnp.exp(cum)
    e_prev = jnp.exp(cum - ld)
    e_neg = jnp.exp(-cum)
    e_rem = jnp.exp(tot - cum)
    w_tot = jnp.exp(tot)
    kk = kk_ref[...]
    b = kk * a_ref[...]
    kp = kp_ref[...]
    v = v_ref[...]
    a_full, r_full, bt_full, kt_full = -(kk * e_prev), r_ref[...] * e_cum, b * e_neg, kp * e_neg
    bh_full, kh_full = b * e_rem, kp * e_rem

    pairs = range(N_PAIRS)
    sls = [slice(p * PAIR, (p + 1) * PAIR) for p in pairs]
    v_st = [stack(v[:, sl]) for sl in sls]
    sc = [_dot_nt(jnp.concatenate([a_full[:, sl], r_full[:, sl]], axis=0),
                  jnp.concatenate([stack(bt_full[:, sl]), stack(kt_full[:, sl])], axis=0)) for sl in sls]
    n_ab = [jnp.where(strict, twice(s[0:C, 0:PAIR]), 0.0) for s in sc]
    n_ak = [jnp.where(strict, twice(s[0:C, PAIR:2 * PAIR]), 0.0) for s in sc]
    n_rb = [jnp.where(incl, twice(s[C:2 * C, 0:PAIR]), 0.0) for s in sc]
    n_rk = [jnp.where(incl, twice(s[C:2 * C, PAIR:2 * PAIR]), 0.0) for s in sc]

    inv = [eye_f + n for n in n_ab]
    power = n_ab
    y = [_dot(n_ak[p], v_st[p]) for p in pairs]
    for _ in range(lg - 1):
        power = [_dot(m, m) for m in power]
        inv = [inv[p] + _dot(inv[p], power[p]) for p in pairs]
    au = [_dot(inv[p], jnp.concatenate([stack(a_full[:, sls[p]]), y[p]], axis=1)) for p in pairs]
    zero = jnp.zeros((PAIR, PAIR), F32)
    top = [_dot(jnp.concatenate([n_rb[p], n_rk[p]], axis=1),
                jnp.concatenate([au[p], jnp.concatenate([zero, v_st[p]], axis=1)], axis=0)) for p in pairs]
    r_hat = [stack(r_full[:, sls[p]]) + top[p][:, 0:PAIR] for p in pairs]
    bk_t = [jnp.concatenate([stack(bh_full[:, sls[p]]).T, stack(kh_full[:, sls[p]]).T], axis=1) for p in pairs]

    def load_state(s, p):
        return h0_ref[s, 2 * p:2 * p + 2].reshape(PAIR, HEAD).T

    h0 = [[stack(load_state(s, p)) for s in range(G)] for p in pairs]
    sd = [[None] * G for _ in pairs]
    for p in pairs:
        a_hat = au[p][:, 0:PAIR]
        for s in range(G):
            r0, r1 = slice(s * L, (s + 1) * L), slice(C + s * L, C + (s + 1) * L)
            lhs = jnp.concatenate([a_hat[r0], a_hat[r1], r_hat[p][r0], r_hat[p][r1]], axis=0)
            sd[p][s] = _dot(lhs, h0[p][s])
    u_full = []
    for p in pairs:
        u_full.append(jnp.concatenate([sd[p][s][0:L] for s in range(G)]
                                      + [sd[p][s][L:2 * L] for s in range(G)], axis=0) + au[p][:, PAIR:2 * PAIR])
        o_st = jnp.concatenate([sd[p][s][2 * L:3 * L] for s in range(G)]
                               + [sd[p][s][3 * L:4 * L] for s in range(G)], axis=0) + top[p][:, PAIR:2 * PAIR]
        o_ref[:, sls[p]] = o_st[0:C] + o_st[C:2 * C]
    inc = [[None] * G for _ in pairs]
    for p in pairs:
        for s in range(G):
            mine = row_seq == s
            rhs = jnp.concatenate([jnp.where(mine, u_full[p], 0.0), jnp.where(mine, v_st[p], 0.0)], axis=0)
            inc[p][s] = _dot(bk_t[p], rhs)
    for p in pairs:
        for s in range(G):
            w_row = w_tot[s * L:s * L + 1, sls[p]]
            w_col = jnp.sum(jnp.where(eye, jnp.broadcast_to(w_row, (PAIR, PAIR)), 0.0), axis=1, keepdims=True)
            h1 = w_col * h0[p][s] + inc[p][s]
            hout_ref[s, 2 * p:2 * p + 2] = (h1[0:HEAD] + h1[HEAD:PAIR]).T.reshape(2, HEAD, HEAD)


def _wkv_short_call(ops, n_seq, seq_len, state, layer):
    assert CHUNK % seq_len == 0 and seq_len % CARRY == 0 and seq_len & (seq_len - 1) == 0
    group = CHUNK // seq_len
    assert n_seq % group == 0
    blk = pl.BlockSpec((CHUNK, D_RWKV), lambda i: (i, 0))
    in_state = pl.BlockSpec((None, group, N_HEADS, HEAD, HEAD), lambda i: (layer, i, 0, 0, 0))
    out_state = pl.BlockSpec((group, N_HEADS, HEAD, HEAD), lambda i: (i, 0, 0, 0))
    return pl.pallas_call(
        functools.partial(_wkv_short_kernel, seq_len=seq_len),
        grid=(n_seq // group,), in_specs=[blk] * 6 + [in_state], out_specs=[blk, out_state],
        out_shape=[jax.ShapeDtypeStruct((n_seq * seq_len, D_RWKV), F32),
                   jax.ShapeDtypeStruct(state.shape[1:], F32)],
        compiler_params=pltpu.CompilerParams(dimension_semantics=("arbitrary",),
                                             vmem_limit_bytes=VMEM_LIMIT),
    )(*ops, state)


def _merge_kernel(*refs, tm, tiles_per_seq, banded):
    it = iter(refs)
    o_ref, bonus_ref, g_ref, u_ref, hist_ref, gates_ref, x_ref = (next(it) for _ in range(7))
    if banded:
        band_ref = next(it)
    (invc_ref, lnw_ref, lnb_ref, ones_ref, pmap_ref, pscale_ref, wba_ref, wbb_ref, wout_ref, gpost_ref,
     y_ref) = (next(it) for _ in range(11))
    if not banded:
        s_a, s_b = next(it), next(it)
    d_model = x_ref.shape[1]
    inv_h = 1.0 / HEAD

    o = o_ref[...]
    mean = _head_sums(o, ones_ref[...]) * inv_h
    cen = o - mean
    var = _head_sums(cen * cen, ones_ref[...]) * inv_h
    o_n = cen * jax.lax.rsqrt(var + GN_EPS) * lnw_ref[...] + lnb_ref[...]
    o_rwkv = (o_n + bonus_ref[...]) * g_ref[...]

    u = u_ref[...]
    lane = jax.lax.broadcasted_iota(jnp.int32, (1, D_POOL), 1)
    win_sum = jnp.zeros((tm, D_POOL), F32)
    if banded:
        full = jnp.concatenate([hist_ref[...], u], axis=0)
        for gi in range(len(POOL_WINDOWS)):
            s = _dot_exact_lhs(band_ref[gi], full, 2)
            in_group = (lane >= gi * POOL_GROUP) & (lane < (gi + 1) * POOL_GROUP)
            win_sum = jnp.where(in_group, s, win_sum)
    else:
        keep = jnp.where((pl.program_id(0) % tiles_per_seq) == 0, 0.0, 1.0)
        s_a[0:POOL_HIST, :] = hist_ref[...] * keep
        s_a[POOL_HIST:POOL_HIST + tm, :] = u
        src, dst = s_a, s_b
        total = POOL_HIST + tm
        w = 1
        for gi, win in enumerate(POOL_WINDOWS):
            while w < win:
                lo = CARRY * int(np.log2(2 * w))
                dst[lo:total, :] = src[lo:total, :] + src[lo - w:total - w, :]
                src, dst = dst, src
                w *= 2
            in_group = (lane >= gi * POOL_GROUP) & (lane < (gi + 1) * POOL_GROUP)
            win_sum = jnp.where(in_group, src[POOL_HIST:total, :], win_sum)
    diff = win_sum * invc_ref[...] - u
    o_pool = _dot(diff, pmap_ref[...]) * pscale_ref[...]

    gates = gates_ref[...]
    merged = (gates[:, 0:d_model] * _dot(o_rwkv, wba_ref[...])
              + gates[:, d_model:2 * d_model] * _dot(o_pool, wbb_ref[...]))
    y_ref[...] = x_ref[...] + _rms(_dot(merged, wout_ref[...]), gpost_ref[...])


def _merge_call(o, bonus, g, u, hist, hist_rows, gates, x2d, seq_len, tm, band, invc, lw):
    n, d_model = x2d.shape
    nt = n // tm
    tiles_per_seq = max(seq_len // tm, 1)
    banded = band is not None
    row = lambda w: pl.BlockSpec((tm, w), lambda i: (i, 0))
    if banded:
        hist_spec = pl.BlockSpec((hist_rows, D_POOL), lambda i: (i, 0))
        invc_spec = _full(invc.shape)
        scratch = []
    else:
        per = tm // hist_rows
        hist_spec = pl.BlockSpec((hist_rows, D_POOL), lambda i: (jnp.maximum(i * per - 1, 0), 0))
        invc_spec = pl.BlockSpec((tm, D_POOL), lambda i: (i % tiles_per_seq, 0))
        scratch = [pltpu.VMEM((hist_rows + tm, D_POOL), F32)] * 2
    consts = [lw['ln_x_w'], lw['ln_x_b'], lw['ones'], lw['pool_map'], lw['pool_scale'],
              lw['w_branch_rwkv'], lw['w_branch_pool'], lw['w_out'], lw['g_mix_post']]
    ins = [o, bonus, g, u, hist, gates, x2d] + ([band] if banded else []) + [invc] + consts
    specs = ([row(D_RWKV)] * 3 + [row(D_POOL), hist_spec, row(2 * d_model), row(d_model)]
             + ([_full(band.shape)] if banded else []) + [invc_spec] + [_const_spec(a) for a in consts])
    return pl.pallas_call(
        functools.partial(_merge_kernel, tm=tm, tiles_per_seq=tiles_per_seq, banded=banded),
        grid=(nt,), in_specs=specs, out_specs=row(d_model),
        out_shape=jax.ShapeDtypeStruct((n, d_model), F32), scratch_shapes=scratch,
        compiler_params=pltpu.CompilerParams(dimension_semantics=("arbitrary",),
                                             vmem_limit_bytes=VMEM_LIMIT),
    )(*[_const_arg(a) for a in ins])


def _ffn_kernel(x_ref, gpre_ref, win_ref, wout_ref, gpost_ref, y_ref):
    x = x_ref[...]
    d_ff = wout_ref.shape[0]
    h = _rms(x, gpre_ref[...]).astype(BF16)
    gu = jnp.dot(h, win_ref[...], preferred_element_type=F32)
    gt = gu[:, 0:d_ff]
    act = gt * _sigmoid(gt) * gu[:, d_ff:2 * d_ff]
    y_ref[...] = x + _rms(_dot(act, wout_ref[...]), gpost_ref[...])


def _ffn_call(x2d, tm, lw):
    n, d_model = x2d.shape
    row = pl.BlockSpec((tm, d_model), lambda i: (i, 0))
    consts = [lw['g_ffn_pre'], lw['w_ffn_in'], lw['w_ffn_out'], lw['g_ffn_post']]
    return pl.pallas_call(
        _ffn_kernel, grid=(n // tm,), in_specs=[row] + [_const_spec(a) for a in consts],
        out_specs=row, out_shape=jax.ShapeDtypeStruct((n, d_model), F32),
        compiler_params=pltpu.CompilerParams(dimension_semantics=("arbitrary",),
                                             vmem_limit_bytes=VMEM_LIMIT),
    )(x2d, *[_const_arg(a) for a in consts])


def _prompt_inv_count(seq_len):
    pos = np.arange(seq_len)[:, None]
    win = np.repeat(np.array(POOL_WINDOWS), POOL_GROUP)[None, :]
    return jnp.asarray(1.0 / np.minimum(pos + 1, win).astype(np.float32), F32)


def _sample_pool_consts(tm, seq_len):
    ns = tm // seq_len
    rs, rt = np.divmod(np.arange(tm), seq_len)
    hs, hj = np.divmod(np.arange(ns * POOL_SLOTS), POOL_SLOTS)
    cs = np.concatenate([hs, rs])[None, :]
    cpos = np.concatenate([hj - 1, POOL_BUF + rt])[None, :]
    dist = (POOL_BUF + rt)[:, None] - cpos
    same = (rs[:, None] == cs) & (cpos >= 0)
    band = np.stack([same & (dist >= 0) & (dist < w) for w in POOL_WINDOWS]).astype(np.float32)
    win = np.repeat(np.array(POOL_WINDOWS), POOL_GROUP)[None, :]
    pos = (SAMPLE_START_POS + rt)[:, None]
    invc = 1.0 / np.minimum(pos + 1, win).astype(np.float32)
    return jnp.asarray(band, BF16), jnp.asarray(invc, F32)


def _block_diag(blocks):
    n = len(blocks)
    rows = []
    for i, blk in enumerate(blocks):
        rows.append(jnp.concatenate(
            [blk if j == i else jnp.zeros((blk.shape[0], blocks[j].shape[1]), blk.dtype) for j in range(n)],
            axis=1))
    return jnp.concatenate(rows, axis=0)


BIG_WEIGHTS = ('w_in', 'w_gate_up', 'w_branch_rwkv', 'w_branch_pool', 'w_out', 'w_ffn_in', 'w_ffn_out')


def _layer_weights(l, w, wb):
    row = lambda a: a.reshape(1, -1).astype(F32)
    lw = {k: (wb[k], l) for k in BIG_WEIGHTS}
    lw.update({
        'g_mix_pre': row(w['norm_mix_pre'][l]), 'g_mix_post': row(w['norm_mix_post'][l]),
        'g_ffn_pre': row(w['norm_ffn_pre'][l]), 'g_ffn_post': row(w['norm_ffn_post'][l]),
        'mu': row(w['mu_shift'][l]), 'decay_bias': row(w['decay_bias'][l]), 'iclr_bias': row(w['iclr_bias'][l]),
        'lora': _block_diag([w['w_decay_up'][l], w['w_iclr_up'][l]]).astype(BF16),
        'k_k': row(w['k_k'][l]), 'k_a': row(w['k_a'][l]), 'r_k': row(w['r_k'][l]),
        'ln_x_w': row(w['ln_x_w'][l]), 'ln_x_b': row(w['ln_x_b'][l]),
        'pool_map': _block_diag([w['pool_map'][l, gi] for gi in range(len(POOL_WINDOWS))]).astype(BF16),
        'pool_scale': row(w['pool_scale'][l]),
        'ones': jnp.asarray(np.kron(np.eye(SUM_TILE // HEAD), np.ones((HEAD, HEAD))), BF16),
    })
    if l > 0:
        pad = LANES - LORA_VRES
        lw['vres_bias'] = row(w['vres_bias'][l - 1])
        lw['w_vres_down'] = jnp.pad(w['w_vres_down'][l - 1], ((0, 0), (0, pad))).astype(BF16)
        lw['w_vres_up'] = jnp.pad(w['w_vres_up'][l - 1], ((0, pad), (0, 0))).astype(BF16)
    return lw


def _from_pairs(h):
    lead = h.shape[:-3]
    n = len(lead)
    h = h.reshape(*lead, N_PAIRS, HEAD, 2, HEAD)
    return h.transpose(*range(n), n, n + 2, n + 3, n + 1).reshape(*lead, N_HEADS, HEAD, HEAD)


def _tile_rows(n_seq, seq_len, target):
    if seq_len >= target:
        return target
    return min(n_seq * seq_len, target)


def kernel(x_prompt, x_sample, state_wkv, state_shift, state_pool, norm_mix_pre, norm_mix_post, norm_ffn_pre, norm_ffn_post, w_in, mu_shift, decay_bias, w_decay_up, iclr_bias, w_iclr_up, w_gate_up, k_k, k_a, r_k, ln_x_w, ln_x_b, vres_bias, w_vres_down, w_vres_up, pool_map, pool_scale, w_branch_rwkv, w_branch_pool, w_out, w_ffn_in, w_ffn_out):
    weights = dict(norm_mix_pre=norm_mix_pre, norm_mix_post=norm_mix_post, norm_ffn_pre=norm_ffn_pre,
                   norm_ffn_post=norm_ffn_post, w_in=w_in, mu_shift=mu_shift, decay_bias=decay_bias,
                   w_decay_up=w_decay_up, iclr_bias=iclr_bias, w_iclr_up=w_iclr_up, w_gate_up=w_gate_up,
                   k_k=k_k, k_a=k_a, r_k=r_k, ln_x_w=ln_x_w, ln_x_b=ln_x_b, vres_bias=vres_bias,
                   w_vres_down=w_vres_down, w_vres_up=w_vres_up, pool_map=pool_map, pool_scale=pool_scale,
                   w_branch_rwkv=w_branch_rwkv, w_branch_pool=w_branch_pool, w_out=w_out,
                   w_ffn_in=w_ffn_in, w_ffn_out=w_ffn_out)
    depth = w_in.shape[0]
    wb = {k: weights[k].astype(BF16) for k in BIG_WEIGHTS}
    bp, tp, d_model = x_prompt.shape
    bs, ts, _ = x_sample.shape
    tm_p = _tile_rows(bp, tp, 256)
    tm_s = _tile_rows(bs, ts, 256)
    invc_p = _prompt_inv_count(tp)
    band_s, invc_s = _sample_pool_consts(tm_s, ts)

    yp = x_prompt.reshape(bp * tp, d_model)
    ys = x_sample.reshape(bs * ts, d_model)
    vf_p = vf_s = None
    outs = {k: [] for k in ('wkv_p', 'shift_p', 'pool_p', 'wkv_s', 'shift_s', 'pool_s')}
    for l in range(depth):
        lw = _layer_weights(l, weights, wb)

        (r, kp, v, kkn, a, ld, g, bonus, u, gates, last) = _proj_call(yp, tp, tm_p, lw, vf_p, None)
        if l == 0:
            vf_p = v
        o, hout = _wkv_call((r, kp, v, kkn, a, ld), bp, tp)
        x1 = _merge_call(o, bonus, g, u, u, POOL_HIST, gates, yp, tp, 2 * tm_p, None, invc_p, lw)
        yp = _ffn_call(x1, 2 * tm_p, lw)
        outs['wkv_p'].append(_from_pairs(hout))
        tiles = tp // tm_p
        outs['shift_p'].append(last.reshape(bp, tiles, CARRY, D_SHIFT)[:, -1, -1])
        outs['pool_p'].append(u.reshape(bp, tp, D_POOL)[:, -POOL_BUF:])

        first = jnp.zeros((bs, ts, D_SHIFT), F32).at[:, 0].set(state_shift[l]).reshape(bs * ts, D_SHIFT)
        (r, kp, v, kkn, a, ld, g, bonus, u, gates, ps_all) = _proj_call(ys, ts, tm_s, lw, vf_s, first)
        if l == 0:
            vf_s = v
        o, hout = _wkv_short_call((r, kp, v, kkn, a, ld), bs, ts, state_wkv, l)
        outs['wkv_s'].append(hout)
        hist = jnp.pad(state_pool[l], ((0, 0), (1, 0), (0, 0))).reshape(bs * POOL_SLOTS, D_POOL)
        x1 = _merge_call(o, bonus, g, u, hist, (tm_s // ts) * POOL_SLOTS, gates, ys, ts, tm_s, band_s, invc_s,
                         lw)
        ys = _ffn_call(x1, tm_s, lw)
        outs['shift_s'].append(ps_all.reshape(bs, ts, D_SHIFT)[:, -1])
        outs['pool_s'].append(jnp.concatenate(
            [state_pool[l], u.reshape(bs, ts, D_POOL)], axis=1)[:, -POOL_BUF:])

    return (yp.reshape(bp, tp, d_model), ys.reshape(bs, ts, d_model),
            jnp.stack(outs['wkv_p']), jnp.stack(outs['shift_p']), jnp.stack(outs['pool_p']),
            jnp.stack(outs['wkv_s']), jnp.stack(outs['shift_s']), jnp.stack(outs['pool_s']))
```

```python
import functools

import numpy as np
import jax
import jax.numpy as jnp
from jax.experimental import pallas as pl
from jax.experimental.pallas import tpu as pltpu

F32 = jnp.float32
BF16 = jnp.bfloat16

HEAD = 64
D_RWKV = 768
N_HEADS = D_RWKV // HEAD
PAIR = 2 * HEAD
N_PAIRS = D_RWKV // PAIR
D_POOL = 256
POOL_WINDOWS = (2, 4, 8, 16)
POOL_GROUP = D_POOL // len(POOL_WINDOWS)
POOL_BUF = max(POOL_WINDOWS) - 1
POOL_SLOTS = POOL_BUF + 1
LORA_DECAY = 64
LORA_ICLR = 64
LORA_GATE = 128
LORA_VRES = 32
D_SHIFT = 3 * D_RWKV + LORA_DECAY + LORA_ICLR + LORA_GATE
SAMPLE_START_POS = 16384
RMS_EPS = 1e-6
GN_EPS = 1e-5 * HEAD
DECAY_SCALE = float(np.exp(-0.5))
CHUNK = 64
WKV_SUB = 4
POOL_HIST = 32
CARRY = 8
LANES = 128
SUM_TILE = 256
VMEM_LIMIT = 56 * 1024 * 1024


def _dot(a, b):
    return jnp.dot(a.astype(BF16), b.astype(BF16), preferred_element_type=F32)


def _dot_nt(a, b):
    return jax.lax.dot_general(a.astype(BF16), b.astype(BF16), (((1,), (1,)), ((), ())),
                               preferred_element_type=F32)


def _split(x, parts):
    out = []
    for _ in range(parts - 1):
        hi = x.astype(BF16)
        out.append(hi)
        x = x - hi.astype(F32)
    out.append(x.astype(BF16))
    return out


def _dot_exact_lhs(sel, x, parts):
    acc = None
    for p in _split(x, parts):
        t = jnp.dot(sel, p, preferred_element_type=F32)
        acc = t if acc is None else acc + t
    return acc


def _head_sums(x, ones):
    return jnp.concatenate(
        [_dot(x[:, c:c + SUM_TILE], ones) for c in range(0, x.shape[1], SUM_TILE)], axis=1)


def _sigmoid(x):
    return 1.0 / (1.0 + jnp.exp(-x))


def _rms(x, g):
    return x * jax.lax.rsqrt(jnp.mean(x * x, axis=-1, keepdims=True) + RMS_EPS) * g


def _proj_kernel(*refs, tm, seq_len, has_vres, has_first):
    it = iter(refs)
    x_ref, gpre_ref, win_ref, mu_ref, lora_ref, dbias_ref, ibias_ref, wgate_ref = (next(it) for _ in range(8))
    kk_ref, ka_ref, rk_ref, ones_ref = (next(it) for _ in range(4))
    if has_vres:
        vfirst_ref, vbias_ref, wvdown_ref, wvup_ref = (next(it) for _ in range(4))
    if has_first:
        first_ref = next(it)
    (r_out, kp_out, v_out, kkn_out, a_out, ld_out, g_out, bonus_out, u_out, gates_out,
     last_out, p_scr) = (next(it) for _ in range(12))

    gates_end = D_SHIFT + D_POOL + 2 * x_ref.shape[1]
    i = pl.program_id(0)
    if seq_len >= tm:
        tiles_per_seq = seq_len // tm
        new_seq = (i % tiles_per_seq) == 0
    else:
        new_seq = i == 0

    @pl.when(new_seq)
    def _():
        p_scr[0:CARRY, :] = jnp.zeros((CARRY, D_SHIFT), F32)

    xn = _rms(x_ref[...], gpre_ref[...]).astype(BF16)
    p_scr[CARRY:CARRY + tm, :] = jnp.dot(xn, win_ref[:, 0:D_SHIFT], preferred_element_type=F32)

    if has_first:
        row = jax.lax.broadcasted_iota(jnp.int32, (tm, 1), 0)
        seq_start = jax.lax.rem(row, seq_len) == 0

    def mixed(c0, c1):
        ps = p_scr[CARRY:CARRY + tm, c0:c1]
        prev = p_scr[CARRY - 1:CARRY - 1 + tm, c0:c1]
        if has_first:
            prev = jnp.where(seq_start, first_ref[:, c0:c1], prev)
        return ps + (prev - ps) * mu_ref[:, c0:c1]

    o1, o2, o3 = D_RWKV, 2 * D_RWKV, 3 * D_RWKV
    o5 = o3 + LORA_DECAY + LORA_ICLR
    r = mixed(0, o1)
    k = mixed(o1, o2)
    v = mixed(o2, o3)
    xwa = mixed(o3, o5)
    xg = mixed(o5, D_SHIFT)

    lane = jax.lax.broadcasted_iota(jnp.int32, (1, LANES), 1)
    lora_in = jnp.where(lane < LORA_DECAY, jnp.tanh(xwa), xwa)
    lora = _dot(lora_in, lora_ref[...])
    ld_out[...] = -DECAY_SCALE * _sigmoid(dbias_ref[...] + lora[:, 0:D_RWKV])
    a = _sigmoid(ibias_ref[...] + lora[:, D_RWKV:2 * D_RWKV])
    a_out[...] = a
    g_out[...] = _dot(_sigmoid(xg), wgate_ref[...])

    if has_vres:
        vdown = jnp.dot(xn, wvdown_ref[...], preferred_element_type=F32)
        v = v + (vfirst_ref[...] - v) * _sigmoid(vbias_ref[...] + _dot(vdown, wvup_ref[...]))
    v_out[...] = v

    kk = k * kk_ref[...]
    norm = jnp.sqrt(_head_sums(kk * kk, ones_ref[...]))
    kkn_out[...] = kk / jnp.maximum(norm, 1e-12)
    kp = k * (1.0 + (a - 1.0) * ka_ref[...])
    kp_out[...] = kp
    r_out[...] = r
    bonus_out[...] = _head_sums(r * kp * rk_ref[...], ones_ref[...]) * v

    u_out[...] = jnp.dot(xn, win_ref[:, D_SHIFT:D_SHIFT + D_POOL], preferred_element_type=F32)
    gates_out[...] = _sigmoid(jnp.dot(xn, win_ref[:, D_SHIFT + D_POOL:gates_end],
                                      preferred_element_type=F32))

    if has_first:
        last_out[...] = p_scr[CARRY:CARRY + tm, :]
    else:
        tail = p_scr[tm:tm + CARRY, :]
        last_out[...] = tail
        p_scr[0:CARRY, :] = tail


def _full(shape):
    nd = len(shape)
    return pl.BlockSpec(shape, lambda *_: (0,) * nd, pipeline_mode=pl.Buffered(1))


def _const_spec(a):
    if isinstance(a, tuple):
        arr, l = a
        return pl.BlockSpec((None,) + arr.shape[1:], lambda *_: (l,) + (0,) * (arr.ndim - 1),
                            pipeline_mode=pl.Buffered(1))
    return _full(a.shape)


def _const_arg(a):
    return a[0] if isinstance(a, tuple) else a


def _proj_call(x2d, seq_len, tm, lw, vfirst, first):
    n = x2d.shape[0]
    d_model = x2d.shape[1]
    has_vres = vfirst is not None
    has_first = first is not None
    nt = n // tm
    row = lambda w: pl.BlockSpec((tm, w), lambda i: (i, 0))
    ins = [x2d, lw['g_mix_pre'], lw['w_in'], lw['mu'], lw['lora'], lw['decay_bias'], lw['iclr_bias'],
           lw['w_gate_up'], lw['k_k'], lw['k_a'], lw['r_k'], lw['ones']]
    specs = [row(d_model)] + [_const_spec(a) for a in ins[1:]]
    if has_vres:
        ins += [vfirst, lw['vres_bias'], lw['w_vres_down'], lw['w_vres_up']]
        specs += [row(D_RWKV)] + [_const_spec(lw[k]) for k in ('vres_bias', 'w_vres_down', 'w_vres_up')]
    if has_first:
        ins.append(first)
        specs.append(row(D_SHIFT))
    sds = lambda w: jax.ShapeDtypeStruct((n, w), F32)
    last_rows = tm if has_first else CARRY
    out_shape = [sds(D_RWKV)] * 8 + [sds(D_POOL), sds(2 * d_model),
                                     jax.ShapeDtypeStruct((nt * last_rows, D_SHIFT), F32)]
    out_specs = [row(D_RWKV)] * 8 + [row(D_POOL), row(2 * d_model),
                                     pl.BlockSpec((last_rows, D_SHIFT), lambda i: (i, 0))]
    return pl.pallas_call(
        functools.partial(_proj_kernel, tm=tm, seq_len=seq_len, has_vres=has_vres, has_first=has_first),
        grid=(nt,), in_specs=specs, out_specs=out_specs, out_shape=out_shape,
        scratch_shapes=[pltpu.VMEM((tm + CARRY, D_SHIFT), F32)],
        compiler_params=pltpu.CompilerParams(dimension_semantics=("arbitrary",),
                                             vmem_limit_bytes=VMEM_LIMIT),
    )(*[_const_arg(a) for a in ins])


def _wkv_kernel(r_ref, kp_ref, v_ref, kk_ref, a_ref, ld_ref, o_ref, hout_ref, h_scr, *, n_sub):
    c_idx = pl.program_id(1)
    n_steps = pl.num_programs(1)
    C = CHUNK

    lane = jax.lax.broadcasted_iota(jnp.int32, (1, PAIR), 1)
    left = lane < HEAD
    ri = jax.lax.broadcasted_iota(jnp.int32, (PAIR, PAIR), 0)
    ci = jax.lax.broadcasted_iota(jnp.int32, (PAIR, PAIR), 1)
    same = (ri >= HEAD) == (ci >= HEAD)
    strict = same & (ci < ri)
    incl = same & (ci <= ri)
    eye = ri == ci
    eye_f = jnp.where(eye, 1.0, 0.0).astype(F32)

    def stack(x):
        return jnp.concatenate([jnp.where(left, x, 0.0), jnp.where(left, 0.0, x)], axis=0)

    def twice(x):
        return jnp.concatenate([x, x], axis=0)

    @pl.when(c_idx == 0)
    def _():
        h_scr[...] = jnp.zeros(h_scr.shape, F32)

    tr = jax.lax.broadcasted_iota(jnp.int32, (C, C), 0)
    tc = jax.lax.broadcasted_iota(jnp.int32, (C, C), 1)
    tri = jnp.where(tr >= tc, 1.0, 0.0).astype(BF16)

    a_t, r_t, b_t, k_t, b_h, k_h, v_st, w_last = ([] for _ in range(8))
    for j in range(n_sub):
        rows = slice(j * C, (j + 1) * C)
        ld = ld_ref[rows, :]
        cum = _dot_exact_lhs(tri, ld, 3)
        clast = cum[C - 1:C, :]
        e_cum = jnp.exp(cum)
        e_prev = jnp.exp(cum - ld)
        e_neg = jnp.exp(-cum)
        e_rem = jnp.exp(clast - cum)
        wl = jnp.exp(clast)
        kk = kk_ref[rows, :]
        b = kk * a_ref[rows, :]
        kp = kp_ref[rows, :]
        full = (-(kk * e_prev), r_ref[rows, :] * e_cum, b * e_neg, kp * e_neg, b * e_rem, kp * e_rem)
        v = v_ref[rows, :]
        for p in range(N_PAIRS):
            sl = slice(p * PAIR, (p + 1) * PAIR)
            for dst, src in zip((a_t, r_t, b_t, k_t, b_h, k_h), full):
                dst.append(src[:, sl])
            v_st.append(stack(v[:, sl]))
            w_last.append(wl[:, sl])

    units = range(n_sub * N_PAIRS)
    sc = [_dot_nt(jnp.concatenate([a_t[q], r_t[q]], axis=0),
                  jnp.concatenate([stack(b_t[q]), stack(k_t[q])], axis=0)) for q in units]
    n_ab = [jnp.where(strict, twice(s[0:C, 0:PAIR]), 0.0) for s in sc]
    n_ak = [jnp.where(strict, twice(s[0:C, PAIR:2 * PAIR]), 0.0) for s in sc]
    n_rb = [jnp.where(incl, twice(s[C:2 * C, 0:PAIR]), 0.0) for s in sc]
    n_rk = [jnp.where(incl, twice(s[C:2 * C, PAIR:2 * PAIR]), 0.0) for s in sc]

    inv = [eye_f + n for n in n_ab]
    power = n_ab
    y = [_dot(n_ak[q], v_st[q]) for q in units]
    for _ in range(int(np.log2(C)) - 1):
        power = [_dot(m, m) for m in power]
        inv = [inv[q] + _dot(inv[q], power[q]) for q in units]

    au = [_dot(inv[q], jnp.concatenate([stack(a_t[q]), y[q]], axis=1)) for q in units]
    zero = jnp.zeros((PAIR, PAIR), F32)
    big = []
    for q in units:
        rhs = jnp.concatenate([au[q], jnp.concatenate([zero, v_st[q]], axis=1)], axis=0)
        lhs = jnp.concatenate(
            [jnp.concatenate([n_rb[q], n_rk[q]], axis=1),
             jnp.concatenate([stack(b_h[q]).T, stack(k_h[q]).T], axis=1)], axis=0)
        big.append(_dot(lhs, rhs))

    h = [h_scr[p] for p in range(N_PAIRS)]
    for j in range(n_sub):
        rows = slice(j * C, (j + 1) * C)
        qs = [j * N_PAIRS + p for p in range(N_PAIRS)]
        sd = []
        for p, q in enumerate(qs):
            r_hat = stack(r_t[q]) + big[q][0:PAIR, 0:PAIR]
            m_low = big[q][PAIR:2 * PAIR, 0:PAIR]
            sd.append(_dot(jnp.concatenate([r_hat, m_low], axis=0), h[p]))
        for p, q in enumerate(qs):
            o_st = sd[p][0:PAIR] + big[q][0:PAIR, PAIR:2 * PAIR]
            o_ref[rows, p * PAIR:(p + 1) * PAIR] = o_st[0:C] + o_st[C:2 * C]
            w_col = jnp.sum(jnp.where(eye, jnp.broadcast_to(w_last[q], (PAIR, PAIR)), 0.0),
                            axis=1, keepdims=True)
            h[p] = w_col * h[p] + sd[p][PAIR:2 * PAIR] + big[q][PAIR:2 * PAIR, PAIR:2 * PAIR]
    for p in range(N_PAIRS):
        h_scr[p] = h[p]

    @pl.when(c_idx == n_steps - 1)
    def _():
        for p in range(N_PAIRS):
            hout_ref[0, p] = h_scr[p, 0:HEAD, :] + h_scr[p, HEAD:PAIR, :]


def _wkv_call(ops, n_seq, seq_len):
    n_sub = WKV_SUB if seq_len % (WKV_SUB * CHUNK) == 0 else 1
    rows = n_sub * CHUNK
    assert seq_len % rows == 0
    ns = seq_len // rows
    blk = pl.BlockSpec((rows, D_RWKV), lambda b, c: (b * ns + c, 0))
    hspec = pl.BlockSpec((1, N_PAIRS, HEAD, PAIR), lambda b, c: (b, 0, 0, 0))
    return pl.pallas_call(
        functools.partial(_wkv_kernel, n_sub=n_sub),
        grid=(n_seq, ns), in_specs=[blk] * 6, out_specs=[blk, hspec],
        out_shape=[jax.ShapeDtypeStruct((n_seq * seq_len, D_RWKV), F32),
                   jax.ShapeDtypeStruct((n_seq, N_PAIRS, HEAD, PAIR), F32)],
        scratch_shapes=[pltpu.VMEM((N_PAIRS, PAIR, PAIR), F32)],
        compiler_params=pltpu.CompilerParams(dimension_semantics=("arbitrary", "arbitrary"),
                                             vmem_limit_bytes=VMEM_LIMIT),
    )(*ops)


def _wkv_lanes_kernel(r_ref, kp_ref, v_ref, kk_ref, a_ref, ld_ref, s_ref, o_ref, sout_ref, op_scr, o_scr, *,
                      seq_len):
    T = seq_len
    nb = r_ref.shape[0] // T
    tok = lambda ref, t: ref[pl.ds(t, nb, stride=T), :]

    for t in range(T):
        kk = tok(kk_ref, t)
        op_scr[0, t] = (-kk).T
        op_scr[1, t] = jnp.exp(tok(ld_ref, t)).T
        op_scr[2, t] = (kk * tok(a_ref, t)).T
        op_scr[3, t] = tok(kp_ref, t).T
        op_scr[4, t] = tok(r_ref, t).T
        op_scr[5, t] = tok(v_ref, t).T

    def all_keys(x):
        y = jnp.sum(x.reshape(HEAD // CARRY, CARRY, nb), axis=0)
        for i in range(1, int(np.log2(CARRY)) + 1):
            y = y + pltpu.roll(y, CARRY >> i, 0)
        return y

    def over_keys(y, x):
        return (y[None] * x.reshape(HEAD // CARRY, CARRY, nb)).reshape(HEAD, nb)

    for hh in range(2):
        keys = slice(hh * HEAD, (hh + 1) * HEAD)

        def value_row(vi, carry, hh=hh, keys=keys):
            s = s_ref[hh, vi]
            row = hh * HEAD + vi
            for t in range(T):
                sa = all_keys(s * op_scr[0, t, keys, :])
                v_row = jnp.broadcast_to(op_scr[5, t, pl.ds(row, 1), :], (CARRY, nb))
                s = (s * op_scr[1, t, keys, :] + over_keys(sa, op_scr[2, t, keys, :])
                     + over_keys(v_row, op_scr[3, t, keys, :]))
                o_scr[t, pl.ds(row, 1), :] = all_keys(s * op_scr[4, t, keys, :])[0:1]
            sout_ref[hh, vi] = s
            return carry

        jax.lax.fori_loop(0, HEAD, value_row, 0, unroll=4)

    for t in range(T):
        o_ref[pl.ds(t, nb, stride=T), :] = o_scr[t].T


def _wkv_lanes_call(ops, n_seq, seq_len, state_t, layer):
    assert n_seq == LANES
    rows = n_seq * seq_len
    blk = pl.BlockSpec((rows, PAIR), lambda p: (0, p))
    in_state = pl.BlockSpec((None, 2, HEAD, HEAD, n_seq), lambda p: (layer, p, 0, 0, 0))
    out_state = pl.BlockSpec((2, HEAD, HEAD, n_seq), lambda p: (p, 0, 0, 0))
    return pl.pallas_call(
        functools.partial(_wkv_lanes_kernel, seq_len=seq_len),
        grid=(N_PAIRS,), in_specs=[blk] * 6 + [in_state], out_specs=[blk, out_state],
        out_shape=[jax.ShapeDtypeStruct((rows, D_RWKV), F32),
                   jax.ShapeDtypeStruct(state_t.shape[1:], F32)],
        scratch_shapes=[pltpu.VMEM((6, seq_len, PAIR, n_seq), F32), pltpu.VMEM((seq_len, PAIR, n_seq), F32)],
        compiler_params=pltpu.CompilerParams(dimension_semantics=("arbitrary",),
                                             vmem_limit_bytes=VMEM_LIMIT),
    )(*ops, state_t)


def _merge_kernel(*refs, tm, tiles_per_seq, banded):
    it = iter(refs)
    o_ref, bonus_ref, g_ref, u_ref, hist_ref, gates_ref, x_ref = (next(it) for _ in range(7))
    if banded:
        band_ref = next(it)
    (invc_ref, lnw_ref, lnb_ref, ones_ref, pmap_ref, pscale_ref, wba_ref, wbb_ref, wout_ref, gpost_ref,
     y_ref) = (next(it) for _ in range(11))
    if not banded:
        s_a, s_b = next(it), next(it)
    d_model = x_ref.shape[1]
    inv_h = 1.0 / HEAD

    o = o_ref[...]
    mean = _head_sums(o, ones_ref[...]) * inv_h
    cen = o - mean
    var = _head_sums(cen * cen, ones_ref[...]) * inv_h
    o_n = cen * jax.lax.rsqrt(var + GN_EPS) * lnw_ref[...] + lnb_ref[...]
    o_rwkv = (o_n + bonus_ref[...]) * g_ref[...]

    u = u_ref[...]
    lane = jax.lax.broadcasted_iota(jnp.int32, (1, D_POOL), 1)
    win_sum = jnp.zeros((tm, D_POOL), F32)
    if banded:
        full = jnp.concatenate([hist_ref[...], u], axis=0)
        for gi in range(len(POOL_WINDOWS)):
            s = _dot_exact_lhs(band_ref[gi], full, 2)
            in_group = (lane >= gi * POOL_GROUP) & (lane < (gi + 1) * POOL_GROUP)
            win_sum = jnp.where(in_group, s, win_sum)
    else:
        keep = jnp.where((pl.program_id(0) % tiles_per_seq) == 0, 0.0, 1.0)
        s_a[0:POOL_HIST, :] = hist_ref[...] * keep
        s_a[POOL_HIST:POOL_HIST + tm, :] = u
        src, dst = s_a, s_b
        total = POOL_HIST + tm
        w = 1
        for gi, win in enumerate(POOL_WINDOWS):
            while w < win:
                lo = CARRY * int(np.log2(2 * w))
                dst[lo:total, :] = src[lo:total, :] + src[lo - w:total - w, :]
                src, dst = dst, src
                w *= 2
            in_group = (lane >= gi * POOL_GROUP) & (lane < (gi + 1) * POOL_GROUP)
            win_sum = jnp.where(in_group, src[POOL_HIST:total, :], win_sum)
    diff = win_sum * invc_ref[...] - u
    o_pool = _dot(diff, pmap_ref[...]) * pscale_ref[...]

    gates = gates_ref[...]
    merged = (gates[:, 0:d_model] * _dot(o_rwkv, wba_ref[...])
              + gates[:, d_model:2 * d_model] * _dot(o_pool, wbb_ref[...]))
    y_ref[...] = x_ref[...] + _rms(_dot(merged, wout_ref[...]), gpost_ref[...])


def _merge_call(o, bonus, g, u, hist, hist_rows, gates, x2d, seq_len, tm, band, invc, lw):
    n, d_model = x2d.shape
    nt = n // tm
    tiles_per_seq = max(seq_len // tm, 1)
    banded = band is not None
    row = lambda w: pl.BlockSpec((tm, w), lambda i: (i, 0))
    if banded:
        hist_spec = pl.BlockSpec((hist_rows, D_POOL), lambda i: (i, 0))
        invc_spec = _full(invc.shape)
        scratch = []
    else:
        per = tm // hist_rows
        hist_spec = pl.BlockSpec((hist_rows, D_POOL), lambda i: (jnp.maximum(i * per - 1, 0), 0))
        invc_spec = pl.BlockSpec((tm, D_POOL), lambda i: (i % tiles_per_seq, 0))
        scratch = [pltpu.VMEM((hist_rows + tm, D_POOL), F32)] * 2
    consts = [lw['ln_x_w'], lw['ln_x_b'], lw['ones'], lw['pool_map'], lw['pool_scale'],
              lw['w_branch_rwkv'], lw['w_branch_pool'], lw['w_out'], lw['g_mix_post']]
    ins = [o, bonus, g, u, hist, gates, x2d] + ([band] if banded else []) + [invc] + consts
    specs = ([row(D_RWKV)] * 3 + [row(D_POOL), hist_spec, row(2 * d_model), row(d_model)]
             + ([_full(band.shape)] if banded else []) + [invc_spec] + [_const_spec(a) for a in consts])
    return pl.pallas_call(
        functools.partial(_merge_kernel, tm=tm, tiles_per_seq=tiles_per_seq, banded=banded),
        grid=(nt,), in_specs=specs, out_specs=row(d_model),
        out_shape=jax.ShapeDtypeStruct((n, d_model), F32), scratch_shapes=scratch,
        compiler_params=pltpu.CompilerParams(dimension_semantics=("arbitrary",),
                                             vmem_limit_bytes=VMEM_LIMIT),
    )(*[_const_arg(a) for a in ins])


def _ffn_kernel(x_ref, gpre_ref, win_ref, wout_ref, gpost_ref, y_ref):
    x = x_ref[...]
    d_ff = wout_ref.shape[0]
    h = _rms(x, gpre_ref[...]).astype(BF16)
    gu = jnp.dot(h, win_ref[...], preferred_element_type=F32)
    gt = gu[:, 0:d_ff]
    act = gt * _sigmoid(gt) * gu[:, d_ff:2 * d_ff]
    y_ref[...] = x + _rms(_dot(act, wout_ref[...]), gpost_ref[...])


def _ffn_call(x2d, tm, lw):
    n, d_model = x2d.shape
    row = pl.BlockSpec((tm, d_model), lambda i: (i, 0))
    consts = [lw['g_ffn_pre'], lw['w_ffn_in'], lw['w_ffn_out'], lw['g_ffn_post']]
    return pl.pallas_call(
        _ffn_kernel, grid=(n // tm,), in_specs=[row] + [_const_spec(a) for a in consts],
        out_specs=row, out_shape=jax.ShapeDtypeStruct((n, d_model), F32),
        compiler_params=pltpu.CompilerParams(dimension_semantics=("arbitrary",),
                                             vmem_limit_bytes=VMEM_LIMIT),
    )(x2d, *[_const_arg(a) for a in consts])


def _prompt_inv_count(seq_len):
    pos = np.arange(seq_len)[:, None]
    win = np.repeat(np.array(POOL_WINDOWS), POOL_GROUP)[None, :]
    return jnp.asarray(1.0 / np.minimum(pos + 1, win).astype(np.float32), F32)


def _sample_pool_consts(tm, seq_len):
    ns = tm // seq_len
    rs, rt = np.divmod(np.arange(tm), seq_len)
    hs, hj = np.divmod(np.arange(ns * POOL_SLOTS), POOL_SLOTS)
    cs = np.concatenate([hs, rs])[None, :]
    cpos = np.concatenate([hj - 1, POOL_BUF + rt])[None, :]
    dist = (POOL_BUF + rt)[:, None] - cpos
    same = (rs[:, None] == cs) & (cpos >= 0)
    band = np.stack([same & (dist >= 0) & (dist < w) for w in POOL_WINDOWS]).astype(np.float32)
    win = np.repeat(np.array(POOL_WINDOWS), POOL_GROUP)[None, :]
    pos = (SAMPLE_START_POS + rt)[:, None]
    invc = 1.0 / np.minimum(pos + 1, win).astype(np.float32)
    return jnp.asarray(band, BF16), jnp.asarray(invc, F32)


def _block_diag(blocks):
    n = len(blocks)
    rows = []
    for i, blk in enumerate(blocks):
        rows.append(jnp.concatenate(
            [blk if j == i else jnp.zeros((blk.shape[0], blocks[j].shape[1]), blk.dtype) for j in range(n)],
            axis=1))
    return jnp.concatenate(rows, axis=0)


BIG_WEIGHTS = ('w_in', 'w_gate_up', 'w_branch_rwkv', 'w_branch_pool', 'w_out', 'w_ffn_in', 'w_ffn_out')


def _layer_weights(l, w, wb):
    row = lambda a: a.reshape(1, -1).astype(F32)
    lw = {k: (wb[k], l) for k in BIG_WEIGHTS}
    lw.update({
        'g_mix_pre': row(w['norm_mix_pre'][l]), 'g_mix_post': row(w['norm_mix_post'][l]),
        'g_ffn_pre': row(w['norm_ffn_pre'][l]), 'g_ffn_post': row(w['norm_ffn_post'][l]),
        'mu': row(w['mu_shift'][l]), 'decay_bias': row(w['decay_bias'][l]), 'iclr_bias': row(w['iclr_bias'][l]),
        'lora': _block_diag([w['w_decay_up'][l], w['w_iclr_up'][l]]).astype(BF16),
        'k_k': row(w['k_k'][l]), 'k_a': row(w['k_a'][l]), 'r_k': row(w['r_k'][l]),
        'ln_x_w': row(w['ln_x_w'][l]), 'ln_x_b': row(w['ln_x_b'][l]),
        'pool_map': _block_diag([w['pool_map'][l, gi] for gi in range(len(POOL_WINDOWS))]).astype(BF16),
        'pool_scale': row(w['pool_scale'][l]),
        'ones': jnp.asarray(np.kron(np.eye(SUM_TILE // HEAD), np.ones((HEAD, HEAD))), BF16),
    })
    if l > 0:
        pad = LANES - LORA_VRES
        lw['vres_bias'] = row(w['vres_bias'][l - 1])
        lw['w_vres_down'] = jnp.pad(w['w_vres_down'][l - 1], ((0, 0), (0, pad))).astype(BF16)
        lw['w_vres_up'] = jnp.pad(w['w_vres_up'][l - 1], ((0, pad), (0, 0))).astype(BF16)
    return lw


def _from_pairs(h):
    lead = h.shape[:-3]
    n = len(lead)
    h = h.reshape(*lead, N_PAIRS, HEAD, 2, HEAD)
    return h.transpose(*range(n), n, n + 2, n + 3, n + 1).reshape(*lead, N_HEADS, HEAD, HEAD)


def _tile_rows(n_seq, seq_len, target):
    if seq_len >= target:
        return target
    return min(n_seq * seq_len, target)


def kernel(x_prompt, x_sample, state_wkv, state_shift, state_pool, norm_mix_pre, norm_mix_post, norm_ffn_pre, norm_ffn_post, w_in, mu_shift, decay_bias, w_decay_up, iclr_bias, w_iclr_up, w_gate_up, k_k, k_a, r_k, ln_x_w, ln_x_b, vres_bias, w_vres_down, w_vres_up, pool_map, pool_scale, w_branch_rwkv, w_branch_pool, w_out, w_ffn_in, w_ffn_out):
    weights = dict(norm_mix_pre=norm_mix_pre, norm_mix_post=norm_mix_post, norm_ffn_pre=norm_ffn_pre,
                   norm_ffn_post=norm_ffn_post, w_in=w_in, mu_shift=mu_shift, decay_bias=decay_bias,
                   w_decay_up=w_decay_up, iclr_bias=iclr_bias, w_iclr_up=w_iclr_up, w_gate_up=w_gate_up,
                   k_k=k_k, k_a=k_a, r_k=r_k, ln_x_w=ln_x_w, ln_x_b=ln_x_b, vres_bias=vres_bias,
                   w_vres_down=w_vres_down, w_vres_up=w_vres_up, pool_map=pool_map, pool_scale=pool_scale,
                   w_branch_rwkv=w_branch_rwkv, w_branch_pool=w_branch_pool, w_out=w_out,
                   w_ffn_in=w_ffn_in, w_ffn_out=w_ffn_out)
    depth = w_in.shape[0]
    wb = {k: weights[k].astype(BF16) for k in BIG_WEIGHTS}
    bp, tp, d_model = x_prompt.shape
    bs, ts, _ = x_sample.shape
    tm_p = _tile_rows(bp, tp, 256)
    tm_s = _tile_rows(bs, ts, 256)
    invc_p = _prompt_inv_count(tp)
    band_s, invc_s = _sample_pool_consts(tm_s, ts)

    state_t = state_wkv.transpose(0, 2, 3, 4, 1)
    yp = x_prompt.reshape(bp * tp, d_model)
    ys = x_sample.reshape(bs * ts, d_model)
    vf_p = vf_s = None
    outs = {k: [] for k in ('wkv_p', 'shift_p', 'pool_p', 'wkv_s', 'shift_s', 'pool_s')}
    for l in range(depth):
        lw = _layer_weights(l, weights, wb)

        (r, kp, v, kkn, a, ld, g, bonus, u, gates, last) = _proj_call(yp, tp, tm_p, lw, vf_p, None)
        if l == 0:
            vf_p = v
        o, hout = _wkv_call((r, kp, v, kkn, a, ld), bp, tp)
        x1 = _merge_call(o, bonus, g, u, u, POOL_HIST, gates, yp, tp, 2 * tm_p, None, invc_p, lw)
        yp = _ffn_call(x1, 2 * tm_p, lw)
        outs['wkv_p'].append(_from_pairs(hout))
        tiles = tp // tm_p
        outs['shift_p'].append(last.reshape(bp, tiles, CARRY, D_SHIFT)[:, -1, -1])
        outs['pool_p'].append(u.reshape(bp, tp, D_POOL)[:, -POOL_BUF:])

        first = jnp.zeros((bs, ts, D_SHIFT), F32).at[:, 0].set(state_shift[l]).reshape(bs * ts, D_SHIFT)
        (r, kp, v, kkn, a, ld, g, bonus, u, gates, ps_all) = _proj_call(ys, ts, tm_s, lw, vf_s, first)
        if l == 0:
            vf_s = v
        o, hout = _wkv_lanes_call((r, kp, v, kkn, a, ld), bs, ts, state_t, l)
        outs['wkv_s'].append(hout)
        hist = jnp.pad(state_pool[l], ((0, 0), (1, 0), (0, 0))).reshape(bs * POOL_SLOTS, D_POOL)
        x1 = _merge_call(o, bonus, g, u, hist, (tm_s // ts) * POOL_SLOTS, gates, ys, ts, tm_s, band_s, invc_s,
                         lw)
        ys = _ffn_call(x1, tm_s, lw)
        outs['shift_s'].append(ps_all.reshape(bs, ts, D_SHIFT)[:, -1])
        outs['pool_s'].append(jnp.concatenate(
            [state_pool[l], u.reshape(bs, ts, D_POOL)], axis=1)[:, -POOL_BUF:])

    return (yp.reshape(bp, tp, d_model), ys.reshape(bs, ts, d_model),
            jnp.stack(outs['wkv_p']), jnp.stack(outs['shift_p']), jnp.stack(outs['pool_p']),
            jnp.stack(outs['wkv_s']).transpose(0, 4, 1, 2, 3), jnp.stack(outs['shift_s']), jnp.stack(outs['pool_s']))
```

```python
import functools

import numpy as np
import jax
import jax.numpy as jnp
from jax.experimental import pallas as pl
from jax.experimental.pallas import tpu as pltpu

F32 = jnp.float32
BF16 = jnp.bfloat16

HEAD = 64
D_RWKV = 768
N_HEADS = D_RWKV // HEAD
PAIR = 2 * HEAD
N_PAIRS = D_RWKV // PAIR
D_POOL = 256
POOL_WINDOWS = (2, 4, 8, 16)
POOL_GROUP = D_POOL // len(POOL_WINDOWS)
POOL_BUF = max(POOL_WINDOWS) - 1
POOL_SLOTS = POOL_BUF + 1
LORA_DECAY = 64
LORA_ICLR = 64
LORA_GATE = 128
LORA_VRES = 32
D_SHIFT = 3 * D_RWKV + LORA_DECAY + LORA_ICLR + LORA_GATE
SAMPLE_START_POS = 16384
RMS_EPS = 1e-6
GN_EPS = 1e-5 * HEAD
DECAY_SCALE = float(np.exp(-0.5))
CHUNK = 64
WKV_SUB = 4
POOL_HIST = 32
CARRY = 8
LANES = 128
SUM_TILE = 256
VMEM_LIMIT = 56 * 1024 * 1024


def _dot(a, b):
    return jnp.dot(a.astype(BF16), b.astype(BF16), preferred_element_type=F32)


def _dot_nt(a, b):
    return jax.lax.dot_general(a.astype(BF16), b.astype(BF16), (((1,), (1,)), ((), ())),
                               preferred_element_type=F32)


def _split(x, parts):
    out = []
    for _ in range(parts - 1):
        hi = x.astype(BF16)
        out.append(hi)
        x = x - hi.astype(F32)
    out.append(x.astype(BF16))
    return out


def _dot_exact_lhs(sel, x, parts):
    acc = None
    for p in _split(x, parts):
        t = jnp.dot(sel, p, preferred_element_type=F32)
        acc = t if acc is None else acc + t
    return acc


def _head_sums(x, ones):
    return jnp.concatenate(
        [_dot(x[:, c:c + SUM_TILE], ones) for c in range(0, x.shape[1], SUM_TILE)], axis=1)


def _sigmoid(x):
    return 1.0 / (1.0 + jnp.exp(-x))


def _rms(x, g):
    return x * jax.lax.rsqrt(jnp.mean(x * x, axis=-1, keepdims=True) + RMS_EPS) * g


def _proj_kernel(*refs, tm, seq_len, has_vres, has_first):
    it = iter(refs)
    x_ref, gpre_ref, win_ref, mu_ref, lora_ref, dbias_ref, ibias_ref, wgate_ref = (next(it) for _ in range(8))
    kk_ref, ka_ref, rk_ref, ones_ref = (next(it) for _ in range(4))
    if has_vres:
        vfirst_ref, vbias_ref, wvdown_ref, wvup_ref = (next(it) for _ in range(4))
    if has_first:
        first_ref = next(it)
    (r_out, kp_out, v_out, kkn_out, a_out, ld_out, g_out, bonus_out, u_out, gates_out,
     last_out, p_scr) = (next(it) for _ in range(12))

    gates_end = D_SHIFT + D_POOL + 2 * x_ref.shape[1]
    i = pl.program_id(0)
    if seq_len >= tm:
        tiles_per_seq = seq_len // tm
        new_seq = (i % tiles_per_seq) == 0
    else:
        new_seq = i == 0

    @pl.when(new_seq)
    def _():
        p_scr[0:CARRY, :] = jnp.zeros((CARRY, D_SHIFT), F32)

    xn = _rms(x_ref[...], gpre_ref[...]).astype(BF16)
    p_scr[CARRY:CARRY + tm, :] = jnp.dot(xn, win_ref[:, 0:D_SHIFT], preferred_element_type=F32)
    u_out[...] = jnp.dot(xn, win_ref[:, D_SHIFT:D_SHIFT + D_POOL], preferred_element_type=F32)
    gates_out[...] = _sigmoid(jnp.dot(xn, win_ref[:, D_SHIFT + D_POOL:gates_end],
                                      preferred_element_type=F32)).astype(BF16)

    if has_first:
        row = jax.lax.broadcasted_iota(jnp.int32, (tm, 1), 0)
        seq_start = jax.lax.rem(row, seq_len) == 0

    def mixed(c0, c1):
        ps = p_scr[CARRY:CARRY + tm, c0:c1]
        prev = p_scr[CARRY - 1:CARRY - 1 + tm, c0:c1]
        if has_first:
            prev = jnp.where(seq_start, first_ref[:, c0:c1], prev)
        return ps + (prev - ps) * mu_ref[:, c0:c1]

    o1, o2, o3 = D_RWKV, 2 * D_RWKV, 3 * D_RWKV
    o5 = o3 + LORA_DECAY + LORA_ICLR
    r = mixed(0, o1)
    k = mixed(o1, o2)
    v = mixed(o2, o3)
    xwa = mixed(o3, o5)
    xg = mixed(o5, D_SHIFT)

    lane = jax.lax.broadcasted_iota(jnp.int32, (1, LANES), 1)
    lora_in = jnp.where(lane < LORA_DECAY, jnp.tanh(xwa), xwa)
    lora = _dot(lora_in, lora_ref[...])
    ld_out[...] = -DECAY_SCALE * _sigmoid(dbias_ref[...] + lora[:, 0:D_RWKV])
    a = _sigmoid(ibias_ref[...] + lora[:, D_RWKV:2 * D_RWKV])
    a_out[...] = a
    g_out[...] = _dot(_sigmoid(xg), wgate_ref[...]).astype(BF16)

    if has_vres:
        vdown = jnp.dot(xn, wvdown_ref[...], preferred_element_type=F32)
        v = v + (vfirst_ref[...] - v) * _sigmoid(vbias_ref[...] + _dot(vdown, wvup_ref[...]))
    v_out[...] = v

    kk = k * kk_ref[...]
    norm = jnp.sqrt(_head_sums(kk * kk, ones_ref[...]))
    kkn_out[...] = kk / jnp.maximum(norm, 1e-12)
    kp = k * (1.0 + (a - 1.0) * ka_ref[...])
    kp_out[...] = kp
    r_out[...] = r
    bonus_out[...] = (_head_sums(r * kp * rk_ref[...], ones_ref[...]) * v).astype(BF16)

    if has_first:
        last_out[...] = p_scr[CARRY:CARRY + tm, :]
    else:
        tail = p_scr[tm:tm + CARRY, :]
        last_out[...] = tail
        p_scr[0:CARRY, :] = tail


def _full(shape):
    nd = len(shape)
    return pl.BlockSpec(shape, lambda *_: (0,) * nd, pipeline_mode=pl.Buffered(1))


def _const_spec(a):
    if isinstance(a, tuple):
        arr, l = a
        return pl.BlockSpec((None,) + arr.shape[1:], lambda *_: (l,) + (0,) * (arr.ndim - 1),
                            pipeline_mode=pl.Buffered(1))
    return _full(a.shape)


def _const_arg(a):
    return a[0] if isinstance(a, tuple) else a


def _proj_call(x2d, seq_len, tm, lw, vfirst, first):
    n = x2d.shape[0]
    d_model = x2d.shape[1]
    has_vres = vfirst is not None
    has_first = first is not None
    nt = n // tm
    row = lambda w: pl.BlockSpec((tm, w), lambda i: (i, 0))
    ins = [x2d, lw['g_mix_pre'], lw['w_in'], lw['mu'], lw['lora'], lw['decay_bias'], lw['iclr_bias'],
           lw['w_gate_up'], lw['k_k'], lw['k_a'], lw['r_k'], lw['ones']]
    specs = [row(d_model)] + [_const_spec(a) for a in ins[1:]]
    if has_vres:
        ins += [vfirst, lw['vres_bias'], lw['w_vres_down'], lw['w_vres_up']]
        specs += [row(D_RWKV)] + [_const_spec(lw[k]) for k in ('vres_bias', 'w_vres_down', 'w_vres_up')]
    if has_first:
        ins.append(first)
        specs.append(row(D_SHIFT))
    sds = lambda w, dt=F32: jax.ShapeDtypeStruct((n, w), dt)
    last_rows = tm if has_first else CARRY
    out_shape = [sds(D_RWKV)] * 6 + [sds(D_RWKV, BF16)] * 2 + [sds(D_POOL), sds(2 * d_model, BF16),
                                     jax.ShapeDtypeStruct((nt * last_rows, D_SHIFT), F32)]
    out_specs = [row(D_RWKV)] * 8 + [row(D_POOL), row(2 * d_model),
                                     pl.BlockSpec((last_rows, D_SHIFT), lambda i: (i, 0))]
    return pl.pallas_call(
        functools.partial(_proj_kernel, tm=tm, seq_len=seq_len, has_vres=has_vres, has_first=has_first),
        grid=(nt,), in_specs=specs, out_specs=out_specs, out_shape=out_shape,
        scratch_shapes=[pltpu.VMEM((tm + CARRY, D_SHIFT), F32)],
        compiler_params=pltpu.CompilerParams(dimension_semantics=("arbitrary",),
                                             vmem_limit_bytes=VMEM_LIMIT),
    )(*[_const_arg(a) for a in ins])


def _wkv_kernel(r_ref, kp_ref, v_ref, kk_ref, a_ref, ld_ref, o_ref, hout_ref, h_scr, *, n_sub):
    c_idx = pl.program_id(1)
    n_steps = pl.num_programs(1)
    C = CHUNK

    lane = jax.lax.broadcasted_iota(jnp.int32, (1, PAIR), 1)
    left = lane < HEAD
    ri = jax.lax.broadcasted_iota(jnp.int32, (PAIR, PAIR), 0)
    ci = jax.lax.broadcasted_iota(jnp.int32, (PAIR, PAIR), 1)
    same = (ri >= HEAD) == (ci >= HEAD)
    strict = same & (ci < ri)
    incl = same & (ci <= ri)
    eye = ri == ci
    eye_f = jnp.where(eye, 1.0, 0.0).astype(F32)

    def stack(x):
        return jnp.concatenate([jnp.where(left, x, 0.0), jnp.where(left, 0.0, x)], axis=0)

    def twice(x):
        return jnp.concatenate([x, x], axis=0)

    @pl.when(c_idx == 0)
    def _():
        h_scr[...] = jnp.zeros(h_scr.shape, F32)

    tr = jax.lax.broadcasted_iota(jnp.int32, (C, C), 0)
    tc = jax.lax.broadcasted_iota(jnp.int32, (C, C), 1)
    tri = jnp.where(tr >= tc, 1.0, 0.0).astype(BF16)

    a_t, r_t, b_t, k_t, b_h, k_h, v_st, w_last = ([] for _ in range(8))
    for j in range(n_sub):
        rows = slice(j * C, (j + 1) * C)
        ld = ld_ref[rows, :]
        cum = _dot_exact_lhs(tri, ld, 3)
        clast = cum[C - 1:C, :]
        e_cum = jnp.exp(cum)
        e_prev = jnp.exp(cum - ld)
        e_neg = jnp.exp(-cum)
        e_rem = jnp.exp(clast - cum)
        wl = jnp.exp(clast)
        kk = kk_ref[rows, :]
        b = kk * a_ref[rows, :]
        kp = kp_ref[rows, :]
        full = (-(kk * e_prev), r_ref[rows, :] * e_cum, b * e_neg, kp * e_neg, b * e_rem, kp * e_rem)
        v = v_ref[rows, :]
        for p in range(N_PAIRS):
            sl = slice(p * PAIR, (p + 1) * PAIR)
            for dst, src in zip((a_t, r_t, b_t, k_t, b_h, k_h), full):
                dst.append(src[:, sl])
            v_st.append(stack(v[:, sl]))
            w_last.append(wl[:, sl])

    units = range(n_sub * N_PAIRS)
    sc = [_dot_nt(jnp.concatenate([a_t[q], r_t[q]], axis=0),
                  jnp.concatenate([stack(b_t[q]), stack(k_t[q])], axis=0)) for q in units]
    n_ab = [jnp.where(strict, twice(s[0:C, 0:PAIR]), 0.0) for s in sc]
    n_ak = [jnp.where(strict, twice(s[0:C, PAIR:2 * PAIR]), 0.0) for s in sc]
    n_rb = [jnp.where(incl, twice(s[C:2 * C, 0:PAIR]), 0.0) for s in sc]
    n_rk = [jnp.where(incl, twice(s[C:2 * C, PAIR:2 * PAIR]), 0.0) for s in sc]

    inv = [eye_f + n for n in n_ab]
    power = n_ab
    y = [_dot(n_ak[q], v_st[q]) for q in units]
    for _ in range(int(np.log2(C)) - 1):
        power = [_dot(m, m) for m in power]
        inv = [inv[q] + _dot(inv[q], power[q]) for q in units]

    au = [_dot(inv[q], jnp.concatenate([stack(a_t[q]), y[q]], axis=1)) for q in units]
    zero = jnp.zeros((PAIR, PAIR), F32)
    big = []
    for q in units:
        rhs = jnp.concatenate([au[q], jnp.concatenate([zero, v_st[q]], axis=1)], axis=0)
        lhs = jnp.concatenate(
            [jnp.concatenate([n_rb[q], n_rk[q]], axis=1),
             jnp.concatenate([stack(b_h[q]).T, stack(k_h[q]).T], axis=1)], axis=0)
        big.append(_dot(lhs, rhs))

    h = [h_scr[p] for p in range(N_PAIRS)]
    for j in range(n_sub):
        rows = slice(j * C, (j + 1) * C)
        qs = [j * N_PAIRS + p for p in range(N_PAIRS)]
        sd = []
        for p, q in enumerate(qs):
            r_hat = stack(r_t[q]) + big[q][0:PAIR, 0:PAIR]
            m_low = big[q][PAIR:2 * PAIR, 0:PAIR]
            sd.append(_dot(jnp.concatenate([r_hat, m_low], axis=0), h[p]))
        for p, q in enumerate(qs):
            o_st = sd[p][0:PAIR] + big[q][0:PAIR, PAIR:2 * PAIR]
            o_ref[rows, p * PAIR:(p + 1) * PAIR] = o_st[0:C] + o_st[C:2 * C]
            w_col = jnp.sum(jnp.where(eye, jnp.broadcast_to(w_last[q], (PAIR, PAIR)), 0.0),
                            axis=1, keepdims=True)
            h[p] = w_col * h[p] + sd[p][PAIR:2 * PAIR] + big[q][PAIR:2 * PAIR, PAIR:2 * PAIR]
    for p in range(N_PAIRS):
        h_scr[p] = h[p]

    @pl.when(c_idx == n_steps - 1)
    def _():
        for p in range(N_PAIRS):
            hout_ref[0, p] = h_scr[p, 0:HEAD, :] + h_scr[p, HEAD:PAIR, :]


def _wkv_call(ops, n_seq, seq_len):
    n_sub = WKV_SUB if seq_len % (WKV_SUB * CHUNK) == 0 else 1
    rows = n_sub * CHUNK
    assert seq_len % rows == 0
    ns = seq_len // rows
    blk = pl.BlockSpec((rows, D_RWKV), lambda b, c: (b * ns + c, 0))
    hspec = pl.BlockSpec((1, N_PAIRS, HEAD, PAIR), lambda b, c: (b, 0, 0, 0))
    return pl.pallas_call(
        functools.partial(_wkv_kernel, n_sub=n_sub),
        grid=(n_seq, ns), in_specs=[blk] * 6, out_specs=[blk, hspec],
        out_shape=[jax.ShapeDtypeStruct((n_seq * seq_len, D_RWKV), F32),
                   jax.ShapeDtypeStruct((n_seq, N_PAIRS, HEAD, PAIR), F32)],
        scratch_shapes=[pltpu.VMEM((N_PAIRS, PAIR, PAIR), F32)],
        compiler_params=pltpu.CompilerParams(dimension_semantics=("arbitrary", "arbitrary"),
                                             vmem_limit_bytes=VMEM_LIMIT),
    )(*ops)


def _wkv_lanes_kernel(r_ref, kp_ref, v_ref, kk_ref, a_ref, ld_ref, s_ref, o_ref, sout_ref, op_scr, o_scr, *,
                      seq_len):
    T = seq_len
    nb = r_ref.shape[0] // T
    tok = lambda ref, t: ref[pl.ds(t, nb, stride=T), :]

    for t in range(T):
        kk = tok(kk_ref, t)
        op_scr[0, t] = (-kk).T
        op_scr[1, t] = jnp.exp(tok(ld_ref, t)).T
        op_scr[2, t] = (kk * tok(a_ref, t)).T
        op_scr[3, t] = tok(kp_ref, t).T
        op_scr[4, t] = tok(r_ref, t).T
        op_scr[5, t] = tok(v_ref, t).T

    def all_keys(x):
        y = jnp.sum(x.reshape(HEAD // CARRY, CARRY, nb), axis=0)
        for i in range(1, int(np.log2(CARRY)) + 1):
            y = y + pltpu.roll(y, CARRY >> i, 0)
        return y

    def over_keys(y, x):
        return (y[None] * x.reshape(HEAD // CARRY, CARRY, nb)).reshape(HEAD, nb)

    for hh in range(2):
        keys = slice(hh * HEAD, (hh + 1) * HEAD)

        def value_row(vi, carry, hh=hh, keys=keys):
            s = s_ref[hh, vi]
            row = hh * HEAD + vi
            for t in range(T):
                sa = all_keys(s * op_scr[0, t, keys, :])
                v_row = jnp.broadcast_to(op_scr[5, t, pl.ds(row, 1), :], (CARRY, nb))
                s = (s * op_scr[1, t, keys, :] + over_keys(sa, op_scr[2, t, keys, :])
                     + over_keys(v_row, op_scr[3, t, keys, :]))
                o_scr[t, pl.ds(row, 1), :] = all_keys(s * op_scr[4, t, keys, :])[0:1]
            sout_ref[hh, vi] = s
            return carry

        jax.lax.fori_loop(0, HEAD, value_row, 0, unroll=4)

    for t in range(T):
        o_ref[pl.ds(t, nb, stride=T), :] = o_scr[t].T


def _wkv_lanes_call(ops, n_seq, seq_len, state_t, layer):
    assert n_seq == LANES
    rows = n_seq * seq_len
    blk = pl.BlockSpec((rows, PAIR), lambda p: (0, p))
    in_state = pl.BlockSpec((None, 2, HEAD, HEAD, n_seq), lambda p: (layer, p, 0, 0, 0))
    out_state = pl.BlockSpec((2, HEAD, HEAD, n_seq), lambda p: (p, 0, 0, 0))
    return pl.pallas_call(
        functools.partial(_wkv_lanes_kernel, seq_len=seq_len),
        grid=(N_PAIRS,), in_specs=[blk] * 6 + [in_state], out_specs=[blk, out_state],
        out_shape=[jax.ShapeDtypeStruct((rows, D_RWKV), F32),
                   jax.ShapeDtypeStruct(state_t.shape[1:], F32)],
        scratch_shapes=[pltpu.VMEM((6, seq_len, PAIR, n_seq), F32), pltpu.VMEM((seq_len, PAIR, n_seq), F32)],
        compiler_params=pltpu.CompilerParams(dimension_semantics=("arbitrary",),
                                             vmem_limit_bytes=VMEM_LIMIT),
    )(*ops, state_t)


def _merge_kernel(*refs, tm, tiles_per_seq, banded):
    it = iter(refs)
    o_ref, bonus_ref, g_ref, u_ref, hist_ref, gates_ref, x_ref = (next(it) for _ in range(7))
    if banded:
        band_ref = next(it)
    (invc_ref, lnw_ref, lnb_ref, ones_ref, pmap_ref, pscale_ref, wba_ref, wbb_ref, wout_ref, gpost_ref,
     y_ref) = (next(it) for _ in range(11))
    if not banded:
        s_a, s_b = next(it), next(it)
    d_model = x_ref.shape[1]
    inv_h = 1.0 / HEAD

    o = o_ref[...]
    mean = _head_sums(o, ones_ref[...]) * inv_h
    cen = o - mean
    var = _head_sums(cen * cen, ones_ref[...]) * inv_h
    o_n = cen * jax.lax.rsqrt(var + GN_EPS) * lnw_ref[...] + lnb_ref[...]
    o_rwkv = (o_n + bonus_ref[...].astype(F32)) * g_ref[...].astype(F32)

    u = u_ref[...]
    lane = jax.lax.broadcasted_iota(jnp.int32, (1, D_POOL), 1)
    win_sum = jnp.zeros((tm, D_POOL), F32)
    if banded:
        full = jnp.concatenate([hist_ref[...], u], axis=0)
        for gi in range(len(POOL_WINDOWS)):
            s = _dot_exact_lhs(band_ref[gi], full, 2)
            in_group = (lane >= gi * POOL_GROUP) & (lane < (gi + 1) * POOL_GROUP)
            win_sum = jnp.where(in_group, s, win_sum)
    else:
        keep = jnp.where((pl.program_id(0) % tiles_per_seq) == 0, 0.0, 1.0)
        s_a[0:POOL_HIST, :] = hist_ref[...] * keep
        s_a[POOL_HIST:POOL_HIST + tm, :] = u
        src, dst = s_a, s_b
        total = POOL_HIST + tm
        w = 1
        for gi, win in enumerate(POOL_WINDOWS):
            while w < win:
                lo = CARRY * int(np.log2(2 * w))
                dst[lo:total, :] = src[lo:total, :] + src[lo - w:total - w, :]
                src, dst = dst, src
                w *= 2
            in_group = (lane >= gi * POOL_GROUP) & (lane < (gi + 1) * POOL_GROUP)
            win_sum = jnp.where(in_group, src[POOL_HIST:total, :], win_sum)
    diff = win_sum * invc_ref[...] - u
    o_pool = _dot(diff, pmap_ref[...]) * pscale_ref[...]

    gates = gates_ref[...].astype(F32)
    merged = (gates[:, 0:d_model] * _dot(o_rwkv, wba_ref[...])
              + gates[:, d_model:2 * d_model] * _dot(o_pool, wbb_ref[...]))
    y_ref[...] = x_ref[...] + _rms(_dot(merged, wout_ref[...]), gpost_ref[...])


def _merge_call(o, bonus, g, u, hist, hist_rows, gates, x2d, seq_len, tm, band, invc, lw):
    n, d_model = x2d.shape
    nt = n // tm
    tiles_per_seq = max(seq_len // tm, 1)
    banded = band is not None
    row = lambda w: pl.BlockSpec((tm, w), lambda i: (i, 0))
    if banded:
        hist_spec = pl.BlockSpec((hist_rows, D_POOL), lambda i: (i, 0))
        invc_spec = _full(invc.shape)
        scratch = []
    else:
        per = tm // hist_rows
        hist_spec = pl.BlockSpec((hist_rows, D_POOL), lambda i: (jnp.maximum(i * per - 1, 0), 0))
        invc_spec = pl.BlockSpec((tm, D_POOL), lambda i: (i % tiles_per_seq, 0))
        scratch = [pltpu.VMEM((hist_rows + tm, D_POOL), F32)] * 2
    consts = [lw['ln_x_w'], lw['ln_x_b'], lw['ones'], lw['pool_map'], lw['pool_scale'],
              lw['w_branch_rwkv'], lw['w_branch_pool'], lw['w_out'], lw['g_mix_post']]
    ins = [o, bonus, g, u, hist, gates, x2d] + ([band] if banded else []) + [invc] + consts
    specs = ([row(D_RWKV)] * 3 + [row(D_POOL), hist_spec, row(2 * d_model), row(d_model)]
             + ([_full(band.shape)] if banded else []) + [invc_spec] + [_const_spec(a) for a in consts])
    return pl.pallas_call(
        functools.partial(_merge_kernel, tm=tm, tiles_per_seq=tiles_per_seq, banded=banded),
        grid=(nt,), in_specs=specs, out_specs=row(d_model),
        out_shape=jax.ShapeDtypeStruct((n, d_model), F32), scratch_shapes=scratch,
        compiler_params=pltpu.CompilerParams(dimension_semantics=("arbitrary",),
                                             vmem_limit_bytes=VMEM_LIMIT),
    )(*[_const_arg(a) for a in ins])


def _ffn_kernel(x_ref, gpre_ref, win_ref, wout_ref, gpost_ref, y_ref):
    x = x_ref[...]
    d_ff = wout_ref.shape[0]
    h = _rms(x, gpre_ref[...]).astype(BF16)
    gu = jnp.dot(h, win_ref[...], preferred_element_type=F32)
    gt = gu[:, 0:d_ff]
    act = gt * _sigmoid(gt) * gu[:, d_ff:2 * d_ff]
    y_ref[...] = x + _rms(_dot(act, wout_ref[...]), gpost_ref[...])


def _ffn_call(x2d, tm, lw):
    n, d_model = x2d.shape
    row = pl.BlockSpec((tm, d_model), lambda i: (i, 0))
    consts = [lw['g_ffn_pre'], lw['w_ffn_in'], lw['w_ffn_out'], lw['g_ffn_post']]
    return pl.pallas_call(
        _ffn_kernel, grid=(n // tm,), in_specs=[row] + [_const_spec(a) for a in consts],
        out_specs=row, out_shape=jax.ShapeDtypeStruct((n, d_model), F32),
        compiler_params=pltpu.CompilerParams(dimension_semantics=("arbitrary",),
                                             vmem_limit_bytes=VMEM_LIMIT),
    )(x2d, *[_const_arg(a) for a in consts])


def _prompt_inv_count(seq_len):
    pos = np.arange(seq_len)[:, None]
    win = np.repeat(np.array(POOL_WINDOWS), POOL_GROUP)[None, :]
    return jnp.asarray(1.0 / np.minimum(pos + 1, win).astype(np.float32), F32)


def _sample_pool_consts(tm, seq_len):
    ns = tm // seq_len
    rs, rt = np.divmod(np.arange(tm), seq_len)
    hs, hj = np.divmod(np.arange(ns * POOL_SLOTS), POOL_SLOTS)
    cs = np.concatenate([hs, rs])[None, :]
    cpos = np.concatenate([hj - 1, POOL_BUF + rt])[None, :]
    dist = (POOL_BUF + rt)[:, None] - cpos
    same = (rs[:, None] == cs) & (cpos >= 0)
    band = np.stack([same & (dist >= 0) & (dist < w) for w in POOL_WINDOWS]).astype(np.float32)
    win = np.repeat(np.array(POOL_WINDOWS), POOL_GROUP)[None, :]
    pos = (SAMPLE_START_POS + rt)[:, None]
    invc = 1.0 / np.minimum(pos + 1, win).astype(np.float32)
    return jnp.asarray(band, BF16), jnp.asarray(invc, F32)


def _block_diag(blocks):
    n = len(blocks)
    rows = []
    for i, blk in enumerate(blocks):
        rows.append(jnp.concatenate(
            [blk if j == i else jnp.zeros((blk.shape[0], blocks[j].shape[1]), blk.dtype) for j in range(n)],
            axis=1))
    return jnp.concatenate(rows, axis=0)


BIG_WEIGHTS = ('w_in', 'w_gate_up', 'w_branch_rwkv', 'w_branch_pool', 'w_out', 'w_ffn_in', 'w_ffn_out')


def _layer_weights(l, w, wb):
    row = lambda a: a.reshape(1, -1).astype(F32)
    lw = {k: (wb[k], l) for k in BIG_WEIGHTS}
    lw.update({
        'g_mix_pre': row(w['norm_mix_pre'][l]), 'g_mix_post': row(w['norm_mix_post'][l]),
        'g_ffn_pre': row(w['norm_ffn_pre'][l]), 'g_ffn_post': row(w['norm_ffn_post'][l]),
        'mu': row(w['mu_shift'][l]), 'decay_bias': row(w['decay_bias'][l]), 'iclr_bias': row(w['iclr_bias'][l]),
        'lora': _block_diag([w['w_decay_up'][l], w['w_iclr_up'][l]]).astype(BF16),
        'k_k': row(w['k_k'][l]), 'k_a': row(w['k_a'][l]), 'r_k': row(w['r_k'][l]),
        'ln_x_w': row(w['ln_x_w'][l]), 'ln_x_b': row(w['ln_x_b'][l]),
        'pool_map': _block_diag([w['pool_map'][l, gi] for gi in range(len(POOL_WINDOWS))]).astype(BF16),
        'pool_scale': row(w['pool_scale'][l]),
        'ones': jnp.asarray(np.kron(np.eye(SUM_TILE // HEAD), np.ones((HEAD, HEAD))), BF16),
    })
    if l > 0:
        pad = LANES - LORA_VRES
        lw['vres_bias'] = row(w['vres_bias'][l - 1])
        lw['w_vres_down'] = jnp.pad(w['w_vres_down'][l - 1], ((0, 0), (0, pad))).astype(BF16)
        lw['w_vres_up'] = jnp.pad(w['w_vres_up'][l - 1], ((0, pad), (0, 0))).astype(BF16)
    return lw


def _from_pairs(h):
    lead = h.shape[:-3]
    n = len(lead)
    h = h.reshape(*lead, N_PAIRS, HEAD, 2, HEAD)
    return h.transpose(*range(n), n, n + 2, n + 3, n + 1).reshape(*lead, N_HEADS, HEAD, HEAD)


def _tile_rows(n_seq, seq_len, target):
    if seq_len >= target:
        return target
    return min(n_seq * seq_len, target)


def kernel(x_prompt, x_sample, state_wkv, state_shift, state_pool, norm_mix_pre, norm_mix_post, norm_ffn_pre, norm_ffn_post, w_in, mu_shift, decay_bias, w_decay_up, iclr_bias, w_iclr_up, w_gate_up, k_k, k_a, r_k, ln_x_w, ln_x_b, vres_bias, w_vres_down, w_vres_up, pool_map, pool_scale, w_branch_rwkv, w_branch_pool, w_out, w_ffn_in, w_ffn_out):
    weights = dict(norm_mix_pre=norm_mix_pre, norm_mix_post=norm_mix_post, norm_ffn_pre=norm_ffn_pre,
                   norm_ffn_post=norm_ffn_post, w_in=w_in, mu_shift=mu_shift, decay_bias=decay_bias,
                   w_decay_up=w_decay_up, iclr_bias=iclr_bias, w_iclr_up=w_iclr_up, w_gate_up=w_gate_up,
                   k_k=k_k, k_a=k_a, r_k=r_k, ln_x_w=ln_x_w, ln_x_b=ln_x_b, vres_bias=vres_bias,
                   w_vres_down=w_vres_down, w_vres_up=w_vres_up, pool_map=pool_map, pool_scale=pool_scale,
                   w_branch_rwkv=w_branch_rwkv, w_branch_pool=w_branch_pool, w_out=w_out,
                   w_ffn_in=w_ffn_in, w_ffn_out=w_ffn_out)
    depth = w_in.shape[0]
    wb = {k: weights[k].astype(BF16) for k in BIG_WEIGHTS}
    bp, tp, d_model = x_prompt.shape
    bs, ts, _ = x_sample.shape
    tm_p = _tile_rows(bp, tp, 256)
    tm_s = _tile_rows(bs, ts, 256)
    invc_p = _prompt_inv_count(tp)
    band_s, invc_s = _sample_pool_consts(tm_s, ts)

    state_t = state_wkv.transpose(0, 2, 3, 4, 1)
    yp = x_prompt.reshape(bp * tp, d_model)
    ys = x_sample.reshape(bs * ts, d_model)
    vf_p = vf_s = None
    outs = {k: [] for k in ('wkv_p', 'shift_p', 'pool_p', 'wkv_s', 'shift_s', 'pool_s')}
    for l in range(depth):
        lw = _layer_weights(l, weights, wb)

        (r, kp, v, kkn, a, ld, g, bonus, u, gates, last) = _proj_call(yp, tp, tm_p, lw, vf_p, None)
        if l == 0:
            vf_p = v
        o, hout = _wkv_call((r, kp, v, kkn, a, ld), bp, tp)
        x1 = _merge_call(o, bonus, g, u, u, POOL_HIST, gates, yp, tp, 2 * tm_p, None, invc_p, lw)
        yp = _ffn_call(x1, 2 * tm_p, lw)
        outs['wkv_p'].append(_from_pairs(hout))
        tiles = tp // tm_p
        outs['shift_p'].append(last.reshape(bp, tiles, CARRY, D_SHIFT)[:, -1, -1])
        outs['pool_p'].append(u.reshape(bp, tp, D_POOL)[:, -POOL_BUF:])

        first = jnp.zeros((bs, ts, D_SHIFT), F32).at[:, 0].set(state_shift[l]).reshape(bs * ts, D_SHIFT)
        (r, kp, v, kkn, a, ld, g, bonus, u, gates, ps_all) = _proj_call(ys, ts, tm_s, lw, vf_s, first)
        if l == 0:
            vf_s = v
        o, hout = _wkv_lanes_call((r, kp, v, kkn, a, ld), bs, ts, state_t, l)
        outs['wkv_s'].append(hout)
        hist = jnp.pad(state_pool[l], ((0, 0), (1, 0), (0, 0))).reshape(bs * POOL_SLOTS, D_POOL)
        x1 = _merge_call(o, bonus, g, u, hist, (tm_s // ts) * POOL_SLOTS, gates, ys, ts, tm_s, band_s, invc_s,
                         lw)
        ys = _ffn_call(x1, tm_s, lw)
        outs['shift_s'].append(ps_all.reshape(bs, ts, D_SHIFT)[:, -1])
        outs['pool_s'].append(jnp.concatenate(
            [state_pool[l], u.reshape(bs, ts, D_POOL)], axis=1)[:, -POOL_BUF:])

    return (yp.reshape(bp, tp, d_model), ys.reshape(bs, ts, d_model),
            jnp.stack(outs['wkv_p']), jnp.stack(outs['shift_p']), jnp.stack(outs['pool_p']),
            jnp.stack(outs['wkv_s']).transpose(0, 4, 1, 2, 3), jnp.stack(outs['shift_s']), jnp.stack(outs['pool_s']))
```

```python
import functools

import numpy as np
import jax
import jax.numpy as jnp
from jax.experimental import pallas as pl
from jax.experimental.pallas import tpu as pltpu

F32 = jnp.float32
BF16 = jnp.bfloat16

HEAD = 64
D_RWKV = 768
N_HEADS = D_RWKV // HEAD
PAIR = 2 * HEAD
N_PAIRS = D_RWKV // PAIR
D_POOL = 256
POOL_WINDOWS = (2, 4, 8, 16)
POOL_GROUP = D_POOL // len(POOL_WINDOWS)
POOL_BUF = max(POOL_WINDOWS) - 1
POOL_SLOTS = POOL_BUF + 1
LORA_DECAY = 64
LORA_ICLR = 64
LORA_GATE = 128
LORA_VRES = 32
D_SHIFT = 3 * D_RWKV + LORA_DECAY + LORA_ICLR + LORA_GATE
SAMPLE_START_POS = 16384
RMS_EPS = 1e-6
GN_EPS = 1e-5 * HEAD
DECAY_SCALE = float(np.exp(-0.5))
CHUNK = 64
WKV_SUB = 4
POOL_HIST = 32
CARRY = 8
LANES = 128
LANES_UNROLL = 8
SUM_TILE = 256
ROW_TILE = 256
VMEM_LIMIT = 56 * 1024 * 1024


def _dot(a, b):
    return jnp.dot(a.astype(BF16), b.astype(BF16), preferred_element_type=F32)


def _dot_nt(a, b):
    return jax.lax.dot_general(a.astype(BF16), b.astype(BF16), (((1,), (1,)), ((), ())),
                               preferred_element_type=F32)


def _split(x, parts):
    out = []
    for _ in range(parts - 1):
        hi = x.astype(BF16)
        out.append(hi)
        x = x - hi.astype(F32)
    out.append(x.astype(BF16))
    return out


def _dot_exact_lhs(sel, x, parts):
    acc = None
    for p in _split(x, parts):
        t = jnp.dot(sel, p, preferred_element_type=F32)
        acc = t if acc is None else acc + t
    return acc


def _head_sums(x, ones):
    return jnp.concatenate(
        [_dot(x[:, c:c + SUM_TILE], ones) for c in range(0, x.shape[1], SUM_TILE)], axis=1)


def _sigmoid(x):
    return 1.0 / (1.0 + jnp.exp(-x))


def _rms(x, g):
    return x * jax.lax.rsqrt(jnp.mean(x * x, axis=-1, keepdims=True) + RMS_EPS) * g


def _proj_kernel(*refs, tm, seq_len, has_vres, has_first):
    it = iter(refs)
    x_ref, gpre_ref, win_ref, mu_ref, lora_ref, dbias_ref, ibias_ref, wgate_ref = (next(it) for _ in range(8))
    kk_ref, ka_ref, rk_ref, ones_ref = (next(it) for _ in range(4))
    if has_vres:
        vfirst_ref, vbias_ref, wvdown_ref, wvup_ref = (next(it) for _ in range(4))
    if has_first:
        first_ref = next(it)
    (r_out, kp_out, v_out, kkn_out, a_out, ld_out, g_out, bonus_out, u_out, gates_out,
     last_out, p_scr) = (next(it) for _ in range(12))

    gates_end = D_SHIFT + D_POOL + 2 * x_ref.shape[1]
    i = pl.program_id(0)
    if seq_len >= tm:
        tiles_per_seq = seq_len // tm
        new_seq = (i % tiles_per_seq) == 0
    else:
        new_seq = i == 0

    @pl.when(new_seq)
    def _():
        p_scr[0:CARRY, :] = jnp.zeros((CARRY, D_SHIFT), F32)

    xn = _rms(x_ref[...], gpre_ref[...]).astype(BF16)
    p_scr[CARRY:CARRY + tm, :] = jnp.dot(xn, win_ref[:, 0:D_SHIFT], preferred_element_type=F32)
    u_out[...] = jnp.dot(xn, win_ref[:, D_SHIFT:D_SHIFT + D_POOL], preferred_element_type=F32)
    gates_out[...] = _sigmoid(jnp.dot(xn, win_ref[:, D_SHIFT + D_POOL:gates_end],
                                      preferred_element_type=F32)).astype(BF16)

    if has_first:
        row = jax.lax.broadcasted_iota(jnp.int32, (tm, 1), 0)
        seq_start = jax.lax.rem(row, seq_len) == 0

    def mixed(c0, c1):
        ps = p_scr[CARRY:CARRY + tm, c0:c1]
        prev = p_scr[CARRY - 1:CARRY - 1 + tm, c0:c1]
        if has_first:
            prev = jnp.where(seq_start, first_ref[:, c0:c1], prev)
        return ps + (prev - ps) * mu_ref[:, c0:c1]

    o1, o2, o3 = D_RWKV, 2 * D_RWKV, 3 * D_RWKV
    o5 = o3 + LORA_DECAY + LORA_ICLR
    r = mixed(0, o1)
    k = mixed(o1, o2)
    v = mixed(o2, o3)
    xwa = mixed(o3, o5)
    xg = mixed(o5, D_SHIFT)

    lane = jax.lax.broadcasted_iota(jnp.int32, (1, LANES), 1)
    lora_in = jnp.where(lane < LORA_DECAY, jnp.tanh(xwa), xwa)
    lora = _dot(lora_in, lora_ref[...])
    ld_out[...] = -DECAY_SCALE * _sigmoid(dbias_ref[...] + lora[:, 0:D_RWKV])
    a = _sigmoid(ibias_ref[...] + lora[:, D_RWKV:2 * D_RWKV])
    a_out[...] = a
    g_out[...] = _dot(_sigmoid(xg), wgate_ref[...]).astype(BF16)

    if has_vres:
        vdown = jnp.dot(xn, wvdown_ref[...], preferred_element_type=F32)
        v = v + (vfirst_ref[...] - v) * _sigmoid(vbias_ref[...] + _dot(vdown, wvup_ref[...]))
    v_out[...] = v

    kk = k * kk_ref[...]
    norm = jnp.sqrt(_head_sums(kk * kk, ones_ref[...]))
    kkn_out[...] = kk / jnp.maximum(norm, 1e-12)
    kp = k * (1.0 + (a - 1.0) * ka_ref[...])
    kp_out[...] = kp
    r_out[...] = r
    bonus_out[...] = (_head_sums(r * kp * rk_ref[...], ones_ref[...]) * v).astype(BF16)

    if has_first:
        last_out[...] = p_scr[CARRY:CARRY + tm, :]
    else:
        tail = p_scr[tm:tm + CARRY, :]
        last_out[...] = tail
        p_scr[0:CARRY, :] = tail


def _full(shape):
    nd = len(shape)
    return pl.BlockSpec(shape, lambda *_: (0,) * nd, pipeline_mode=pl.Buffered(1))


def _const_spec(a):
    if isinstance(a, tuple):
        arr, l = a
        return pl.BlockSpec((None,) + arr.shape[1:], lambda *_: (l,) + (0,) * (arr.ndim - 1),
                            pipeline_mode=pl.Buffered(1))
    return _full(a.shape)


def _const_arg(a):
    return a[0] if isinstance(a, tuple) else a


def _proj_call(x2d, seq_len, tm, lw, vfirst, first):
    n = x2d.shape[0]
    d_model = x2d.shape[1]
    has_vres = vfirst is not None
    has_first = first is not None
    nt = n // tm
    row = lambda w: pl.BlockSpec((tm, w), lambda i: (i, 0))
    ins = [x2d, lw['g_mix_pre'], lw['w_in'], lw['mu'], lw['lora'], lw['decay_bias'], lw['iclr_bias'],
           lw['w_gate_up'], lw['k_k'], lw['k_a'], lw['r_k'], lw['ones']]
    specs = [row(d_model)] + [_const_spec(a) for a in ins[1:]]
    if has_vres:
        ins += [vfirst, lw['vres_bias'], lw['w_vres_down'], lw['w_vres_up']]
        specs += [row(D_RWKV)] + [_const_spec(lw[k]) for k in ('vres_bias', 'w_vres_down', 'w_vres_up')]
    if has_first:
        ins.append(first)
        specs.append(row(D_SHIFT))
    sds = lambda w, dt=F32: jax.ShapeDtypeStruct((n, w), dt)
    last_rows = tm if has_first else CARRY
    out_shape = [sds(D_RWKV)] * 6 + [sds(D_RWKV, BF16)] * 2 + [sds(D_POOL), sds(2 * d_model, BF16),
                                     jax.ShapeDtypeStruct((nt * last_rows, D_SHIFT), F32)]
    out_specs = [row(D_RWKV)] * 8 + [row(D_POOL), row(2 * d_model),
                                     pl.BlockSpec((last_rows, D_SHIFT), lambda i: (i, 0))]
    return pl.pallas_call(
        functools.partial(_proj_kernel, tm=tm, seq_len=seq_len, has_vres=has_vres, has_first=has_first),
        grid=(nt,), in_specs=specs, out_specs=out_specs, out_shape=out_shape,
        scratch_shapes=[pltpu.VMEM((tm + CARRY, D_SHIFT), F32)],
        compiler_params=pltpu.CompilerParams(dimension_semantics=("arbitrary",),
                                             vmem_limit_bytes=VMEM_LIMIT),
    )(*[_const_arg(a) for a in ins])


def _wkv_kernel(r_ref, kp_ref, v_ref, kk_ref, a_ref, ld_ref, o_ref, hout_ref, h_scr, *, n_sub):
    c_idx = pl.program_id(1)
    n_steps = pl.num_programs(1)
    C = CHUNK

    lane = jax.lax.broadcasted_iota(jnp.int32, (1, PAIR), 1)
    left = lane < HEAD
    ri = jax.lax.broadcasted_iota(jnp.int32, (PAIR, PAIR), 0)
    ci = jax.lax.broadcasted_iota(jnp.int32, (PAIR, PAIR), 1)
    same = (ri >= HEAD) == (ci >= HEAD)
    strict = same & (ci < ri)
    incl = same & (ci <= ri)
    eye = ri == ci
    eye_f = jnp.where(eye, 1.0, 0.0).astype(F32)

    def stack(x):
        return jnp.concatenate([jnp.where(left, x, 0.0), jnp.where(left, 0.0, x)], axis=0)

    def twice(x):
        return jnp.concatenate([x, x], axis=0)

    @pl.when(c_idx == 0)
    def _():
        h_scr[...] = jnp.zeros(h_scr.shape, F32)

    tr = jax.lax.broadcasted_iota(jnp.int32, (C, C), 0)
    tc = jax.lax.broadcasted_iota(jnp.int32, (C, C), 1)
    tri = jnp.where(tr >= tc, 1.0, 0.0).astype(BF16)

    a_t, r_t, b_t, k_t, b_h, k_h, v_st, w_last = ([] for _ in range(8))
    for j in range(n_sub):
        rows = slice(j * C, (j + 1) * C)
        ld = ld_ref[rows, :]
        cum = _dot_exact_lhs(tri, ld, 3)
        clast = cum[C - 1:C, :]
        e_cum = jnp.exp(cum)
        e_prev = jnp.exp(cum - ld)
        e_neg = jnp.exp(-cum)
        e_rem = jnp.exp(clast - cum)
        wl = jnp.exp(clast)
        kk = kk_ref[rows, :]
        b = kk * a_ref[rows, :]
        kp = kp_ref[rows, :]
        full = (-(kk * e_prev), r_ref[rows, :] * e_cum, b * e_neg, kp * e_neg, b * e_rem, kp * e_rem)
        v = v_ref[rows, :]
        for p in range(N_PAIRS):
            sl = slice(p * PAIR, (p + 1) * PAIR)
            for dst, src in zip((a_t, r_t, b_t, k_t, b_h, k_h), full):
                dst.append(src[:, sl])
            v_st.append(stack(v[:, sl]))
            w_last.append(wl[:, sl])

    units = range(n_sub * N_PAIRS)
    sc = [_dot_nt(jnp.concatenate([a_t[q], r_t[q]], axis=0),
                  jnp.concatenate([stack(b_t[q]), stack(k_t[q])], axis=0)) for q in units]
    n_ab = [jnp.where(strict, twice(s[0:C, 0:PAIR]), 0.0) for s in sc]
    n_ak = [jnp.where(strict, twice(s[0:C, PAIR:2 * PAIR]), 0.0) for s in sc]
    n_rb = [jnp.where(incl, twice(s[C:2 * C, 0:PAIR]), 0.0) for s in sc]
    n_rk = [jnp.where(incl, twice(s[C:2 * C, PAIR:2 * PAIR]), 0.0) for s in sc]

    inv = [eye_f + n for n in n_ab]
    power = n_ab
    y = [_dot(n_ak[q], v_st[q]) for q in units]
    for _ in range(int(np.log2(C)) - 1):
        power = [_dot(m, m) for m in power]
        inv = [inv[q] + _dot(inv[q], power[q]) for q in units]

    au = [_dot(inv[q], jnp.concatenate([stack(a_t[q]), y[q]], axis=1)) for q in units]
    zero = jnp.zeros((PAIR, PAIR), F32)
    big = []
    for q in units:
        rhs = jnp.concatenate([au[q], jnp.concatenate([zero, v_st[q]], axis=1)], axis=0)
        lhs = jnp.concatenate(
            [jnp.concatenate([n_rb[q], n_rk[q]], axis=1),
             jnp.concatenate([stack(b_h[q]).T, stack(k_h[q]).T], axis=1)], axis=0)
        big.append(_dot(lhs, rhs))

    h = [h_scr[p] for p in range(N_PAIRS)]
    for j in range(n_sub):
        rows = slice(j * C, (j + 1) * C)
        qs = [j * N_PAIRS + p for p in range(N_PAIRS)]
        sd = []
        for p, q in enumerate(qs):
            r_hat = stack(r_t[q]) + big[q][0:PAIR, 0:PAIR]
            m_low = big[q][PAIR:2 * PAIR, 0:PAIR]
            sd.append(_dot(jnp.concatenate([r_hat, m_low], axis=0), h[p]))
        for p, q in enumerate(qs):
            o_st = sd[p][0:PAIR] + big[q][0:PAIR, PAIR:2 * PAIR]
            o_ref[rows, p * PAIR:(p + 1) * PAIR] = o_st[0:C] + o_st[C:2 * C]
            w_col = jnp.sum(jnp.where(eye, jnp.broadcast_to(w_last[q], (PAIR, PAIR)), 0.0),
                            axis=1, keepdims=True)
            h[p] = w_col * h[p] + sd[p][PAIR:2 * PAIR] + big[q][PAIR:2 * PAIR, PAIR:2 * PAIR]
    for p in range(N_PAIRS):
        h_scr[p] = h[p]

    @pl.when(c_idx == n_steps - 1)
    def _():
        for p in range(N_PAIRS):
            hout_ref[0, p] = h_scr[p, 0:HEAD, :] + h_scr[p, HEAD:PAIR, :]


def _wkv_call(ops, n_seq, seq_len):
    n_sub = WKV_SUB if seq_len % (WKV_SUB * CHUNK) == 0 else 1
    rows = n_sub * CHUNK
    assert seq_len % rows == 0
    ns = seq_len // rows
    blk = pl.BlockSpec((rows, D_RWKV), lambda b, c: (b * ns + c, 0))
    hspec = pl.BlockSpec((1, N_PAIRS, HEAD, PAIR), lambda b, c: (b, 0, 0, 0))
    return pl.pallas_call(
        functools.partial(_wkv_kernel, n_sub=n_sub),
        grid=(n_seq, ns), in_specs=[blk] * 6, out_specs=[blk, hspec],
        out_shape=[jax.ShapeDtypeStruct((n_seq * seq_len, D_RWKV), F32),
                   jax.ShapeDtypeStruct((n_seq, N_PAIRS, HEAD, PAIR), F32)],
        scratch_shapes=[pltpu.VMEM((N_PAIRS, PAIR, PAIR), F32)],
        compiler_params=pltpu.CompilerParams(dimension_semantics=("arbitrary", "arbitrary"),
                                             vmem_limit_bytes=VMEM_LIMIT),
    )(*ops)


def _wkv_lanes_kernel(r_ref, kp_ref, v_ref, kk_ref, a_ref, ld_ref, s_ref, o_ref, sout_ref, op_scr, o_scr, *,
                      seq_len):
    T = seq_len
    nb = r_ref.shape[0] // T
    tok = lambda ref, t: ref[pl.ds(t, nb, stride=T), :]

    for t in range(T):
        kk = tok(kk_ref, t)
        op_scr[0, t] = (-kk).T
        op_scr[1, t] = jnp.exp(tok(ld_ref, t)).T
        op_scr[2, t] = (kk * tok(a_ref, t)).T
        op_scr[3, t] = tok(kp_ref, t).T
        op_scr[4, t] = tok(r_ref, t).T
        op_scr[5, t] = tok(v_ref, t).T

    def all_keys(x):
        y = jnp.sum(x.reshape(HEAD // CARRY, CARRY, nb), axis=0)
        for i in range(1, int(np.log2(CARRY)) + 1):
            y = y + pltpu.roll(y, CARRY >> i, 0)
        return y

    def over_keys(y, x):
        return (y[None] * x.reshape(HEAD // CARRY, CARRY, nb)).reshape(HEAD, nb)

    for hh in range(2):
        keys = slice(hh * HEAD, (hh + 1) * HEAD)

        def value_row(vi, carry, hh=hh, keys=keys):
            s = s_ref[hh, vi]
            row = hh * HEAD + vi
            for t in range(T):
                sa = all_keys(s * op_scr[0, t, keys, :])
                v_row = jnp.broadcast_to(op_scr[5, t, pl.ds(row, 1), :], (CARRY, nb))
                s = (s * op_scr[1, t, keys, :] + over_keys(sa, op_scr[2, t, keys, :])
                     + over_keys(v_row, op_scr[3, t, keys, :]))
                o_scr[t, pl.ds(row, 1), :] = all_keys(s * op_scr[4, t, keys, :])[0:1]
            sout_ref[hh, vi] = s
            return carry

        jax.lax.fori_loop(0, HEAD, value_row, 0, unroll=LANES_UNROLL)

    for t in range(T):
        o_ref[pl.ds(t, nb, stride=T), :] = o_scr[t].T


def _wkv_lanes_call(ops, n_seq, seq_len, state_t, layer):
    assert n_seq == LANES
    rows = n_seq * seq_len
    blk = pl.BlockSpec((rows, PAIR), lambda p: (0, p))
    in_state = pl.BlockSpec((None, 2, HEAD, HEAD, n_seq), lambda p: (layer, p, 0, 0, 0))
    out_state = pl.BlockSpec((2, HEAD, HEAD, n_seq), lambda p: (p, 0, 0, 0))
    return pl.pallas_call(
        functools.partial(_wkv_lanes_kernel, seq_len=seq_len),
        grid=(N_PAIRS,), in_specs=[blk] * 6 + [in_state], out_specs=[blk, out_state],
        out_shape=[jax.ShapeDtypeStruct((rows, D_RWKV), F32),
                   jax.ShapeDtypeStruct(state_t.shape[1:], F32)],
        scratch_shapes=[pltpu.VMEM((6, seq_len, PAIR, n_seq), F32), pltpu.VMEM((seq_len, PAIR, n_seq), F32)],
        compiler_params=pltpu.CompilerParams(dimension_semantics=("arbitrary",),
                                             vmem_limit_bytes=VMEM_LIMIT),
    )(*ops, state_t)


def _merge_kernel(*refs, tm, tiles_per_seq, banded):
    it = iter(refs)
    o_ref, bonus_ref, g_ref, u_ref, hist_ref, gates_ref, x_ref = (next(it) for _ in range(7))
    if banded:
        band_ref = next(it)
    (invc_ref, lnw_ref, lnb_ref, ones_ref, pmap_ref, pscale_ref, wba_ref, wbb_ref, wout_ref, gpost_ref,
     y_ref) = (next(it) for _ in range(11))
    if not banded:
        s_a, s_b = next(it), next(it)
    d_model = x_ref.shape[1]
    inv_h = 1.0 / HEAD

    o = o_ref[...]
    mean = _head_sums(o, ones_ref[...]) * inv_h
    cen = o - mean
    var = _head_sums(cen * cen, ones_ref[...]) * inv_h
    o_n = cen * jax.lax.rsqrt(var + GN_EPS) * lnw_ref[...] + lnb_ref[...]
    o_rwkv = (o_n + bonus_ref[...].astype(F32)) * g_ref[...].astype(F32)

    u = u_ref[...]
    lane = jax.lax.broadcasted_iota(jnp.int32, (1, D_POOL), 1)
    win_sum = jnp.zeros((tm, D_POOL), F32)
    if banded:
        full = jnp.concatenate([hist_ref[...], u], axis=0)
        for gi in range(len(POOL_WINDOWS)):
            s = _dot_exact_lhs(band_ref[gi], full, 2)
            in_group = (lane >= gi * POOL_GROUP) & (lane < (gi + 1) * POOL_GROUP)
            win_sum = jnp.where(in_group, s, win_sum)
    else:
        keep = jnp.where((pl.program_id(0) % tiles_per_seq) == 0, 0.0, 1.0)
        s_a[0:POOL_HIST, :] = hist_ref[...] * keep
        s_a[POOL_HIST:POOL_HIST + tm, :] = u
        src, dst = s_a, s_b
        total = POOL_HIST + tm
        w = 1
        for gi, win in enumerate(POOL_WINDOWS):
            while w < win:
                lo = CARRY * int(np.log2(2 * w))
                dst[lo:total, :] = src[lo:total, :] + src[lo - w:total - w, :]
                src, dst = dst, src
                w *= 2
            in_group = (lane >= gi * POOL_GROUP) & (lane < (gi + 1) * POOL_GROUP)
            win_sum = jnp.where(in_group, src[POOL_HIST:total, :], win_sum)
    diff = win_sum * invc_ref[...] - u
    o_pool = _dot(diff, pmap_ref[...]) * pscale_ref[...]

    gates = gates_ref[...].astype(F32)
    merged = (gates[:, 0:d_model] * _dot(o_rwkv, wba_ref[...])
              + gates[:, d_model:2 * d_model] * _dot(o_pool, wbb_ref[...]))
    y_ref[...] = x_ref[...] + _rms(_dot(merged, wout_ref[...]), gpost_ref[...])


def _merge_call(o, bonus, g, u, hist, hist_rows, gates, x2d, seq_len, tm, band, invc, lw):
    n, d_model = x2d.shape
    nt = n // tm
    tiles_per_seq = max(seq_len // tm, 1)
    banded = band is not None
    row = lambda w: pl.BlockSpec((tm, w), lambda i: (i, 0))
    if banded:
        hist_spec = pl.BlockSpec((hist_rows, D_POOL), lambda i: (i, 0))
        invc_spec = _full(invc.shape)
        scratch = []
    else:
        per = tm // hist_rows
        hist_spec = pl.BlockSpec((hist_rows, D_POOL), lambda i: (jnp.maximum(i * per - 1, 0), 0))
        invc_spec = pl.BlockSpec((tm, D_POOL), lambda i: (i % tiles_per_seq, 0))
        scratch = [pltpu.VMEM((hist_rows + tm, D_POOL), F32)] * 2
    consts = [lw['ln_x_w'], lw['ln_x_b'], lw['ones'], lw['pool_map'], lw['pool_scale'],
              lw['w_branch_rwkv'], lw['w_branch_pool'], lw['w_out'], lw['g_mix_post']]
    ins = [o, bonus, g, u, hist, gates, x2d] + ([band] if banded else []) + [invc] + consts
    specs = ([row(D_RWKV)] * 3 + [row(D_POOL), hist_spec, row(2 * d_model), row(d_model)]
             + ([_full(band.shape)] if banded else []) + [invc_spec] + [_const_spec(a) for a in consts])
    return pl.pallas_call(
        functools.partial(_merge_kernel, tm=tm, tiles_per_seq=tiles_per_seq, banded=banded),
        grid=(nt,), in_specs=specs, out_specs=row(d_model),
        out_shape=jax.ShapeDtypeStruct((n, d_model), F32), scratch_shapes=scratch,
        compiler_params=pltpu.CompilerParams(dimension_semantics=("arbitrary",),
                                             vmem_limit_bytes=VMEM_LIMIT),
    )(*[_const_arg(a) for a in ins])


def _ffn_kernel(x_ref, gpre_ref, win_ref, wout_ref, gpost_ref, y_ref):
    x = x_ref[...]
    d_ff = wout_ref.shape[0]
    h = _rms(x, gpre_ref[...]).astype(BF16)
    gu = jnp.dot(h, win_ref[...], preferred_element_type=F32)
    gt = gu[:, 0:d_ff]
    act = gt * _sigmoid(gt) * gu[:, d_ff:2 * d_ff]
    y_ref[...] = x + _rms(_dot(act, wout_ref[...]), gpost_ref[...])


def _ffn_call(x2d, tm, lw):
    n, d_model = x2d.shape
    row = pl.BlockSpec((tm, d_model), lambda i: (i, 0))
    consts = [lw['g_ffn_pre'], lw['w_ffn_in'], lw['w_ffn_out'], lw['g_ffn_post']]
    return pl.pallas_call(
        _ffn_kernel, grid=(n // tm,), in_specs=[row] + [_const_spec(a) for a in consts],
        out_specs=row, out_shape=jax.ShapeDtypeStruct((n, d_model), F32),
        compiler_params=pltpu.CompilerParams(dimension_semantics=("arbitrary",),
                                             vmem_limit_bytes=VMEM_LIMIT),
    )(x2d, *[_const_arg(a) for a in consts])


def _prompt_inv_count(seq_len):
    pos = np.arange(seq_len)[:, None]
    win = np.repeat(np.array(POOL_WINDOWS), POOL_GROUP)[None, :]
    return jnp.asarray(1.0 / np.minimum(pos + 1, win).astype(np.float32), F32)


def _sample_pool_consts(tm, seq_len):
    ns = tm // seq_len
    rs, rt = np.divmod(np.arange(tm), seq_len)
    hs, hj = np.divmod(np.arange(ns * POOL_SLOTS), POOL_SLOTS)
    cs = np.concatenate([hs, rs])[None, :]
    cpos = np.concatenate([hj - 1, POOL_BUF + rt])[None, :]
    dist = (POOL_BUF + rt)[:, None] - cpos
    same = (rs[:, None] == cs) & (cpos >= 0)
    band = np.stack([same & (dist >= 0) & (dist < w) for w in POOL_WINDOWS]).astype(np.float32)
    win = np.repeat(np.array(POOL_WINDOWS), POOL_GROUP)[None, :]
    pos = (SAMPLE_START_POS + rt)[:, None]
    invc = 1.0 / np.minimum(pos + 1, win).astype(np.float32)
    return jnp.asarray(band, BF16), jnp.asarray(invc, F32)


def _block_diag(blocks):
    n = len(blocks)
    rows = []
    for i, blk in enumerate(blocks):
        rows.append(jnp.concatenate(
            [blk if j == i else jnp.zeros((blk.shape[0], blocks[j].shape[1]), blk.dtype) for j in range(n)],
            axis=1))
    return jnp.concatenate(rows, axis=0)


BIG_WEIGHTS = ('w_in', 'w_gate_up', 'w_branch_rwkv', 'w_branch_pool', 'w_out', 'w_ffn_in', 'w_ffn_out')


def _layer_weights(l, w, wb):
    row = lambda a: a.reshape(1, -1).astype(F32)
    lw = {k: (wb[k], l) for k in BIG_WEIGHTS}
    lw.update({
        'g_mix_pre': row(w['norm_mix_pre'][l]), 'g_mix_post': row(w['norm_mix_post'][l]),
        'g_ffn_pre': row(w['norm_ffn_pre'][l]), 'g_ffn_post': row(w['norm_ffn_post'][l]),
        'mu': row(w['mu_shift'][l]), 'decay_bias': row(w['decay_bias'][l]), 'iclr_bias': row(w['iclr_bias'][l]),
        'lora': _block_diag([w['w_decay_up'][l], w['w_iclr_up'][l]]).astype(BF16),
        'k_k': row(w['k_k'][l]), 'k_a': row(w['k_a'][l]), 'r_k': row(w['r_k'][l]),
        'ln_x_w': row(w['ln_x_w'][l]), 'ln_x_b': row(w['ln_x_b'][l]),
        'pool_map': _block_diag([w['pool_map'][l, gi] for gi in range(len(POOL_WINDOWS))]).astype(BF16),
        'pool_scale': row(w['pool_scale'][l]),
        'ones': jnp.asarray(np.kron(np.eye(SUM_TILE // HEAD), np.ones((HEAD, HEAD))), BF16),
    })
    if l > 0:
        pad = LANES - LORA_VRES
        lw['vres_bias'] = row(w['vres_bias'][l - 1])
        lw['w_vres_down'] = jnp.pad(w['w_vres_down'][l - 1], ((0, 0), (0, pad))).astype(BF16)
        lw['w_vres_up'] = jnp.pad(w['w_vres_up'][l - 1], ((0, pad), (0, 0))).astype(BF16)
    return lw


def _from_pairs(h):
    lead = h.shape[:-3]
    n = len(lead)
    h = h.reshape(*lead, N_PAIRS, HEAD, 2, HEAD)
    return h.transpose(*range(n), n, n + 2, n + 3, n + 1).reshape(*lead, N_HEADS, HEAD, HEAD)


def _tile_rows(n_seq, seq_len, target):
    if seq_len >= target:
        return target
    return min(n_seq * seq_len, target)


def kernel(x_prompt, x_sample, state_wkv, state_shift, state_pool, norm_mix_pre, norm_mix_post, norm_ffn_pre, norm_ffn_post, w_in, mu_shift, decay_bias, w_decay_up, iclr_bias, w_iclr_up, w_gate_up, k_k, k_a, r_k, ln_x_w, ln_x_b, vres_bias, w_vres_down, w_vres_up, pool_map, pool_scale, w_branch_rwkv, w_branch_pool, w_out, w_ffn_in, w_ffn_out):
    weights = dict(norm_mix_pre=norm_mix_pre, norm_mix_post=norm_mix_post, norm_ffn_pre=norm_ffn_pre,
                   norm_ffn_post=norm_ffn_post, w_in=w_in, mu_shift=mu_shift, decay_bias=decay_bias,
                   w_decay_up=w_decay_up, iclr_bias=iclr_bias, w_iclr_up=w_iclr_up, w_gate_up=w_gate_up,
                   k_k=k_k, k_a=k_a, r_k=r_k, ln_x_w=ln_x_w, ln_x_b=ln_x_b, vres_bias=vres_bias,
                   w_vres_down=w_vres_down, w_vres_up=w_vres_up, pool_map=pool_map, pool_scale=pool_scale,
                   w_branch_rwkv=w_branch_rwkv, w_branch_pool=w_branch_pool, w_out=w_out,
                   w_ffn_in=w_ffn_in, w_ffn_out=w_ffn_out)
    depth = w_in.shape[0]
    wb = {k: weights[k].astype(BF16) for k in BIG_WEIGHTS}
    bp, tp, d_model = x_prompt.shape
    bs, ts, _ = x_sample.shape
    tm_p = _tile_rows(bp, tp, ROW_TILE)
    tm_s = _tile_rows(bs, ts, ROW_TILE)
    invc_p = _prompt_inv_count(tp)
    band_s, invc_s = _sample_pool_consts(tm_s, ts)

    state_t = state_wkv.transpose(0, 2, 3, 4, 1)
    yp = x_prompt.reshape(bp * tp, d_model)
    ys = x_sample.reshape(bs * ts, d_model)
    vf_p = vf_s = None
    outs = {k: [] for k in ('wkv_p', 'shift_p', 'pool_p', 'wkv_s', 'shift_s', 'pool_s')}
    for l in range(depth):
        lw = _layer_weights(l, weights, wb)

        (r, kp, v, kkn, a, ld, g, bonus, u, gates, last) = _proj_call(yp, tp, tm_p, lw, vf_p, None)
        if l == 0:
            vf_p = v
        o, hout = _wkv_call((r, kp, v, kkn, a, ld), bp, tp)
        x1 = _merge_call(o, bonus, g, u, u, POOL_HIST, gates, yp, tp, 2 * tm_p, None, invc_p, lw)
        yp = _ffn_call(x1, 2 * tm_p, lw)
        outs['wkv_p'].append(_from_pairs(hout))
        tiles = tp // tm_p
        outs['shift_p'].append(last.reshape(bp, tiles, CARRY, D_SHIFT)[:, -1, -1])
        outs['pool_p'].append(u.reshape(bp, tp, D_POOL)[:, -POOL_BUF:])

        first = jnp.zeros((bs, ts, D_SHIFT), F32).at[:, 0].set(state_shift[l]).reshape(bs * ts, D_SHIFT)
        (r, kp, v, kkn, a, ld, g, bonus, u, gates, ps_all) = _proj_call(ys, ts, tm_s, lw, vf_s, first)
        if l == 0:
            vf_s = v
        o, hout = _wkv_lanes_call((r, kp, v, kkn, a, ld), bs, ts, state_t, l)
        outs['wkv_s'].append(hout)
        hist = jnp.pad(state_pool[l], ((0, 0), (1, 0), (0, 0))).reshape(bs * POOL_SLOTS, D_POOL)
        x1 = _merge_call(o, bonus, g, u, hist, (tm_s // ts) * POOL_SLOTS, gates, ys, ts, tm_s, band_s, invc_s,
                         lw)
        ys = _ffn_call(x1, tm_s, lw)
        outs['shift_s'].append(ps_all.reshape(bs, ts, D_SHIFT)[:, -1])
        outs['pool_s'].append(jnp.concatenate(
            [state_pool[l], u.reshape(bs, ts, D_POOL)], axis=1)[:, -POOL_BUF:])

    return (yp.reshape(bp, tp, d_model), ys.reshape(bs, ts, d_model),
            jnp.stack(outs['wkv_p']), jnp.stack(outs['shift_p']), jnp.stack(outs['pool_p']),
            jnp.stack(outs['wkv_s']).transpose(0, 4, 1, 2, 3), jnp.stack(outs['shift_s']), jnp.stack(outs['pool_s']))
```

```python
import functools

import numpy as np
import jax
import jax.numpy as jnp
from jax.experimental import pallas as pl
from jax.experimental.pallas import tpu as pltpu

F32 = jnp.float32
BF16 = jnp.bfloat16

HEAD = 64
D_RWKV = 768
N_HEADS = D_RWKV // HEAD
PAIR = 2 * HEAD
N_PAIRS = D_RWKV // PAIR
D_POOL = 256
POOL_WINDOWS = (2, 4, 8, 16)
POOL_GROUP = D_POOL // len(POOL_WINDOWS)
POOL_BUF = max(POOL_WINDOWS) - 1
POOL_SLOTS = POOL_BUF + 1
LORA_DECAY = 64
LORA_ICLR = 64
LORA_GATE = 128
LORA_VRES = 32
D_SHIFT = 3 * D_RWKV + LORA_DECAY + LORA_ICLR + LORA_GATE
SAMPLE_START_POS = 16384
RMS_EPS = 1e-6
GN_EPS = 1e-5 * HEAD
DECAY_SCALE = float(np.exp(-0.5))
CHUNK = 64
WKV_SUB = 4
POOL_HIST = 32
CARRY = 8
LANES = 128
LANES_UNROLL = 8
SUM_TILE = 256
ROW_TILE = 256
VMEM_LIMIT = 56 * 1024 * 1024


def _dot(a, b):
    return jnp.dot(a.astype(BF16), b.astype(BF16), preferred_element_type=F32)


def _dot_nt(a, b):
    return jax.lax.dot_general(a.astype(BF16), b.astype(BF16), (((1,), (1,)), ((), ())),
                               preferred_element_type=F32)


def _split(x, parts):
    out = []
    for _ in range(parts - 1):
        hi = x.astype(BF16)
        out.append(hi)
        x = x - hi.astype(F32)
    out.append(x.astype(BF16))
    return out


def _dot_exact_lhs(sel, x, parts):
    acc = None
    for p in _split(x, parts):
        t = jnp.dot(sel, p, preferred_element_type=F32)
        acc = t if acc is None else acc + t
    return acc


def _head_sums(x, ones):
    return jnp.concatenate(
        [_dot(x[:, c:c + SUM_TILE], ones) for c in range(0, x.shape[1], SUM_TILE)], axis=1)


def _prefix_rows(x):
    rows, cols = x.shape
    tiles = rows // CARRY
    y = x.reshape(tiles, CARRY, cols)
    sub = jax.lax.broadcasted_iota(jnp.int32, (1, CARRY, 1), 1)
    d = 1
    while d < CARRY:
        y = y + jnp.where(sub >= d, pltpu.roll(y, d, 1), 0.0)
        d *= 2
    total = y[:, CARRY - 1:CARRY, :]
    run = jnp.zeros((1, 1, cols), F32)
    before = []
    for i in range(tiles):
        before.append(run)
        run = run + total[i:i + 1]
    return (y + jnp.concatenate(before, axis=0)).reshape(rows, cols)


def _sigmoid(x):
    return 1.0 / (1.0 + jnp.exp(-x))


def _rms(x, g):
    return x * jax.lax.rsqrt(jnp.mean(x * x, axis=-1, keepdims=True) + RMS_EPS) * g


def _proj_kernel(*refs, tm, seq_len, has_vres, has_first):
    it = iter(refs)
    x_ref, gpre_ref, win_ref, mu_ref, lora_ref, dbias_ref, ibias_ref, wgate_ref = (next(it) for _ in range(8))
    kk_ref, ka_ref, rk_ref, ones_ref = (next(it) for _ in range(4))
    if has_vres:
        vfirst_ref, vbias_ref, wvdown_ref, wvup_ref = (next(it) for _ in range(4))
    if has_first:
        first_ref = next(it)
    (r_out, kp_out, v_out, kkn_out, a_out, ld_out, g_out, bonus_out, u_out, gates_out,
     last_out, p_scr) = (next(it) for _ in range(12))

    gates_end = D_SHIFT + D_POOL + 2 * x_ref.shape[1]
    i = pl.program_id(0)
    if seq_len >= tm:
        tiles_per_seq = seq_len // tm
        new_seq = (i % tiles_per_seq) == 0
    else:
        new_seq = i == 0

    @pl.when(new_seq)
    def _():
        p_scr[0:CARRY, :] = jnp.zeros((CARRY, D_SHIFT), F32)

    xn = _rms(x_ref[...], gpre_ref[...]).astype(BF16)
    p_scr[CARRY:CARRY + tm, :] = jnp.dot(xn, win_ref[:, 0:D_SHIFT], preferred_element_type=F32)
    u_out[...] = jnp.dot(xn, win_ref[:, D_SHIFT:D_SHIFT + D_POOL], preferred_element_type=F32)
    gates_out[...] = _sigmoid(jnp.dot(xn, win_ref[:, D_SHIFT + D_POOL:gates_end],
                                      preferred_element_type=F32)).astype(BF16)

    if has_first:
        row = jax.lax.broadcasted_iota(jnp.int32, (tm, 1), 0)
        seq_start = jax.lax.rem(row, seq_len) == 0

    def mixed(c0, c1):
        ps = p_scr[CARRY:CARRY + tm, c0:c1]
        prev = p_scr[CARRY - 1:CARRY - 1 + tm, c0:c1]
        if has_first:
            prev = jnp.where(seq_start, first_ref[:, c0:c1], prev)
        return ps + (prev - ps) * mu_ref[:, c0:c1]

    o1, o2, o3 = D_RWKV, 2 * D_RWKV, 3 * D_RWKV
    o5 = o3 + LORA_DECAY + LORA_ICLR
    r = mixed(0, o1)
    k = mixed(o1, o2)
    v = mixed(o2, o3)
    xwa = mixed(o3, o5)
    xg = mixed(o5, D_SHIFT)

    lane = jax.lax.broadcasted_iota(jnp.int32, (1, LANES), 1)
    lora_in = jnp.where(lane < LORA_DECAY, jnp.tanh(xwa), xwa)
    lora = _dot(lora_in, lora_ref[...])
    ld_out[...] = -DECAY_SCALE * _sigmoid(dbias_ref[...] + lora[:, 0:D_RWKV])
    a = _sigmoid(ibias_ref[...] + lora[:, D_RWKV:2 * D_RWKV])
    a_out[...] = a
    g_out[...] = _dot(_sigmoid(xg), wgate_ref[...]).astype(BF16)

    if has_vres:
        vdown = jnp.dot(xn, wvdown_ref[...], preferred_element_type=F32)
        v = v + (vfirst_ref[...] - v) * _sigmoid(vbias_ref[...] + _dot(vdown, wvup_ref[...]))
    v_out[...] = v

    kk = k * kk_ref[...]
    norm = jnp.sqrt(_head_sums(kk * kk, ones_ref[...]))
    kkn_out[...] = kk / jnp.maximum(norm, 1e-12)
    kp = k * (1.0 + (a - 1.0) * ka_ref[...])
    kp_out[...] = kp
    r_out[...] = r
    bonus_out[...] = (_head_sums(r * kp * rk_ref[...], ones_ref[...]) * v).astype(BF16)

    if has_first:
        last_out[...] = p_scr[CARRY:CARRY + tm, :]
    else:
        tail = p_scr[tm:tm + CARRY, :]
        last_out[...] = tail
        p_scr[0:CARRY, :] = tail


def _full(shape):
    nd = len(shape)
    return pl.BlockSpec(shape, lambda *_: (0,) * nd, pipeline_mode=pl.Buffered(1))


def _const_spec(a):
    if isinstance(a, tuple):
        arr, l = a
        return pl.BlockSpec((None,) + arr.shape[1:], lambda *_: (l,) + (0,) * (arr.ndim - 1),
                            pipeline_mode=pl.Buffered(1))
    return _full(a.shape)


def _const_arg(a):
    return a[0] if isinstance(a, tuple) else a


def _proj_call(x2d, seq_len, tm, lw, vfirst, first):
    n = x2d.shape[0]
    d_model = x2d.shape[1]
    has_vres = vfirst is not None
    has_first = first is not None
    nt = n // tm
    row = lambda w: pl.BlockSpec((tm, w), lambda i: (i, 0))
    ins = [x2d, lw['g_mix_pre'], lw['w_in'], lw['mu'], lw['lora'], lw['decay_bias'], lw['iclr_bias'],
           lw['w_gate_up'], lw['k_k'], lw['k_a'], lw['r_k'], lw['ones']]
    specs = [row(d_model)] + [_const_spec(a) for a in ins[1:]]
    if has_vres:
        ins += [vfirst, lw['vres_bias'], lw['w_vres_down'], lw['w_vres_up']]
        specs += [row(D_RWKV)] + [_const_spec(lw[k]) for k in ('vres_bias', 'w_vres_down', 'w_vres_up')]
    if has_first:
        ins.append(first)
        specs.append(row(D_SHIFT))
    sds = lambda w, dt=F32: jax.ShapeDtypeStruct((n, w), dt)
    last_rows = tm if has_first else CARRY
    out_shape = [sds(D_RWKV)] * 6 + [sds(D_RWKV, BF16)] * 2 + [sds(D_POOL), sds(2 * d_model, BF16),
                                     jax.ShapeDtypeStruct((nt * last_rows, D_SHIFT), F32)]
    out_specs = [row(D_RWKV)] * 8 + [row(D_POOL), row(2 * d_model),
                                     pl.BlockSpec((last_rows, D_SHIFT), lambda i: (i, 0))]
    return pl.pallas_call(
        functools.partial(_proj_kernel, tm=tm, seq_len=seq_len, has_vres=has_vres, has_first=has_first),
        grid=(nt,), in_specs=specs, out_specs=out_specs, out_shape=out_shape,
        scratch_shapes=[pltpu.VMEM((tm + CARRY, D_SHIFT), F32)],
        compiler_params=pltpu.CompilerParams(dimension_semantics=("arbitrary",),
                                             vmem_limit_bytes=VMEM_LIMIT),
    )(*[_const_arg(a) for a in ins])


def _wkv_kernel(r_ref, kp_ref, v_ref, kk_ref, a_ref, ld_ref, o_ref, hout_ref, h_scr, *, n_sub):
    c_idx = pl.program_id(1)
    n_steps = pl.num_programs(1)
    C = CHUNK

    lane = jax.lax.broadcasted_iota(jnp.int32, (1, PAIR), 1)
    left = lane < HEAD
    ri = jax.lax.broadcasted_iota(jnp.int32, (PAIR, PAIR), 0)
    ci = jax.lax.broadcasted_iota(jnp.int32, (PAIR, PAIR), 1)
    same = (ri >= HEAD) == (ci >= HEAD)
    strict = same & (ci < ri)
    incl = same & (ci <= ri)
    eye = ri == ci
    eye_f = jnp.where(eye, 1.0, 0.0).astype(F32)

    def stack(x):
        return jnp.concatenate([jnp.where(left, x, 0.0), jnp.where(left, 0.0, x)], axis=0)

    def twice(x):
        return jnp.concatenate([x, x], axis=0)

    @pl.when(c_idx == 0)
    def _():
        h_scr[...] = jnp.zeros(h_scr.shape, F32)


    a_t, r_t, b_t, k_t, b_h, k_h, v_st, w_last = ([] for _ in range(8))
    for j in range(n_sub):
        rows = slice(j * C, (j + 1) * C)
        ld = ld_ref[rows, :]
        cum = _prefix_rows(ld)
        clast = cum[C - 1:C, :]
        e_cum = jnp.exp(cum)
        e_prev = jnp.exp(cum - ld)
        e_neg = jnp.exp(-cum)
        e_rem = jnp.exp(clast - cum)
        wl = jnp.exp(clast)
        kk = kk_ref[rows, :]
        b = kk * a_ref[rows, :]
        kp = kp_ref[rows, :]
        full = (-(kk * e_prev), r_ref[rows, :] * e_cum, b * e_neg, kp * e_neg, b * e_rem, kp * e_rem)
        v = v_ref[rows, :]
        for p in range(N_PAIRS):
            sl = slice(p * PAIR, (p + 1) * PAIR)
            for dst, src in zip((a_t, r_t, b_t, k_t, b_h, k_h), full):
                dst.append(src[:, sl])
            v_st.append(stack(v[:, sl]))
            w_last.append(wl[:, sl])

    units = range(n_sub * N_PAIRS)
    sc = [_dot_nt(jnp.concatenate([a_t[q], r_t[q]], axis=0),
                  jnp.concatenate([stack(b_t[q]), stack(k_t[q])], axis=0)) for q in units]
    n_ab = [jnp.where(strict, twice(s[0:C, 0:PAIR]), 0.0) for s in sc]
    n_ak = [jnp.where(strict, twice(s[0:C, PAIR:2 * PAIR]), 0.0) for s in sc]
    n_rb = [jnp.where(incl, twice(s[C:2 * C, 0:PAIR]), 0.0) for s in sc]
    n_rk = [jnp.where(incl, twice(s[C:2 * C, PAIR:2 * PAIR]), 0.0) for s in sc]

    inv = [eye_f + n for n in n_ab]
    power = n_ab
    y = [_dot(n_ak[q], v_st[q]) for q in units]
    for _ in range(int(np.log2(C)) - 1):
        power = [_dot(m, m) for m in power]
        inv = [inv[q] + _dot(inv[q], power[q]) for q in units]

    au = [_dot(inv[q], jnp.concatenate([stack(a_t[q]), y[q]], axis=1)) for q in units]
    zero = jnp.zeros((PAIR, PAIR), F32)
    big = []
    for q in units:
        rhs = jnp.concatenate([au[q], jnp.concatenate([zero, v_st[q]], axis=1)], axis=0)
        lhs = jnp.concatenate(
            [jnp.concatenate([n_rb[q], n_rk[q]], axis=1),
             jnp.concatenate([stack(b_h[q]).T, stack(k_h[q]).T], axis=1)], axis=0)
        big.append(_dot(lhs, rhs))

    h = [h_scr[p] for p in range(N_PAIRS)]
    for j in range(n_sub):
        rows = slice(j * C, (j + 1) * C)
        qs = [j * N_PAIRS + p for p in range(N_PAIRS)]
        sd = []
        for p, q in enumerate(qs):
            r_hat = stack(r_t[q]) + big[q][0:PAIR, 0:PAIR]
            m_low = big[q][PAIR:2 * PAIR, 0:PAIR]
            sd.append(_dot(jnp.concatenate([r_hat, m_low], axis=0), h[p]))
        for p, q in enumerate(qs):
            o_st = sd[p][0:PAIR] + big[q][0:PAIR, PAIR:2 * PAIR]
            o_ref[rows, p * PAIR:(p + 1) * PAIR] = o_st[0:C] + o_st[C:2 * C]
            w_col = jnp.sum(jnp.where(eye, jnp.broadcast_to(w_last[q], (PAIR, PAIR)), 0.0),
                            axis=1, keepdims=True)
            h[p] = w_col * h[p] + sd[p][PAIR:2 * PAIR] + big[q][PAIR:2 * PAIR, PAIR:2 * PAIR]
    for p in range(N_PAIRS):
        h_scr[p] = h[p]

    @pl.when(c_idx == n_steps - 1)
    def _():
        for p in range(N_PAIRS):
            hout_ref[0, p] = h_scr[p, 0:HEAD, :] + h_scr[p, HEAD:PAIR, :]


def _wkv_call(ops, n_seq, seq_len):
    n_sub = WKV_SUB if seq_len % (WKV_SUB * CHUNK) == 0 else 1
    rows = n_sub * CHUNK
    assert seq_len % rows == 0
    ns = seq_len // rows
    blk = pl.BlockSpec((rows, D_RWKV), lambda b, c: (b * ns + c, 0))
    hspec = pl.BlockSpec((1, N_PAIRS, HEAD, PAIR), lambda b, c: (b, 0, 0, 0))
    return pl.pallas_call(
        functools.partial(_wkv_kernel, n_sub=n_sub),
        grid=(n_seq, ns), in_specs=[blk] * 6, out_specs=[blk, hspec],
        out_shape=[jax.ShapeDtypeStruct((n_seq * seq_len, D_RWKV), F32),
                   jax.ShapeDtypeStruct((n_seq, N_PAIRS, HEAD, PAIR), F32)],
        scratch_shapes=[pltpu.VMEM((N_PAIRS, PAIR, PAIR), F32)],
        compiler_params=pltpu.CompilerParams(dimension_semantics=("arbitrary", "arbitrary"),
                                             vmem_limit_bytes=VMEM_LIMIT),
    )(*ops)


def _wkv_lanes_kernel(r_ref, kp_ref, v_ref, kk_ref, a_ref, ld_ref, s_ref, o_ref, sout_ref, op_scr, o_scr, *,
                      seq_len):
    T = seq_len
    nb = r_ref.shape[0] // T
    tok = lambda ref, t: ref[pl.ds(t, nb, stride=T), :]

    for t in range(T):
        kk = tok(kk_ref, t)
        op_scr[0, t] = (-kk).T
        op_scr[1, t] = jnp.exp(tok(ld_ref, t)).T
        op_scr[2, t] = (kk * tok(a_ref, t)).T
        op_scr[3, t] = tok(kp_ref, t).T
        op_scr[4, t] = tok(r_ref, t).T
        op_scr[5, t] = tok(v_ref, t).T

    def all_keys(x):
        y = jnp.sum(x.reshape(HEAD // CARRY, CARRY, nb), axis=0)
        for i in range(1, int(np.log2(CARRY)) + 1):
            y = y + pltpu.roll(y, CARRY >> i, 0)
        return y

    def over_keys(y, x):
        return (y[None] * x.reshape(HEAD // CARRY, CARRY, nb)).reshape(HEAD, nb)

    for hh in range(2):
        keys = slice(hh * HEAD, (hh + 1) * HEAD)

        def value_row(vi, carry, hh=hh, keys=keys):
            s = s_ref[hh, vi]
            row = hh * HEAD + vi
            for t in range(T):
                sa = all_keys(s * op_scr[0, t, keys, :])
                v_row = jnp.broadcast_to(op_scr[5, t, pl.ds(row, 1), :], (CARRY, nb))
                s = (s * op_scr[1, t, keys, :] + over_keys(sa, op_scr[2, t, keys, :])
                     + over_keys(v_row, op_scr[3, t, keys, :]))
                o_scr[t, pl.ds(row, 1), :] = all_keys(s * op_scr[4, t, keys, :])[0:1]
            sout_ref[hh, vi] = s
            return carry

        jax.lax.fori_loop(0, HEAD, value_row, 0, unroll=LANES_UNROLL)

    for t in range(T):
        o_ref[pl.ds(t, nb, stride=T), :] = o_scr[t].T


def _wkv_lanes_call(ops, n_seq, seq_len, state_t, layer):
    assert n_seq == LANES
    rows = n_seq * seq_len
    blk = pl.BlockSpec((rows, PAIR), lambda p: (0, p))
    in_state = pl.BlockSpec((None, 2, HEAD, HEAD, n_seq), lambda p: (layer, p, 0, 0, 0))
    out_state = pl.BlockSpec((2, HEAD, HEAD, n_seq), lambda p: (p, 0, 0, 0))
    return pl.pallas_call(
        functools.partial(_wkv_lanes_kernel, seq_len=seq_len),
        grid=(N_PAIRS,), in_specs=[blk] * 6 + [in_state], out_specs=[blk, out_state],
        out_shape=[jax.ShapeDtypeStruct((rows, D_RWKV), F32),
                   jax.ShapeDtypeStruct(state_t.shape[1:], F32)],
        scratch_shapes=[pltpu.VMEM((6, seq_len, PAIR, n_seq), F32), pltpu.VMEM((seq_len, PAIR, n_seq), F32)],
        compiler_params=pltpu.CompilerParams(dimension_semantics=("arbitrary",),
                                             vmem_limit_bytes=VMEM_LIMIT),
    )(*ops, state_t)


def _merge_kernel(*refs, tm, tiles_per_seq, banded):
    it = iter(refs)
    o_ref, bonus_ref, g_ref, u_ref, hist_ref, gates_ref, x_ref = (next(it) for _ in range(7))
    if banded:
        band_ref = next(it)
    (invc_ref, lnw_ref, lnb_ref, ones_ref, pmap_ref, pscale_ref, wba_ref, wbb_ref, wout_ref, gpost_ref,
     y_ref) = (next(it) for _ in range(11))
    if not banded:
        s_a, s_b = next(it), next(it)
    d_model = x_ref.shape[1]
    inv_h = 1.0 / HEAD

    o = o_ref[...]
    mean = _head_sums(o, ones_ref[...]) * inv_h
    cen = o - mean
    var = _head_sums(cen * cen, ones_ref[...]) * inv_h
    o_n = cen * jax.lax.rsqrt(var + GN_EPS) * lnw_ref[...] + lnb_ref[...]
    o_rwkv = (o_n + bonus_ref[...].astype(F32)) * g_ref[...].astype(F32)

    u = u_ref[...]
    lane = jax.lax.broadcasted_iota(jnp.int32, (1, D_POOL), 1)
    win_sum = jnp.zeros((tm, D_POOL), F32)
    if banded:
        full = jnp.concatenate([hist_ref[...], u], axis=0)
        for gi in range(len(POOL_WINDOWS)):
            s = _dot_exact_lhs(band_ref[gi], full, 2)
            in_group = (lane >= gi * POOL_GROUP) & (lane < (gi + 1) * POOL_GROUP)
            win_sum = jnp.where(in_group, s, win_sum)
    else:
        keep = jnp.where((pl.program_id(0) % tiles_per_seq) == 0, 0.0, 1.0)
        s_a[0:POOL_HIST, :] = hist_ref[...] * keep
        s_a[POOL_HIST:POOL_HIST + tm, :] = u
        src, dst = s_a, s_b
        total = POOL_HIST + tm
        w = 1
        for gi, win in enumerate(POOL_WINDOWS):
            while w < win:
                lo = CARRY * int(np.log2(2 * w))
                dst[lo:total, :] = src[lo:total, :] + src[lo - w:total - w, :]
                src, dst = dst, src
                w *= 2
            in_group = (lane >= gi * POOL_GROUP) & (lane < (gi + 1) * POOL_GROUP)
            win_sum = jnp.where(in_group, src[POOL_HIST:total, :], win_sum)
    diff = win_sum * invc_ref[...] - u
    o_pool = _dot(diff, pmap_ref[...]) * pscale_ref[...]

    gates = gates_ref[...].astype(F32)
    merged = (gates[:, 0:d_model] * _dot(o_rwkv, wba_ref[...])
              + gates[:, d_model:2 * d_model] * _dot(o_pool, wbb_ref[...]))
    y_ref[...] = x_ref[...] + _rms(_dot(merged, wout_ref[...]), gpost_ref[...])


def _merge_call(o, bonus, g, u, hist, hist_rows, gates, x2d, seq_len, tm, band, invc, lw):
    n, d_model = x2d.shape
    nt = n // tm
    tiles_per_seq = max(seq_len // tm, 1)
    banded = band is not None
    row = lambda w: pl.BlockSpec((tm, w), lambda i: (i, 0))
    if banded:
        hist_spec = pl.BlockSpec((hist_rows, D_POOL), lambda i: (i, 0))
        invc_spec = _full(invc.shape)
        scratch = []
    else:
        per = tm // hist_rows
        hist_spec = pl.BlockSpec((hist_rows, D_POOL), lambda i: (jnp.maximum(i * per - 1, 0), 0))
        invc_spec = pl.BlockSpec((tm, D_POOL), lambda i: (i % tiles_per_seq, 0))
        scratch = [pltpu.VMEM((hist_rows + tm, D_POOL), F32)] * 2
    consts = [lw['ln_x_w'], lw['ln_x_b'], lw['ones'], lw['pool_map'], lw['pool_scale'],
              lw['w_branch_rwkv'], lw['w_branch_pool'], lw['w_out'], lw['g_mix_post']]
    ins = [o, bonus, g, u, hist, gates, x2d] + ([band] if banded else []) + [invc] + consts
    specs = ([row(D_RWKV)] * 3 + [row(D_POOL), hist_spec, row(2 * d_model), row(d_model)]
             + ([_full(band.shape)] if banded else []) + [invc_spec] + [_const_spec(a) for a in consts])
    return pl.pallas_call(
        functools.partial(_merge_kernel, tm=tm, tiles_per_seq=tiles_per_seq, banded=banded),
        grid=(nt,), in_specs=specs, out_specs=row(d_model),
        out_shape=jax.ShapeDtypeStruct((n, d_model), F32), scratch_shapes=scratch,
        compiler_params=pltpu.CompilerParams(dimension_semantics=("arbitrary",),
                                             vmem_limit_bytes=VMEM_LIMIT),
    )(*[_const_arg(a) for a in ins])


def _ffn_kernel(x_ref, gpre_ref, win_ref, wout_ref, gpost_ref, y_ref):
    x = x_ref[...]
    d_ff = wout_ref.shape[0]
    h = _rms(x, gpre_ref[...]).astype(BF16)
    gu = jnp.dot(h, win_ref[...], preferred_element_type=F32)
    gt = gu[:, 0:d_ff]
    act = gt * _sigmoid(gt) * gu[:, d_ff:2 * d_ff]
    y_ref[...] = x + _rms(_dot(act, wout_ref[...]), gpost_ref[...])


def _ffn_call(x2d, tm, lw):
    n, d_model = x2d.shape
    row = pl.BlockSpec((tm, d_model), lambda i: (i, 0))
    consts = [lw['g_ffn_pre'], lw['w_ffn_in'], lw['w_ffn_out'], lw['g_ffn_post']]
    return pl.pallas_call(
        _ffn_kernel, grid=(n // tm,), in_specs=[row] + [_const_spec(a) for a in consts],
        out_specs=row, out_shape=jax.ShapeDtypeStruct((n, d_model), F32),
        compiler_params=pltpu.CompilerParams(dimension_semantics=("arbitrary",),
                                             vmem_limit_bytes=VMEM_LIMIT),
    )(x2d, *[_const_arg(a) for a in consts])


def _prompt_inv_count(seq_len):
    pos = np.arange(seq_len)[:, None]
    win = np.repeat(np.array(POOL_WINDOWS), POOL_GROUP)[None, :]
    return jnp.asarray(1.0 / np.minimum(pos + 1, win).astype(np.float32), F32)


def _sample_pool_consts(tm, seq_len):
    ns = tm // seq_len
    rs, rt = np.divmod(np.arange(tm), seq_len)
    hs, hj = np.divmod(np.arange(ns * POOL_SLOTS), POOL_SLOTS)
    cs = np.concatenate([hs, rs])[None, :]
    cpos = np.concatenate([hj - 1, POOL_BUF + rt])[None, :]
    dist = (POOL_BUF + rt)[:, None] - cpos
    same = (rs[:, None] == cs) & (cpos >= 0)
    band = np.stack([same & (dist >= 0) & (dist < w) for w in POOL_WINDOWS]).astype(np.float32)
    win = np.repeat(np.array(POOL_WINDOWS), POOL_GROUP)[None, :]
    pos = (SAMPLE_START_POS + rt)[:, None]
    invc = 1.0 / np.minimum(pos + 1, win).astype(np.float32)
    return jnp.asarray(band, BF16), jnp.asarray(invc, F32)


def _block_diag(blocks):
    n = len(blocks)
    rows = []
    for i, blk in enumerate(blocks):
        rows.append(jnp.concatenate(
            [blk if j == i else jnp.zeros((blk.shape[0], blocks[j].shape[1]), blk.dtype) for j in range(n)],
            axis=1))
    return jnp.concatenate(rows, axis=0)


BIG_WEIGHTS = ('w_in', 'w_gate_up', 'w_branch_rwkv', 'w_branch_pool', 'w_out', 'w_ffn_in', 'w_ffn_out')


def _layer_weights(l, w, wb):
    row = lambda a: a.reshape(1, -1).astype(F32)
    lw = {k: (wb[k], l) for k in BIG_WEIGHTS}
    lw.update({
        'g_mix_pre': row(w['norm_mix_pre'][l]), 'g_mix_post': row(w['norm_mix_post'][l]),
        'g_ffn_pre': row(w['norm_ffn_pre'][l]), 'g_ffn_post': row(w['norm_ffn_post'][l]),
        'mu': row(w['mu_shift'][l]), 'decay_bias': row(w['decay_bias'][l]), 'iclr_bias': row(w['iclr_bias'][l]),
        'lora': _block_diag([w['w_decay_up'][l], w['w_iclr_up'][l]]).astype(BF16),
        'k_k': row(w['k_k'][l]), 'k_a': row(w['k_a'][l]), 'r_k': row(w['r_k'][l]),
        'ln_x_w': row(w['ln_x_w'][l]), 'ln_x_b': row(w['ln_x_b'][l]),
        'pool_map': _block_diag([w['pool_map'][l, gi] for gi in range(len(POOL_WINDOWS))]).astype(BF16),
        'pool_scale': row(w['pool_scale'][l]),
        'ones': jnp.asarray(np.kron(np.eye(SUM_TILE // HEAD), np.ones((HEAD, HEAD))), BF16),
    })
    if l > 0:
        pad = LANES - LORA_VRES
        lw['vres_bias'] = row(w['vres_bias'][l - 1])
        lw['w_vres_down'] = jnp.pad(w['w_vres_down'][l - 1], ((0, 0), (0, pad))).astype(BF16)
        lw['w_vres_up'] = jnp.pad(w['w_vres_up'][l - 1], ((0, pad), (0, 0))).astype(BF16)
    return lw


def _from_pairs(h):
    lead = h.shape[:-3]
    n = len(lead)
    h = h.reshape(*lead, N_PAIRS, HEAD, 2, HEAD)
    return h.transpose(*range(n), n, n + 2, n + 3, n + 1).reshape(*lead, N_HEADS, HEAD, HEAD)


def _tile_rows(n_seq, seq_len, target):
    if seq_len >= target:
        return target
    return min(n_seq * seq_len, target)


def kernel(x_prompt, x_sample, state_wkv, state_shift, state_pool, norm_mix_pre, norm_mix_post, norm_ffn_pre, norm_ffn_post, w_in, mu_shift, decay_bias, w_decay_up, iclr_bias, w_iclr_up, w_gate_up, k_k, k_a, r_k, ln_x_w, ln_x_b, vres_bias, w_vres_down, w_vres_up, pool_map, pool_scale, w_branch_rwkv, w_branch_pool, w_out, w_ffn_in, w_ffn_out):
    weights = dict(norm_mix_pre=norm_mix_pre, norm_mix_post=norm_mix_post, norm_ffn_pre=norm_ffn_pre,
                   norm_ffn_post=norm_ffn_post, w_in=w_in, mu_shift=mu_shift, decay_bias=decay_bias,
                   w_decay_up=w_decay_up, iclr_bias=iclr_bias, w_iclr_up=w_iclr_up, w_gate_up=w_gate_up,
                   k_k=k_k, k_a=k_a, r_k=r_k, ln_x_w=ln_x_w, ln_x_b=ln_x_b, vres_bias=vres_bias,
                   w_vres_down=w_vres_down, w_vres_up=w_vres_up, pool_map=pool_map, pool_scale=pool_scale,
                   w_branch_rwkv=w_branch_rwkv, w_branch_pool=w_branch_pool, w_out=w_out,
                   w_ffn_in=w_ffn_in, w_ffn_out=w_ffn_out)
    depth = w_in.shape[0]
    wb = {k: weights[k].astype(BF16) for k in BIG_WEIGHTS}
    bp, tp, d_model = x_prompt.shape
    bs, ts, _ = x_sample.shape
    tm_p = _tile_rows(bp, tp, ROW_TILE)
    tm_s = _tile_rows(bs, ts, ROW_TILE)
    invc_p = _prompt_inv_count(tp)
    band_s, invc_s = _sample_pool_consts(tm_s, ts)

    state_t = state_wkv.transpose(0, 2, 3, 4, 1)
    yp = x_prompt.reshape(bp * tp, d_model)
    ys = x_sample.reshape(bs * ts, d_model)
    vf_p = vf_s = None
    outs = {k: [] for k in ('wkv_p', 'shift_p', 'pool_p', 'wkv_s', 'shift_s', 'pool_s')}
    for l in range(depth):
        lw = _layer_weights(l, weights, wb)

        (r, kp, v, kkn, a, ld, g, bonus, u, gates, last) = _proj_call(yp, tp, tm_p, lw, vf_p, None)
        if l == 0:
            vf_p = v
        o, hout = _wkv_call((r, kp, v, kkn, a, ld), bp, tp)
        x1 = _merge_call(o, bonus, g, u, u, POOL_HIST, gates, yp, tp, 2 * tm_p, None, invc_p, lw)
        yp = _ffn_call(x1, 2 * tm_p, lw)
        outs['wkv_p'].append(_from_pairs(hout))
        tiles = tp // tm_p
        outs['shift_p'].append(last.reshape(bp, tiles, CARRY, D_SHIFT)[:, -1, -1])
        outs['pool_p'].append(u.reshape(bp, tp, D_POOL)[:, -POOL_BUF:])

        first = jnp.zeros((bs, ts, D_SHIFT), F32).at[:, 0].set(state_shift[l]).reshape(bs * ts, D_SHIFT)
        (r, kp, v, kkn, a, ld, g, bonus, u, gates, ps_all) = _proj_call(ys, ts, tm_s, lw, vf_s, first)
        if l == 0:
            vf_s = v
        o, hout = _wkv_lanes_call((r, kp, v, kkn, a, ld), bs, ts, state_t, l)
        outs['wkv_s'].append(hout)
        hist = jnp.pad(state_pool[l], ((0, 0), (1, 0), (0, 0))).reshape(bs * POOL_SLOTS, D_POOL)
        x1 = _merge_call(o, bonus, g, u, hist, (tm_s // ts) * POOL_SLOTS, gates, ys, ts, tm_s, band_s, invc_s,
                         lw)
        ys = _ffn_call(x1, tm_s, lw)
        outs['shift_s'].append(ps_all.reshape(bs, ts, D_SHIFT)[:, -1])
        outs['pool_s'].append(jnp.concatenate(
            [state_pool[l], u.reshape(bs, ts, D_POOL)], axis=1)[:, -POOL_BUF:])

    return (yp.reshape(bp, tp, d_model), ys.reshape(bs, ts, d_model),
            jnp.stack(outs['wkv_p']), jnp.stack(outs['shift_p']), jnp.stack(outs['pool_p']),
            jnp.stack(outs['wkv_s']).transpose(0, 4, 1, 2, 3), jnp.stack(outs['shift_s']), jnp.stack(outs['pool_s']))
```

```python
import functools

import numpy as np
import jax
import jax.numpy as jnp
from jax.experimental import pallas as pl
from jax.experimental.pallas import tpu as pltpu

F32 = jnp.float32
BF16 = jnp.bfloat16

HEAD = 64
D_RWKV = 768
N_HEADS = D_RWKV // HEAD
PAIR = 2 * HEAD
N_PAIRS = D_RWKV // PAIR
D_POOL = 256
POOL_WINDOWS = (2, 4, 8, 16)
POOL_GROUP = D_POOL // len(POOL_WINDOWS)
POOL_BUF = max(POOL_WINDOWS) - 1
POOL_SLOTS = POOL_BUF + 1
LORA_DECAY = 64
LORA_ICLR = 64
LORA_GATE = 128
LORA_VRES = 32
D_SHIFT = 3 * D_RWKV + LORA_DECAY + LORA_ICLR + LORA_GATE
SAMPLE_START_POS = 16384
RMS_EPS = 1e-6
GN_EPS = 1e-5 * HEAD
DECAY_SCALE = float(np.exp(-0.5))
CHUNK = 64
WKV_SUB = 4
POOL_HIST = 32
CARRY = 8
LANES = 128
LANES_UNROLL = 16
SUM_TILE = 256
ROW_TILE = 256
VMEM_LIMIT = 56 * 1024 * 1024


def _dot(a, b):
    return jnp.dot(a.astype(BF16), b.astype(BF16), preferred_element_type=F32)


def _dot_nt(a, b):
    return jax.lax.dot_general(a.astype(BF16), b.astype(BF16), (((1,), (1,)), ((), ())),
                               preferred_element_type=F32)


def _split(x, parts):
    out = []
    for _ in range(parts - 1):
        hi = x.astype(BF16)
        out.append(hi)
        x = x - hi.astype(F32)
    out.append(x.astype(BF16))
    return out


def _dot_exact_lhs(sel, x, parts):
    acc = None
    for p in _split(x, parts):
        t = jnp.dot(sel, p, preferred_element_type=F32)
        acc = t if acc is None else acc + t
    return acc


def _head_sums(x, ones):
    return jnp.concatenate(
        [_dot(x[:, c:c + SUM_TILE], ones) for c in range(0, x.shape[1], SUM_TILE)], axis=1)


def _prefix_rows(x):
    rows, cols = x.shape
    tiles = rows // CARRY
    y = x.reshape(tiles, CARRY, cols)
    sub = jax.lax.broadcasted_iota(jnp.int32, (1, CARRY, 1), 1)
    d = 1
    while d < CARRY:
        y = y + jnp.where(sub >= d, pltpu.roll(y, d, 1), 0.0)
        d *= 2
    total = y[:, CARRY - 1:CARRY, :]
    run = jnp.zeros((1, 1, cols), F32)
    before = []
    for i in range(tiles):
        before.append(run)
        run = run + total[i:i + 1]
    return (y + jnp.concatenate(before, axis=0)).reshape(rows, cols)


def _sigmoid(x):
    return 1.0 / (1.0 + jnp.exp(-x))


def _rms(x, g):
    return x * jax.lax.rsqrt(jnp.mean(x * x, axis=-1, keepdims=True) + RMS_EPS) * g


def _proj_kernel(*refs, tm, seq_len, has_vres, has_first):
    it = iter(refs)
    x_ref, gpre_ref, win_ref, mu_ref, lora_ref, dbias_ref, ibias_ref, wgate_ref = (next(it) for _ in range(8))
    kk_ref, ka_ref, rk_ref, ones_ref = (next(it) for _ in range(4))
    if has_vres:
        vfirst_ref, vbias_ref, wvdown_ref, wvup_ref = (next(it) for _ in range(4))
    if has_first:
        first_ref = next(it)
    (r_out, kp_out, v_out, kkn_out, a_out, ld_out, g_out, bonus_out, u_out, gates_out,
     last_out, p_scr) = (next(it) for _ in range(12))

    gates_end = D_SHIFT + D_POOL + 2 * x_ref.shape[1]
    i = pl.program_id(0)
    if seq_len >= tm:
        tiles_per_seq = seq_len // tm
        new_seq = (i % tiles_per_seq) == 0
    else:
        new_seq = i == 0

    @pl.when(new_seq)
    def _():
        p_scr[0:CARRY, :] = jnp.zeros((CARRY, D_SHIFT), F32)

    xn = _rms(x_ref[...], gpre_ref[...]).astype(BF16)
    p_scr[CARRY:CARRY + tm, :] = jnp.dot(xn, win_ref[:, 0:D_SHIFT], preferred_element_type=F32)
    u_out[...] = jnp.dot(xn, win_ref[:, D_SHIFT:D_SHIFT + D_POOL], preferred_element_type=F32)
    gates_out[...] = _sigmoid(jnp.dot(xn, win_ref[:, D_SHIFT + D_POOL:gates_end],
                                      preferred_element_type=F32)).astype(BF16)

    if has_first:
        row = jax.lax.broadcasted_iota(jnp.int32, (tm, 1), 0)
        seq_start = jax.lax.rem(row, seq_len) == 0

    def mixed(c0, c1):
        ps = p_scr[CARRY:CARRY + tm, c0:c1]
        prev = p_scr[CARRY - 1:CARRY - 1 + tm, c0:c1]
        if has_first:
            prev = jnp.where(seq_start, first_ref[:, c0:c1], prev)
        return ps + (prev - ps) * mu_ref[:, c0:c1]

    o1, o2, o3 = D_RWKV, 2 * D_RWKV, 3 * D_RWKV
    o5 = o3 + LORA_DECAY + LORA_ICLR
    r = mixed(0, o1)
    k = mixed(o1, o2)
    v = mixed(o2, o3)
    xwa = mixed(o3, o5)
    xg = mixed(o5, D_SHIFT)

    lane = jax.lax.broadcasted_iota(jnp.int32, (1, LANES), 1)
    lora_in = jnp.where(lane < LORA_DECAY, jnp.tanh(xwa), xwa)
    lora = _dot(lora_in, lora_ref[...])
    ld_out[...] = -DECAY_SCALE * _sigmoid(dbias_ref[...] + lora[:, 0:D_RWKV])
    a = _sigmoid(ibias_ref[...] + lora[:, D_RWKV:2 * D_RWKV])
    a_out[...] = a
    g_out[...] = _dot(_sigmoid(xg), wgate_ref[...]).astype(BF16)

    if has_vres:
        vdown = jnp.dot(xn, wvdown_ref[...], preferred_element_type=F32)
        v = v + (vfirst_ref[...] - v) * _sigmoid(vbias_ref[...] + _dot(vdown, wvup_ref[...]))
    v_out[...] = v

    kk = k * kk_ref[...]
    norm = jnp.sqrt(_head_sums(kk * kk, ones_ref[...]))
    kkn_out[...] = kk / jnp.maximum(norm, 1e-12)
    kp = k * (1.0 + (a - 1.0) * ka_ref[...])
    kp_out[...] = kp
    r_out[...] = r
    bonus_out[...] = (_head_sums(r * kp * rk_ref[...], ones_ref[...]) * v).astype(BF16)

    if has_first:
        last_out[...] = p_scr[CARRY:CARRY + tm, :]
    else:
        tail = p_scr[tm:tm + CARRY, :]
        last_out[...] = tail
        p_scr[0:CARRY, :] = tail


def _full(shape):
    nd = len(shape)
    return pl.BlockSpec(shape, lambda *_: (0,) * nd, pipeline_mode=pl.Buffered(1))


def _const_spec(a):
    if isinstance(a, tuple):
        arr, l = a
        return pl.BlockSpec((None,) + arr.shape[1:], lambda *_: (l,) + (0,) * (arr.ndim - 1),
                            pipeline_mode=pl.Buffered(1))
    return _full(a.shape)


def _const_arg(a):
    return a[0] if isinstance(a, tuple) else a


def _proj_call(x2d, seq_len, tm, lw, vfirst, first):
    n = x2d.shape[0]
    d_model = x2d.shape[1]
    has_vres = vfirst is not None
    has_first = first is not None
    nt = n // tm
    row = lambda w: pl.BlockSpec((tm, w), lambda i: (i, 0))
    ins = [x2d, lw['g_mix_pre'], lw['w_in'], lw['mu'], lw['lora'], lw['decay_bias'], lw['iclr_bias'],
           lw['w_gate_up'], lw['k_k'], lw['k_a'], lw['r_k'], lw['ones']]
    specs = [row(d_model)] + [_const_spec(a) for a in ins[1:]]
    if has_vres:
        ins += [vfirst, lw['vres_bias'], lw['w_vres_down'], lw['w_vres_up']]
        specs += [row(D_RWKV)] + [_const_spec(lw[k]) for k in ('vres_bias', 'w_vres_down', 'w_vres_up')]
    if has_first:
        ins.append(first)
        specs.append(row(D_SHIFT))
    sds = lambda w, dt=F32: jax.ShapeDtypeStruct((n, w), dt)
    last_rows = tm if has_first else CARRY
    out_shape = [sds(D_RWKV)] * 6 + [sds(D_RWKV, BF16)] * 2 + [sds(D_POOL), sds(2 * d_model, BF16),
                                     jax.ShapeDtypeStruct((nt * last_rows, D_SHIFT), F32)]
    out_specs = [row(D_RWKV)] * 8 + [row(D_POOL), row(2 * d_model),
                                     pl.BlockSpec((last_rows, D_SHIFT), lambda i: (i, 0))]
    return pl.pallas_call(
        functools.partial(_proj_kernel, tm=tm, seq_len=seq_len, has_vres=has_vres, has_first=has_first),
        grid=(nt,), in_specs=specs, out_specs=out_specs, out_shape=out_shape,
        scratch_shapes=[pltpu.VMEM((tm + CARRY, D_SHIFT), F32)],
        compiler_params=pltpu.CompilerParams(dimension_semantics=("arbitrary",),
                                             vmem_limit_bytes=VMEM_LIMIT),
    )(*[_const_arg(a) for a in ins])


def _wkv_kernel(r_ref, kp_ref, v_ref, kk_ref, a_ref, ld_ref, o_ref, hout_ref, h_scr, *, n_sub):
    c_idx = pl.program_id(1)
    n_steps = pl.num_programs(1)
    C = CHUNK

    lane = jax.lax.broadcasted_iota(jnp.int32, (1, PAIR), 1)
    left = lane < HEAD
    ri = jax.lax.broadcasted_iota(jnp.int32, (PAIR, PAIR), 0)
    ci = jax.lax.broadcasted_iota(jnp.int32, (PAIR, PAIR), 1)
    same = (ri >= HEAD) == (ci >= HEAD)
    strict = same & (ci < ri)
    incl = same & (ci <= ri)
    eye = ri == ci
    eye_f = jnp.where(eye, 1.0, 0.0).astype(F32)

    def stack(x):
        return jnp.concatenate([jnp.where(left, x, 0.0), jnp.where(left, 0.0, x)], axis=0)

    def twice(x):
        return jnp.concatenate([x, x], axis=0)

    @pl.when(c_idx == 0)
    def _():
        h_scr[...] = jnp.zeros(h_scr.shape, F32)


    a_t, r_t, b_t, k_t, b_h, k_h, v_st, w_last = ([] for _ in range(8))
    for j in range(n_sub):
        rows = slice(j * C, (j + 1) * C)
        ld = ld_ref[rows, :]
        cum = _prefix_rows(ld)
        clast = cum[C - 1:C, :]
        e_cum = jnp.exp(cum)
        e_prev = jnp.exp(cum - ld)
        e_neg = jnp.exp(-cum)
        e_rem = jnp.exp(clast - cum)
        wl = jnp.exp(clast)
        kk = kk_ref[rows, :]
        b = kk * a_ref[rows, :]
        kp = kp_ref[rows, :]
        full = (-(kk * e_prev), r_ref[rows, :] * e_cum, b * e_neg, kp * e_neg, b * e_rem, kp * e_rem)
        v = v_ref[rows, :]
        for p in range(N_PAIRS):
            sl = slice(p * PAIR, (p + 1) * PAIR)
            for dst, src in zip((a_t, r_t, b_t, k_t, b_h, k_h), full):
                dst.append(src[:, sl])
            v_st.append(stack(v[:, sl]))
            w_last.append(wl[:, sl])

    units = range(n_sub * N_PAIRS)
    sc = [_dot_nt(jnp.concatenate([a_t[q], r_t[q]], axis=0),
                  jnp.concatenate([stack(b_t[q]), stack(k_t[q])], axis=0)) for q in units]
    n_ab = [jnp.where(strict, twice(s[0:C, 0:PAIR]), 0.0) for s in sc]
    n_ak = [jnp.where(strict, twice(s[0:C, PAIR:2 * PAIR]), 0.0) for s in sc]
    n_rb = [jnp.where(incl, twice(s[C:2 * C, 0:PAIR]), 0.0) for s in sc]
    n_rk = [jnp.where(incl, twice(s[C:2 * C, PAIR:2 * PAIR]), 0.0) for s in sc]

    inv = [eye_f + n for n in n_ab]
    power = n_ab
    y = [_dot(n_ak[q], v_st[q]) for q in units]
    for _ in range(int(np.log2(C)) - 1):
        power = [_dot(m, m) for m in power]
        inv = [inv[q] + _dot(inv[q], power[q]) for q in units]

    au = [_dot(inv[q], jnp.concatenate([stack(a_t[q]), y[q]], axis=1)) for q in units]
    zero = jnp.zeros((PAIR, PAIR), F32)
    big = []
    for q in units:
        rhs = jnp.concatenate([au[q], jnp.concatenate([zero, v_st[q]], axis=1)], axis=0)
        lhs = jnp.concatenate(
            [jnp.concatenate([n_rb[q], n_rk[q]], axis=1),
             jnp.concatenate([stack(b_h[q]).T, stack(k_h[q]).T], axis=1)], axis=0)
        big.append(_dot(lhs, rhs))

    h = [h_scr[p] for p in range(N_PAIRS)]
    for j in range(n_sub):
        rows = slice(j * C, (j + 1) * C)
        qs = [j * N_PAIRS + p for p in range(N_PAIRS)]
        sd = []
        for p, q in enumerate(qs):
            r_hat = stack(r_t[q]) + big[q][0:PAIR, 0:PAIR]
            m_low = big[q][PAIR:2 * PAIR, 0:PAIR]
            sd.append(_dot(jnp.concatenate([r_hat, m_low], axis=0), h[p]))
        for p, q in enumerate(qs):
            o_st = sd[p][0:PAIR] + big[q][0:PAIR, PAIR:2 * PAIR]
            o_ref[rows, p * PAIR:(p + 1) * PAIR] = o_st[0:C] + o_st[C:2 * C]
            w_col = jnp.sum(jnp.where(eye, jnp.broadcast_to(w_last[q], (PAIR, PAIR)), 0.0),
                            axis=1, keepdims=True)
            h[p] = w_col * h[p] + sd[p][PAIR:2 * PAIR] + big[q][PAIR:2 * PAIR, PAIR:2 * PAIR]
    for p in range(N_PAIRS):
        h_scr[p] = h[p]

    @pl.when(c_idx == n_steps - 1)
    def _():
        for p in range(N_PAIRS):
            hout_ref[0, p] = h_scr[p, 0:HEAD, :] + h_scr[p, HEAD:PAIR, :]


def _wkv_call(ops, n_seq, seq_len):
    n_sub = WKV_SUB if seq_len % (WKV_SUB * CHUNK) == 0 else 1
    rows = n_sub * CHUNK
    assert seq_len % rows == 0
    ns = seq_len // rows
    blk = pl.BlockSpec((rows, D_RWKV), lambda b, c: (b * ns + c, 0))
    hspec = pl.BlockSpec((1, N_PAIRS, HEAD, PAIR), lambda b, c: (b, 0, 0, 0))
    return pl.pallas_call(
        functools.partial(_wkv_kernel, n_sub=n_sub),
        grid=(n_seq, ns), in_specs=[blk] * 6, out_specs=[blk, hspec],
        out_shape=[jax.ShapeDtypeStruct((n_seq * seq_len, D_RWKV), F32),
                   jax.ShapeDtypeStruct((n_seq, N_PAIRS, HEAD, PAIR), F32)],
        scratch_shapes=[pltpu.VMEM((N_PAIRS, PAIR, PAIR), F32)],
        compiler_params=pltpu.CompilerParams(dimension_semantics=("arbitrary", "arbitrary"),
                                             vmem_limit_bytes=VMEM_LIMIT),
    )(*ops)


def _wkv_lanes_kernel(*refs, seq_len, n_prev):
    r_ref, kp_ref, v_ref, kk_ref, a_ref, ld_ref, s_ref = refs[:7]
    prev_refs = refs[7:7 + n_prev]
    o_ref, stacked_ref, op_scr, o_scr = refs[7 + n_prev:]
    for i, prev in enumerate(prev_refs):
        stacked_ref[i] = prev[...]
    sout_ref = stacked_ref.at[n_prev]
    T = seq_len
    nb = r_ref.shape[0] // T
    tok = lambda ref, t: ref[pl.ds(t, nb, stride=T), :]

    for t in range(T):
        kk = tok(kk_ref, t)
        op_scr[0, t] = (-kk).T
        op_scr[1, t] = jnp.exp(tok(ld_ref, t)).T
        op_scr[2, t] = (kk * tok(a_ref, t)).T
        op_scr[3, t] = tok(kp_ref, t).T
        op_scr[4, t] = tok(r_ref, t).T
        op_scr[5, t] = tok(v_ref, t).T

    def all_keys(x):
        y = jnp.sum(x.reshape(HEAD // CARRY, CARRY, nb), axis=0)
        for i in range(1, int(np.log2(CARRY)) + 1):
            y = y + pltpu.roll(y, CARRY >> i, 0)
        return y

    def over_keys(y, x):
        return (y[None] * x.reshape(HEAD // CARRY, CARRY, nb)).reshape(HEAD, nb)

    for hh in range(2):
        keys = slice(hh * HEAD, (hh + 1) * HEAD)

        def value_row(vi, carry, hh=hh, keys=keys):
            s = s_ref[hh, vi]
            row = hh * HEAD + vi
            for t in range(T):
                sa = all_keys(s * op_scr[0, t, keys, :])
                v_row = jnp.broadcast_to(op_scr[5, t, pl.ds(row, 1), :], (CARRY, nb))
                s = (s * op_scr[1, t, keys, :] + over_keys(sa, op_scr[2, t, keys, :])
                     + over_keys(v_row, op_scr[3, t, keys, :]))
                o_scr[t, pl.ds(row, 1), :] = all_keys(s * op_scr[4, t, keys, :])[0:1]
            sout_ref[hh, vi] = s
            return carry

        jax.lax.fori_loop(0, HEAD, value_row, 0, unroll=LANES_UNROLL)

    for t in range(T):
        o_ref[pl.ds(t, nb, stride=T), :] = o_scr[t].T


def _wkv_lanes_call(ops, n_seq, seq_len, state_t, layer, prev_states):
    assert n_seq == LANES
    n_prev = len(prev_states)
    rows = n_seq * seq_len
    blk = pl.BlockSpec((rows, PAIR), lambda p: (0, p))
    in_state = pl.BlockSpec((None, 2, HEAD, HEAD, n_seq), lambda p: (layer, p, 0, 0, 0))
    prev_spec = pl.BlockSpec((2, HEAD, HEAD, n_seq), lambda p: (p, 0, 0, 0))
    out_state = pl.BlockSpec((n_prev + 1, 2, HEAD, HEAD, n_seq), lambda p: (0, p, 0, 0, 0))
    return pl.pallas_call(
        functools.partial(_wkv_lanes_kernel, seq_len=seq_len, n_prev=n_prev),
        grid=(N_PAIRS,), in_specs=[blk] * 6 + [in_state] + [prev_spec] * n_prev, out_specs=[blk, out_state],
        out_shape=[jax.ShapeDtypeStruct((rows, D_RWKV), F32),
                   jax.ShapeDtypeStruct((n_prev + 1,) + state_t.shape[1:], F32)],
        scratch_shapes=[pltpu.VMEM((6, seq_len, PAIR, n_seq), F32), pltpu.VMEM((seq_len, PAIR, n_seq), F32)],
        compiler_params=pltpu.CompilerParams(dimension_semantics=("arbitrary",),
                                             vmem_limit_bytes=VMEM_LIMIT),
    )(*ops, state_t, *prev_states)


def _merge_kernel(*refs, tm, tiles_per_seq, banded):
    it = iter(refs)
    o_ref, bonus_ref, g_ref, u_ref, hist_ref, gates_ref, x_ref = (next(it) for _ in range(7))
    if banded:
        band_ref = next(it)
    (invc_ref, lnw_ref, lnb_ref, ones_ref, pmap_ref, pscale_ref, wba_ref, wbb_ref, wout_ref, gpost_ref,
     y_ref) = (next(it) for _ in range(11))
    if not banded:
        s_a, s_b = next(it), next(it)
    d_model = x_ref.shape[1]
    inv_h = 1.0 / HEAD

    o = o_ref[...]
    mean = _head_sums(o, ones_ref[...]) * inv_h
    cen = o - mean
    var = _head_sums(cen * cen, ones_ref[...]) * inv_h
    o_n = cen * jax.lax.rsqrt(var + GN_EPS) * lnw_ref[...] + lnb_ref[...]
    o_rwkv = (o_n + bonus_ref[...].astype(F32)) * g_ref[...].astype(F32)

    u = u_ref[...]
    lane = jax.lax.broadcasted_iota(jnp.int32, (1, D_POOL), 1)
    win_sum = jnp.zeros((tm, D_POOL), F32)
    if banded:
        full = jnp.concatenate([hist_ref[...], u], axis=0)
        for gi in range(len(POOL_WINDOWS)):
            s = _dot_exact_lhs(band_ref[gi], full, 2)
            in_group = (lane >= gi * POOL_GROUP) & (lane < (gi + 1) * POOL_GROUP)
            win_sum = jnp.where(in_group, s, win_sum)
    else:
        keep = jnp.where((pl.program_id(0) % tiles_per_seq) == 0, 0.0, 1.0)
        s_a[0:POOL_HIST, :] = hist_ref[...] * keep
        s_a[POOL_HIST:POOL_HIST + tm, :] = u
        src, dst = s_a, s_b
        total = POOL_HIST + tm
        w = 1
        for gi, win in enumerate(POOL_WINDOWS):
            while w < win:
                lo = CARRY * int(np.log2(2 * w))
                dst[lo:total, :] = src[lo:total, :] + src[lo - w:total - w, :]
                src, dst = dst, src
                w *= 2
            in_group = (lane >= gi * POOL_GROUP) & (lane < (gi + 1) * POOL_GROUP)
            win_sum = jnp.where(in_group, src[POOL_HIST:total, :], win_sum)
    diff = win_sum * invc_ref[...] - u
    o_pool = _dot(diff, pmap_ref[...]) * pscale_ref[...]

    gates = gates_ref[...].astype(F32)
    merged = (gates[:, 0:d_model] * _dot(o_rwkv, wba_ref[...])
              + gates[:, d_model:2 * d_model] * _dot(o_pool, wbb_ref[...]))
    y_ref[...] = x_ref[...] + _rms(_dot(merged, wout_ref[...]), gpost_ref[...])


def _merge_call(o, bonus, g, u, hist, hist_rows, gates, x2d, seq_len, tm, band, invc, lw):
    n, d_model = x2d.shape
    nt = n // tm
    tiles_per_seq = max(seq_len // tm, 1)
    banded = band is not None
    row = lambda w: pl.BlockSpec((tm, w), lambda i: (i, 0))
    if banded:
        hist_spec = pl.BlockSpec((hist_rows, D_POOL), lambda i: (i, 0))
        invc_spec = _full(invc.shape)
        scratch = []
    else:
        per = tm // hist_rows
        hist_spec = pl.BlockSpec((hist_rows, D_POOL), lambda i: (jnp.maximum(i * per - 1, 0), 0))
        invc_spec = pl.BlockSpec((tm, D_POOL), lambda i: (i % tiles_per_seq, 0))
        scratch = [pltpu.VMEM((hist_rows + tm, D_POOL), F32)] * 2
    consts = [lw['ln_x_w'], lw['ln_x_b'], lw['ones'], lw['pool_map'], lw['pool_scale'],
              lw['w_branch_rwkv'], lw['w_branch_pool'], lw['w_out'], lw['g_mix_post']]
    ins = [o, bonus, g, u, hist, gates, x2d] + ([band] if banded else []) + [invc] + consts
    specs = ([row(D_RWKV)] * 3 + [row(D_POOL), hist_spec, row(2 * d_model), row(d_model)]
             + ([_full(band.shape)] if banded else []) + [invc_spec] + [_const_spec(a) for a in consts])
    return pl.pallas_call(
        functools.partial(_merge_kernel, tm=tm, tiles_per_seq=tiles_per_seq, banded=banded),
        grid=(nt,), in_specs=specs, out_specs=row(d_model),
        out_shape=jax.ShapeDtypeStruct((n, d_model), F32), scratch_shapes=scratch,
        compiler_params=pltpu.CompilerParams(dimension_semantics=("arbitrary",),
                                             vmem_limit_bytes=VMEM_LIMIT),
    )(*[_const_arg(a) for a in ins])


def _ffn_kernel(x_ref, gpre_ref, win_ref, wout_ref, gpost_ref, y_ref):
    x = x_ref[...]
    d_ff = wout_ref.shape[0]
    h = _rms(x, gpre_ref[...]).astype(BF16)
    gu = jnp.dot(h, win_ref[...], preferred_element_type=F32)
    gt = gu[:, 0:d_ff]
    act = gt * _sigmoid(gt) * gu[:, d_ff:2 * d_ff]
    y_ref[...] = x + _rms(_dot(act, wout_ref[...]), gpost_ref[...])


def _ffn_call(x2d, tm, lw):
    n, d_model = x2d.shape
    row = pl.BlockSpec((tm, d_model), lambda i: (i, 0))
    consts = [lw['g_ffn_pre'], lw['w_ffn_in'], lw['w_ffn_out'], lw['g_ffn_post']]
    return pl.pallas_call(
        _ffn_kernel, grid=(n // tm,), in_specs=[row] + [_const_spec(a) for a in consts],
        out_specs=row, out_shape=jax.ShapeDtypeStruct((n, d_model), F32),
        compiler_params=pltpu.CompilerParams(dimension_semantics=("arbitrary",),
                                             vmem_limit_bytes=VMEM_LIMIT),
    )(x2d, *[_const_arg(a) for a in consts])


def _prompt_inv_count(seq_len):
    pos = np.arange(seq_len)[:, None]
    win = np.repeat(np.array(POOL_WINDOWS), POOL_GROUP)[None, :]
    return jnp.asarray(1.0 / np.minimum(pos + 1, win).astype(np.float32), F32)


def _sample_pool_consts(tm, seq_len):
    ns = tm // seq_len
    rs, rt = np.divmod(np.arange(tm), seq_len)
    hs, hj = np.divmod(np.arange(ns * POOL_SLOTS), POOL_SLOTS)
    cs = np.concatenate([hs, rs])[None, :]
    cpos = np.concatenate([hj - 1, POOL_BUF + rt])[None, :]
    dist = (POOL_BUF + rt)[:, None] - cpos
    same = (rs[:, None] == cs) & (cpos >= 0)
    band = np.stack([same & (dist >= 0) & (dist < w) for w in POOL_WINDOWS]).astype(np.float32)
    win = np.repeat(np.array(POOL_WINDOWS), POOL_GROUP)[None, :]
    pos = (SAMPLE_START_POS + rt)[:, None]
    invc = 1.0 / np.minimum(pos + 1, win).astype(np.float32)
    return jnp.asarray(band, BF16), jnp.asarray(invc, F32)


def _block_diag(blocks):
    n = len(blocks)
    rows = []
    for i, blk in enumerate(blocks):
        rows.append(jnp.concatenate(
            [blk if j == i else jnp.zeros((blk.shape[0], blocks[j].shape[1]), blk.dtype) for j in range(n)],
            axis=1))
    return jnp.concatenate(rows, axis=0)


BIG_WEIGHTS = ('w_in', 'w_gate_up', 'w_branch_rwkv', 'w_branch_pool', 'w_out', 'w_ffn_in', 'w_ffn_out')


def _layer_weights(l, w, wb):
    row = lambda a: a.reshape(1, -1).astype(F32)
    lw = {k: (wb[k], l) for k in BIG_WEIGHTS}
    lw.update({
        'g_mix_pre': row(w['norm_mix_pre'][l]), 'g_mix_post': row(w['norm_mix_post'][l]),
        'g_ffn_pre': row(w['norm_ffn_pre'][l]), 'g_ffn_post': row(w['norm_ffn_post'][l]),
        'mu': row(w['mu_shift'][l]), 'decay_bias': row(w['decay_bias'][l]), 'iclr_bias': row(w['iclr_bias'][l]),
        'lora': _block_diag([w['w_decay_up'][l], w['w_iclr_up'][l]]).astype(BF16),
        'k_k': row(w['k_k'][l]), 'k_a': row(w['k_a'][l]), 'r_k': row(w['r_k'][l]),
        'ln_x_w': row(w['ln_x_w'][l]), 'ln_x_b': row(w['ln_x_b'][l]),
        'pool_map': _block_diag([w['pool_map'][l, gi] for gi in range(len(POOL_WINDOWS))]).astype(BF16),
        'pool_scale': row(w['pool_scale'][l]),
        'ones': jnp.asarray(np.kron(np.eye(SUM_TILE // HEAD), np.ones((HEAD, HEAD))), BF16),
    })
    if l > 0:
        pad = LANES - LORA_VRES
        lw['vres_bias'] = row(w['vres_bias'][l - 1])
        lw['w_vres_down'] = jnp.pad(w['w_vres_down'][l - 1], ((0, 0), (0, pad))).astype(BF16)
        lw['w_vres_up'] = jnp.pad(w['w_vres_up'][l - 1], ((0, pad), (0, 0))).astype(BF16)
    return lw


def _from_pairs(h):
    lead = h.shape[:-3]
    n = len(lead)
    h = h.reshape(*lead, N_PAIRS, HEAD, 2, HEAD)
    return h.transpose(*range(n), n, n + 2, n + 3, n + 1).reshape(*lead, N_HEADS, HEAD, HEAD)


def _tile_rows(n_seq, seq_len, target):
    if seq_len >= target:
        return target
    return min(n_seq * seq_len, target)


def kernel(x_prompt, x_sample, state_wkv, state_shift, state_pool, norm_mix_pre, norm_mix_post, norm_ffn_pre, norm_ffn_post, w_in, mu_shift, decay_bias, w_decay_up, iclr_bias, w_iclr_up, w_gate_up, k_k, k_a, r_k, ln_x_w, ln_x_b, vres_bias, w_vres_down, w_vres_up, pool_map, pool_scale, w_branch_rwkv, w_branch_pool, w_out, w_ffn_in, w_ffn_out):
    weights = dict(norm_mix_pre=norm_mix_pre, norm_mix_post=norm_mix_post, norm_ffn_pre=norm_ffn_pre,
                   norm_ffn_post=norm_ffn_post, w_in=w_in, mu_shift=mu_shift, decay_bias=decay_bias,
                   w_decay_up=w_decay_up, iclr_bias=iclr_bias, w_iclr_up=w_iclr_up, w_gate_up=w_gate_up,
                   k_k=k_k, k_a=k_a, r_k=r_k, ln_x_w=ln_x_w, ln_x_b=ln_x_b, vres_bias=vres_bias,
                   w_vres_down=w_vres_down, w_vres_up=w_vres_up, pool_map=pool_map, pool_scale=pool_scale,
                   w_branch_rwkv=w_branch_rwkv, w_branch_pool=w_branch_pool, w_out=w_out,
                   w_ffn_in=w_ffn_in, w_ffn_out=w_ffn_out)
    depth = w_in.shape[0]
    wb = {k: weights[k].astype(BF16) for k in BIG_WEIGHTS}
    bp, tp, d_model = x_prompt.shape
    bs, ts, _ = x_sample.shape
    tm_p = _tile_rows(bp, tp, ROW_TILE)
    tm_s = _tile_rows(bs, ts, ROW_TILE)
    invc_p = _prompt_inv_count(tp)
    band_s, invc_s = _sample_pool_consts(tm_s, ts)

    state_t = state_wkv.transpose(0, 2, 3, 4, 1)
    yp = x_prompt.reshape(bp * tp, d_model)
    ys = x_sample.reshape(bs * ts, d_model)
    vf_p = vf_s = None
    outs = {k: [] for k in ('wkv_p', 'shift_p', 'pool_p', 'wkv_s', 'shift_s', 'pool_s')}
    for l in range(depth):
        lw = _layer_weights(l, weights, wb)

        (r, kp, v, kkn, a, ld, g, bonus, u, gates, last) = _proj_call(yp, tp, tm_p, lw, vf_p, None)
        if l == 0:
            vf_p = v
        o, hout = _wkv_call((r, kp, v, kkn, a, ld), bp, tp)
        x1 = _merge_call(o, bonus, g, u, u, POOL_HIST, gates, yp, tp, 2 * tm_p, None, invc_p, lw)
        yp = _ffn_call(x1, 2 * tm_p, lw)
        outs['wkv_p'].append(_from_pairs(hout))
        tiles = tp // tm_p
        outs['shift_p'].append(last.reshape(bp, tiles, CARRY, D_SHIFT)[:, -1, -1])
        outs['pool_p'].append(u.reshape(bp, tp, D_POOL)[:, -POOL_BUF:])

        first = jnp.zeros((bs, ts, D_SHIFT), F32).at[:, 0].set(state_shift[l]).reshape(bs * ts, D_SHIFT)
        (r, kp, v, kkn, a, ld, g, bonus, u, gates, ps_all) = _proj_call(ys, ts, tm_s, lw, vf_s, first)
        if l == 0:
            vf_s = v
        prev = outs['wkv_s'] if l == depth - 1 else []
        o, hout = _wkv_lanes_call((r, kp, v, kkn, a, ld), bs, ts, state_t, l, prev)
        outs['wkv_s'].append(hout[0] if l < depth - 1 else hout)
        hist = jnp.pad(state_pool[l], ((0, 0), (1, 0), (0, 0))).reshape(bs * POOL_SLOTS, D_POOL)
        x1 = _merge_call(o, bonus, g, u, hist, (tm_s // ts) * POOL_SLOTS, gates, ys, ts, tm_s, band_s, invc_s,
                         lw)
        ys = _ffn_call(x1, tm_s, lw)
        outs['shift_s'].append(ps_all.reshape(bs, ts, D_SHIFT)[:, -1])
        outs['pool_s'].append(jnp.concatenate(
            [state_pool[l], u.reshape(bs, ts, D_POOL)], axis=1)[:, -POOL_BUF:])

    return (yp.reshape(bp, tp, d_model), ys.reshape(bs, ts, d_model),
            jnp.stack(outs['wkv_p']), jnp.stack(outs['shift_p']), jnp.stack(outs['pool_p']),
            outs['wkv_s'][-1].transpose(0, 4, 1, 2, 3), jnp.stack(outs['shift_s']), jnp.stack(outs['pool_s']))
```

```python
import functools

import numpy as np
import jax
import jax.numpy as jnp
from jax.experimental import pallas as pl
from jax.experimental.pallas import tpu as pltpu

F32 = jnp.float32
BF16 = jnp.bfloat16

HEAD = 64
D_RWKV = 768
N_HEADS = D_RWKV // HEAD
PAIR = 2 * HEAD
N_PAIRS = D_RWKV // PAIR
D_POOL = 256
POOL_WINDOWS = (2, 4, 8, 16)
POOL_GROUP = D_POOL // len(POOL_WINDOWS)
POOL_BUF = max(POOL_WINDOWS) - 1
POOL_SLOTS = POOL_BUF + 1
LORA_DECAY = 64
LORA_ICLR = 64
LORA_GATE = 128
LORA_VRES = 32
D_SHIFT = 3 * D_RWKV + LORA_DECAY + LORA_ICLR + LORA_GATE
SAMPLE_START_POS = 16384
RMS_EPS = 1e-6
GN_EPS = 1e-5 * HEAD
DECAY_SCALE = float(np.exp(-0.5))
CHUNK = 64
WKV_SUB = 4
POOL_HIST = 32
CARRY = 8
LANES = 128
LANES_UNROLL = 16
SUM_TILE = 256
ROW_TILE = 256
VMEM_LIMIT = 56 * 1024 * 1024


def _dot(a, b):
    return jnp.dot(a.astype(BF16), b.astype(BF16), preferred_element_type=F32)


def _dot_nt(a, b):
    return jax.lax.dot_general(a.astype(BF16), b.astype(BF16), (((1,), (1,)), ((), ())),
                               preferred_element_type=F32)


def _split(x, parts):
    out = []
    for _ in range(parts - 1):
        hi = x.astype(BF16)
        out.append(hi)
        x = x - hi.astype(F32)
    out.append(x.astype(BF16))
    return out


def _dot_exact_lhs(sel, x, parts):
    acc = None
    for p in _split(x, parts):
        t = jnp.dot(sel, p, preferred_element_type=F32)
        acc = t if acc is None else acc + t
    return acc


def _head_sums(x, ones):
    return jnp.concatenate(
        [_dot(x[:, c:c + SUM_TILE], ones) for c in range(0, x.shape[1], SUM_TILE)], axis=1)


def _prefix_rows(x):
    rows, cols = x.shape
    tiles = rows // CARRY
    y = x.reshape(tiles, CARRY, cols)
    sub = jax.lax.broadcasted_iota(jnp.int32, (1, CARRY, 1), 1)
    d = 1
    while d < CARRY:
        y = y + jnp.where(sub >= d, pltpu.roll(y, d, 1), 0.0)
        d *= 2
    total = y[:, CARRY - 1:CARRY, :]
    run = jnp.zeros((1, 1, cols), F32)
    before = []
    for i in range(tiles):
        before.append(run)
        run = run + total[i:i + 1]
    return (y + jnp.concatenate(before, axis=0)).reshape(rows, cols)


def _sigmoid(x):
    return 1.0 / (1.0 + jnp.exp(-x))


def _rms(x, g):
    return x * jax.lax.rsqrt(jnp.mean(x * x, axis=-1, keepdims=True) + RMS_EPS) * g


def _proj_kernel(*refs, tm, seq_len, has_vres, has_first):
    it = iter(refs)
    x_ref, gpre_ref, win_ref, mu_ref, lora_ref, dbias_ref, ibias_ref, wgate_ref = (next(it) for _ in range(8))
    kk_ref, ka_ref, rk_ref, ones_ref = (next(it) for _ in range(4))
    if has_vres:
        vfirst_ref, vbias_ref, wvdown_ref, wvup_ref = (next(it) for _ in range(4))
    if has_first:
        first_ref = next(it)
    (r_out, kp_out, v_out, kkn_out, a_out, ld_out, g_out, bonus_out, u_out, gates_out,
     last_out, p_scr) = (next(it) for _ in range(12))

    gates_end = D_SHIFT + D_POOL + 2 * x_ref.shape[1]
    i = pl.program_id(0)
    if seq_len >= tm:
        tiles_per_seq = seq_len // tm
        new_seq = (i % tiles_per_seq) == 0
    else:
        new_seq = i == 0

    @pl.when(new_seq)
    def _():
        p_scr[0:CARRY, :] = jnp.zeros((CARRY, D_SHIFT), F32)

    xn = _rms(x_ref[...], gpre_ref[...]).astype(BF16)
    p_scr[CARRY:CARRY + tm, :] = jnp.dot(xn, win_ref[:, 0:D_SHIFT], preferred_element_type=F32)
    u_out[...] = jnp.dot(xn, win_ref[:, D_SHIFT:D_SHIFT + D_POOL], preferred_element_type=F32)
    gates_out[...] = _sigmoid(jnp.dot(xn, win_ref[:, D_SHIFT + D_POOL:gates_end],
                                      preferred_element_type=F32)).astype(BF16)

    if has_first:
        row = jax.lax.broadcasted_iota(jnp.int32, (tm, 1), 0)
        seq_start = jax.lax.rem(row, seq_len) == 0

    def mixed(c0, c1):
        ps = p_scr[CARRY:CARRY + tm, c0:c1]
        prev = p_scr[CARRY - 1:CARRY - 1 + tm, c0:c1]
        if has_first:
            prev = jnp.where(seq_start, first_ref[:, c0:c1], prev)
        return ps + (prev - ps) * mu_ref[:, c0:c1]

    o1, o2, o3 = D_RWKV, 2 * D_RWKV, 3 * D_RWKV
    o5 = o3 + LORA_DECAY + LORA_ICLR
    r = mixed(0, o1)
    k = mixed(o1, o2)
    v = mixed(o2, o3)
    xwa = mixed(o3, o5)
    xg = mixed(o5, D_SHIFT)

    lane = jax.lax.broadcasted_iota(jnp.int32, (1, LANES), 1)
    lora_in = jnp.where(lane < LORA_DECAY, jnp.tanh(xwa), xwa)
    lora = _dot(lora_in, lora_ref[...])
    ld_out[...] = -DECAY_SCALE * _sigmoid(dbias_ref[...] + lora[:, 0:D_RWKV])
    a = _sigmoid(ibias_ref[...] + lora[:, D_RWKV:2 * D_RWKV])
    a_out[...] = a
    g_out[...] = _dot(_sigmoid(xg), wgate_ref[...]).astype(BF16)

    if has_vres:
        vdown = jnp.dot(xn, wvdown_ref[...], preferred_element_type=F32)
        v = v + (vfirst_ref[...] - v) * _sigmoid(vbias_ref[...] + _dot(vdown, wvup_ref[...]))
    v_out[...] = v

    kk = k * kk_ref[...]
    norm = jnp.sqrt(_head_sums(kk * kk, ones_ref[...]))
    kkn_out[...] = kk / jnp.maximum(norm, 1e-12)
    kp = k * (1.0 + (a - 1.0) * ka_ref[...])
    kp_out[...] = kp
    r_out[...] = r
    bonus_out[...] = (_head_sums(r * kp * rk_ref[...], ones_ref[...]) * v).astype(BF16)

    if has_first:
        last_out[...] = p_scr[CARRY:CARRY + tm, :]
    else:
        tail = p_scr[tm:tm + CARRY, :]
        last_out[...] = tail
        p_scr[0:CARRY, :] = tail


def _full(shape):
    nd = len(shape)
    return pl.BlockSpec(shape, lambda *_: (0,) * nd, pipeline_mode=pl.Buffered(1))


def _const_spec(a):
    if isinstance(a, tuple):
        arr, l = a
        return pl.BlockSpec((None,) + arr.shape[1:], lambda *_: (l,) + (0,) * (arr.ndim - 1),
                            pipeline_mode=pl.Buffered(1))
    return _full(a.shape)


def _const_arg(a):
    return a[0] if isinstance(a, tuple) else a


def _proj_call(x2d, seq_len, tm, lw, vfirst, first):
    n = x2d.shape[0]
    d_model = x2d.shape[1]
    has_vres = vfirst is not None
    has_first = first is not None
    nt = n // tm
    row = lambda w: pl.BlockSpec((tm, w), lambda i: (i, 0))
    ins = [x2d, lw['g_mix_pre'], lw['w_in'], lw['mu'], lw['lora'], lw['decay_bias'], lw['iclr_bias'],
           lw['w_gate_up'], lw['k_k'], lw['k_a'], lw['r_k'], lw['ones']]
    specs = [row(d_model)] + [_const_spec(a) for a in ins[1:]]
    if has_vres:
        ins += [vfirst, lw['vres_bias'], lw['w_vres_down'], lw['w_vres_up']]
        specs += [row(D_RWKV)] + [_const_spec(lw[k]) for k in ('vres_bias', 'w_vres_down', 'w_vres_up')]
    if has_first:
        ins.append(first)
        specs.append(row(D_SHIFT))
    sds = lambda w, dt=F32: jax.ShapeDtypeStruct((n, w), dt)
    last_rows = tm if has_first else CARRY
    out_shape = [sds(D_RWKV)] * 6 + [sds(D_RWKV, BF16)] * 2 + [sds(D_POOL), sds(2 * d_model, BF16),
                                     jax.ShapeDtypeStruct((nt * last_rows, D_SHIFT), F32)]
    out_specs = [row(D_RWKV)] * 8 + [row(D_POOL), row(2 * d_model),
                                     pl.BlockSpec((last_rows, D_SHIFT), lambda i: (i, 0))]
    return pl.pallas_call(
        functools.partial(_proj_kernel, tm=tm, seq_len=seq_len, has_vres=has_vres, has_first=has_first),
        grid=(nt,), in_specs=specs, out_specs=out_specs, out_shape=out_shape,
        scratch_shapes=[pltpu.VMEM((tm + CARRY, D_SHIFT), F32)],
        compiler_params=pltpu.CompilerParams(dimension_semantics=("arbitrary",),
                                             vmem_limit_bytes=VMEM_LIMIT),
    )(*[_const_arg(a) for a in ins])


def _wkv_kernel(r_ref, kp_ref, v_ref, kk_ref, a_ref, ld_ref, o_ref, hout_ref, h_scr, *, n_sub):
    c_idx = pl.program_id(1)
    n_steps = pl.num_programs(1)
    C = CHUNK

    lane = jax.lax.broadcasted_iota(jnp.int32, (1, PAIR), 1)
    left = lane < HEAD
    ri = jax.lax.broadcasted_iota(jnp.int32, (PAIR, PAIR), 0)
    ci = jax.lax.broadcasted_iota(jnp.int32, (PAIR, PAIR), 1)
    same = (ri >= HEAD) == (ci >= HEAD)
    strict = same & (ci < ri)
    incl = same & (ci <= ri)
    eye = ri == ci
    eye_f = jnp.where(eye, 1.0, 0.0).astype(F32)

    def stack(x):
        return jnp.concatenate([jnp.where(left, x, 0.0), jnp.where(left, 0.0, x)], axis=0)

    def twice(x):
        return jnp.concatenate([x, x], axis=0)

    @pl.when(c_idx == 0)
    def _():
        h_scr[...] = jnp.zeros(h_scr.shape, F32)


    a_t, r_t, b_t, k_t, b_h, k_h, v_st, w_last = ([] for _ in range(8))
    for j in range(n_sub):
        rows = slice(j * C, (j + 1) * C)
        ld = ld_ref[rows, :]
        cum = _prefix_rows(ld)
        clast = cum[C - 1:C, :]
        e_cum = jnp.exp(cum)
        e_prev = jnp.exp(cum - ld)
        e_neg = jnp.exp(-cum)
        e_rem = jnp.exp(clast - cum)
        wl = jnp.exp(clast)
        kk = kk_ref[rows, :]
        b = kk * a_ref[rows, :]
        kp = kp_ref[rows, :]
        full = (-(kk * e_prev), r_ref[rows, :] * e_cum, b * e_neg, kp * e_neg, b * e_rem, kp * e_rem)
        v = v_ref[rows, :]
        for p in range(N_PAIRS):
            sl = slice(p * PAIR, (p + 1) * PAIR)
            for dst, src in zip((a_t, r_t, b_t, k_t, b_h, k_h), full):
                dst.append(src[:, sl])
            v_st.append(stack(v[:, sl]))
            w_last.append(wl[:, sl])

    units = range(n_sub * N_PAIRS)
    sc = [_dot_nt(jnp.concatenate([a_t[q], r_t[q]], axis=0),
                  jnp.concatenate([stack(b_t[q]), stack(k_t[q])], axis=0)) for q in units]
    n_ab = [jnp.where(strict, twice(s[0:C, 0:PAIR]), 0.0) for s in sc]
    n_ak = [jnp.where(strict, twice(s[0:C, PAIR:2 * PAIR]), 0.0) for s in sc]
    n_rb = [jnp.where(incl, twice(s[C:2 * C, 0:PAIR]), 0.0) for s in sc]
    n_rk = [jnp.where(incl, twice(s[C:2 * C, PAIR:2 * PAIR]), 0.0) for s in sc]

    inv = [eye_f + n for n in n_ab]
    power = n_ab
    y = [_dot(n_ak[q], v_st[q]) for q in units]
    for _ in range(int(np.log2(C)) - 1):
        power = [_dot(m, m) for m in power]
        inv = [inv[q] + _dot(inv[q], power[q]) for q in units]

    au = [_dot(inv[q], jnp.concatenate([stack(a_t[q]), y[q]], axis=1)) for q in units]
    zero = jnp.zeros((PAIR, PAIR), F32)
    big = []
    for q in units:
        rhs = jnp.concatenate([au[q], jnp.concatenate([zero, v_st[q]], axis=1)], axis=0)
        lhs = jnp.concatenate(
            [jnp.concatenate([n_rb[q], n_rk[q]], axis=1),
             jnp.concatenate([stack(b_h[q]).T, stack(k_h[q]).T], axis=1)], axis=0)
        big.append(_dot(lhs, rhs))

    h = [h_scr[p] for p in range(N_PAIRS)]
    for j in range(n_sub):
        rows = slice(j * C, (j + 1) * C)
        qs = [j * N_PAIRS + p for p in range(N_PAIRS)]
        sd = []
        for p, q in enumerate(qs):
            r_hat = stack(r_t[q]) + big[q][0:PAIR, 0:PAIR]
            m_low = big[q][PAIR:2 * PAIR, 0:PAIR]
            sd.append(_dot(jnp.concatenate([r_hat, m_low], axis=0), h[p]))
        for p, q in enumerate(qs):
            o_st = sd[p][0:PAIR] + big[q][0:PAIR, PAIR:2 * PAIR]
            o_ref[rows, p * PAIR:(p + 1) * PAIR] = o_st[0:C] + o_st[C:2 * C]
            w_col = jnp.sum(jnp.where(eye, jnp.broadcast_to(w_last[q], (PAIR, PAIR)), 0.0),
                            axis=1, keepdims=True)
            h[p] = w_col * h[p] + sd[p][PAIR:2 * PAIR] + big[q][PAIR:2 * PAIR, PAIR:2 * PAIR]
    for p in range(N_PAIRS):
        h_scr[p] = h[p]

    @pl.when(c_idx == n_steps - 1)
    def _():
        for p in range(N_PAIRS):
            hout_ref[0, p] = h_scr[p, 0:HEAD, :] + h_scr[p, HEAD:PAIR, :]


def _wkv_call(ops, n_seq, seq_len):
    n_sub = WKV_SUB if seq_len % (WKV_SUB * CHUNK) == 0 else 1
    rows = n_sub * CHUNK
    assert seq_len % rows == 0
    ns = seq_len // rows
    blk = pl.BlockSpec((rows, D_RWKV), lambda b, c: (b * ns + c, 0))
    hspec = pl.BlockSpec((1, N_PAIRS, HEAD, PAIR), lambda b, c: (b, 0, 0, 0))
    return pl.pallas_call(
        functools.partial(_wkv_kernel, n_sub=n_sub),
        grid=(n_seq, ns), in_specs=[blk] * 6, out_specs=[blk, hspec],
        out_shape=[jax.ShapeDtypeStruct((n_seq * seq_len, D_RWKV), F32),
                   jax.ShapeDtypeStruct((n_seq, N_PAIRS, HEAD, PAIR), F32)],
        scratch_shapes=[pltpu.VMEM((N_PAIRS, PAIR, PAIR), F32)],
        compiler_params=pltpu.CompilerParams(dimension_semantics=("arbitrary", "arbitrary"),
                                             vmem_limit_bytes=VMEM_LIMIT),
    )(*ops)


def _wkv_lanes_kernel(*refs, seq_len, n_prev):
    r_ref, kp_ref, v_ref, kk_ref, a_ref, ld_ref, s_ref = refs[:7]
    prev_refs = refs[7:7 + n_prev]
    o_ref, stacked_ref, op_scr, o_scr = refs[7 + n_prev:]
    for i, prev in enumerate(prev_refs):
        stacked_ref[i] = prev[...]
    sout_ref = stacked_ref.at[n_prev]
    T = seq_len
    nb = r_ref.shape[0] // T
    tok = lambda ref, t: ref[pl.ds(t, nb, stride=T), :]

    for t in range(T):
        kk = tok(kk_ref, t)
        op_scr[0, t] = (-kk).T
        op_scr[1, t] = jnp.exp(tok(ld_ref, t)).T
        op_scr[2, t] = (kk * tok(a_ref, t)).T
        op_scr[3, t] = tok(kp_ref, t).T
        op_scr[4, t] = tok(r_ref, t).T
        op_scr[5, t] = tok(v_ref, t).T

    def all_keys(x):
        y = jnp.sum(x.reshape(HEAD // CARRY, CARRY, nb), axis=0)
        for i in range(1, int(np.log2(CARRY)) + 1):
            y = y + pltpu.roll(y, CARRY >> i, 0)
        return y

    def over_keys(y, x):
        return (y[None] * x.reshape(HEAD // CARRY, CARRY, nb)).reshape(HEAD, nb)

    for hh in range(2):
        keys = slice(hh * HEAD, (hh + 1) * HEAD)

        def value_row(vi, carry, hh=hh, keys=keys):
            s = s_ref[hh, vi]
            row = hh * HEAD + vi
            for t in range(T):
                sa = all_keys(s * op_scr[0, t, keys, :])
                v_row = jnp.broadcast_to(op_scr[5, t, pl.ds(row, 1), :], (CARRY, nb))
                s = (s * op_scr[1, t, keys, :] + over_keys(sa, op_scr[2, t, keys, :])
                     + over_keys(v_row, op_scr[3, t, keys, :]))
                o_scr[t, pl.ds(row, 1), :] = all_keys(s * op_scr[4, t, keys, :])[0:1]
            sout_ref[hh, vi] = s
            return carry

        jax.lax.fori_loop(0, HEAD, value_row, 0, unroll=LANES_UNROLL)

    for t in range(T):
        o_ref[pl.ds(t, nb, stride=T), :] = o_scr[t].T


def _wkv_lanes_call(ops, n_seq, seq_len, state_t, layer, prev_states):
    assert n_seq == LANES
    n_prev = len(prev_states)
    rows = n_seq * seq_len
    blk = pl.BlockSpec((rows, PAIR), lambda p: (0, p))
    in_state = pl.BlockSpec((None, 2, HEAD, HEAD, n_seq), lambda p: (layer, p, 0, 0, 0))
    prev_spec = pl.BlockSpec((2, HEAD, HEAD, n_seq), lambda p: (p, 0, 0, 0))
    out_state = pl.BlockSpec((n_prev + 1, 2, HEAD, HEAD, n_seq), lambda p: (0, p, 0, 0, 0))
    return pl.pallas_call(
        functools.partial(_wkv_lanes_kernel, seq_len=seq_len, n_prev=n_prev),
        grid=(N_PAIRS,), in_specs=[blk] * 6 + [in_state] + [prev_spec] * n_prev, out_specs=[blk, out_state],
        out_shape=[jax.ShapeDtypeStruct((rows, D_RWKV), F32),
                   jax.ShapeDtypeStruct((n_prev + 1,) + state_t.shape[1:], F32)],
        scratch_shapes=[pltpu.VMEM((6, seq_len, PAIR, n_seq), F32), pltpu.VMEM((seq_len, PAIR, n_seq), F32)],
        compiler_params=pltpu.CompilerParams(dimension_semantics=("arbitrary",),
                                             vmem_limit_bytes=VMEM_LIMIT),
    )(*ops, state_t, *prev_states)


def _merge_kernel(*refs, tm, tiles_per_seq, banded, with_ffn):
    it = iter(refs)
    o_ref, bonus_ref, g_ref, u_ref, hist_ref, gates_ref, x_ref = (next(it) for _ in range(7))
    if banded:
        band_ref = next(it)
    (invc_ref, lnw_ref, lnb_ref, ones_ref, pmap_ref, pscale_ref, wba_ref, wbb_ref, wout_ref,
     gpost_ref) = (next(it) for _ in range(10))
    if with_ffn:
        ffn_refs = [next(it) for _ in range(4)]
    y_ref = next(it)
    if not banded:
        s_a, s_b = next(it), next(it)
    d_model = x_ref.shape[1]
    inv_h = 1.0 / HEAD

    o = o_ref[...]
    mean = _head_sums(o, ones_ref[...]) * inv_h
    cen = o - mean
    var = _head_sums(cen * cen, ones_ref[...]) * inv_h
    o_n = cen * jax.lax.rsqrt(var + GN_EPS) * lnw_ref[...] + lnb_ref[...]
    o_rwkv = (o_n + bonus_ref[...].astype(F32)) * g_ref[...].astype(F32)

    u = u_ref[...]
    lane = jax.lax.broadcasted_iota(jnp.int32, (1, D_POOL), 1)
    win_sum = jnp.zeros((tm, D_POOL), F32)
    if banded:
        full = jnp.concatenate([hist_ref[...], u], axis=0)
        for gi in range(len(POOL_WINDOWS)):
            s = _dot_exact_lhs(band_ref[gi], full, 2)
            in_group = (lane >= gi * POOL_GROUP) & (lane < (gi + 1) * POOL_GROUP)
            win_sum = jnp.where(in_group, s, win_sum)
    else:
        keep = jnp.where((pl.program_id(0) % tiles_per_seq) == 0, 0.0, 1.0)
        s_a[0:POOL_HIST, :] = hist_ref[...] * keep
        s_a[POOL_HIST:POOL_HIST + tm, :] = u
        src, dst = s_a, s_b
        total = POOL_HIST + tm
        w = 1
        for gi, win in enumerate(POOL_WINDOWS):
            while w < win:
                lo = CARRY * int(np.log2(2 * w))
                dst[lo:total, :] = src[lo:total, :] + src[lo - w:total - w, :]
                src, dst = dst, src
                w *= 2
            in_group = (lane >= gi * POOL_GROUP) & (lane < (gi + 1) * POOL_GROUP)
            win_sum = jnp.where(in_group, src[POOL_HIST:total, :], win_sum)
    diff = win_sum * invc_ref[...] - u
    o_pool = _dot(diff, pmap_ref[...]) * pscale_ref[...]

    gates = gates_ref[...].astype(F32)
    merged = (gates[:, 0:d_model] * _dot(o_rwkv, wba_ref[...])
              + gates[:, d_model:2 * d_model] * _dot(o_pool, wbb_ref[...]))
    x1 = x_ref[...] + _rms(_dot(merged, wout_ref[...]), gpost_ref[...])
    y_ref[...] = _ffn_value(x1, *ffn_refs) if with_ffn else x1


def _merge_call(o, bonus, g, u, hist, hist_rows, gates, x2d, seq_len, tm, band, invc, lw, with_ffn=False):
    n, d_model = x2d.shape
    nt = n // tm
    tiles_per_seq = max(seq_len // tm, 1)
    banded = band is not None
    row = lambda w: pl.BlockSpec((tm, w), lambda i: (i, 0))
    if banded:
        hist_spec = pl.BlockSpec((hist_rows, D_POOL), lambda i: (i, 0))
        invc_spec = _full(invc.shape)
        scratch = []
    else:
        per = tm // hist_rows
        hist_spec = pl.BlockSpec((hist_rows, D_POOL), lambda i: (jnp.maximum(i * per - 1, 0), 0))
        invc_spec = pl.BlockSpec((tm, D_POOL), lambda i: (i % tiles_per_seq, 0))
        scratch = [pltpu.VMEM((hist_rows + tm, D_POOL), F32)] * 2
    consts = [lw['ln_x_w'], lw['ln_x_b'], lw['ones'], lw['pool_map'], lw['pool_scale'],
              lw['w_branch_rwkv'], lw['w_branch_pool'], lw['w_out'], lw['g_mix_post']]
    if with_ffn:
        consts += [lw['g_ffn_pre'], lw['w_ffn_in'], lw['w_ffn_out'], lw['g_ffn_post']]
    ins = [o, bonus, g, u, hist, gates, x2d] + ([band] if banded else []) + [invc] + consts
    specs = ([row(D_RWKV)] * 3 + [row(D_POOL), hist_spec, row(2 * d_model), row(d_model)]
             + ([_full(band.shape)] if banded else []) + [invc_spec] + [_const_spec(a) for a in consts])
    return pl.pallas_call(
        functools.partial(_merge_kernel, tm=tm, tiles_per_seq=tiles_per_seq, banded=banded, with_ffn=with_ffn),
        grid=(nt,), in_specs=specs, out_specs=row(d_model),
        out_shape=jax.ShapeDtypeStruct((n, d_model), F32), scratch_shapes=scratch,
        compiler_params=pltpu.CompilerParams(dimension_semantics=("arbitrary",),
                                             vmem_limit_bytes=VMEM_LIMIT),
    )(*[_const_arg(a) for a in ins])


def _ffn_value(x, gpre_ref, win_ref, wout_ref, gpost_ref):
    d_ff = wout_ref.shape[0]
    h = _rms(x, gpre_ref[...]).astype(BF16)
    gu = jnp.dot(h, win_ref[...], preferred_element_type=F32)
    gt = gu[:, 0:d_ff]
    act = gt * _sigmoid(gt) * gu[:, d_ff:2 * d_ff]
    return x + _rms(_dot(act, wout_ref[...]), gpost_ref[...])


def _ffn_kernel(x_ref, gpre_ref, win_ref, wout_ref, gpost_ref, y_ref):
    y_ref[...] = _ffn_value(x_ref[...], gpre_ref, win_ref, wout_ref, gpost_ref)


def _ffn_call(x2d, tm, lw):
    n, d_model = x2d.shape
    row = pl.BlockSpec((tm, d_model), lambda i: (i, 0))
    consts = [lw['g_ffn_pre'], lw['w_ffn_in'], lw['w_ffn_out'], lw['g_ffn_post']]
    return pl.pallas_call(
        _ffn_kernel, grid=(n // tm,), in_specs=[row] + [_const_spec(a) for a in consts],
        out_specs=row, out_shape=jax.ShapeDtypeStruct((n, d_model), F32),
        compiler_params=pltpu.CompilerParams(dimension_semantics=("arbitrary",),
                                             vmem_limit_bytes=VMEM_LIMIT),
    )(x2d, *[_const_arg(a) for a in consts])


def _prompt_inv_count(seq_len):
    pos = np.arange(seq_len)[:, None]
    win = np.repeat(np.array(POOL_WINDOWS), POOL_GROUP)[None, :]
    return jnp.asarray(1.0 / np.minimum(pos + 1, win).astype(np.float32), F32)


def _sample_pool_consts(tm, seq_len):
    ns = tm // seq_len
    rs, rt = np.divmod(np.arange(tm), seq_len)
    hs, hj = np.divmod(np.arange(ns * POOL_SLOTS), POOL_SLOTS)
    cs = np.concatenate([hs, rs])[None, :]
    cpos = np.concatenate([hj - 1, POOL_BUF + rt])[None, :]
    dist = (POOL_BUF + rt)[:, None] - cpos
    same = (rs[:, None] == cs) & (cpos >= 0)
    band = np.stack([same & (dist >= 0) & (dist < w) for w in POOL_WINDOWS]).astype(np.float32)
    win = np.repeat(np.array(POOL_WINDOWS), POOL_GROUP)[None, :]
    pos = (SAMPLE_START_POS + rt)[:, None]
    invc = 1.0 / np.minimum(pos + 1, win).astype(np.float32)
    return jnp.asarray(band, BF16), jnp.asarray(invc, F32)


def _block_diag(blocks):
    n = len(blocks)
    rows = []
    for i, blk in enumerate(blocks):
        rows.append(jnp.concatenate(
            [blk if j == i else jnp.zeros((blk.shape[0], blocks[j].shape[1]), blk.dtype) for j in range(n)],
            axis=1))
    return jnp.concatenate(rows, axis=0)


BIG_WEIGHTS = ('w_in', 'w_gate_up', 'w_branch_rwkv', 'w_branch_pool', 'w_out', 'w_ffn_in', 'w_ffn_out')


def _layer_weights(l, w, wb):
    row = lambda a: a.reshape(1, -1).astype(F32)
    lw = {k: (wb[k], l) for k in BIG_WEIGHTS}
    lw.update({
        'g_mix_pre': row(w['norm_mix_pre'][l]), 'g_mix_post': row(w['norm_mix_post'][l]),
        'g_ffn_pre': row(w['norm_ffn_pre'][l]), 'g_ffn_post': row(w['norm_ffn_post'][l]),
        'mu': row(w['mu_shift'][l]), 'decay_bias': row(w['decay_bias'][l]), 'iclr_bias': row(w['iclr_bias'][l]),
        'lora': _block_diag([w['w_decay_up'][l], w['w_iclr_up'][l]]).astype(BF16),
        'k_k': row(w['k_k'][l]), 'k_a': row(w['k_a'][l]), 'r_k': row(w['r_k'][l]),
        'ln_x_w': row(w['ln_x_w'][l]), 'ln_x_b': row(w['ln_x_b'][l]),
        'pool_map': _block_diag([w['pool_map'][l, gi] for gi in range(len(POOL_WINDOWS))]).astype(BF16),
        'pool_scale': row(w['pool_scale'][l]),
        'ones': jnp.asarray(np.kron(np.eye(SUM_TILE // HEAD), np.ones((HEAD, HEAD))), BF16),
    })
    if l > 0:
        pad = LANES - LORA_VRES
        lw['vres_bias'] = row(w['vres_bias'][l - 1])
        lw['w_vres_down'] = jnp.pad(w['w_vres_down'][l - 1], ((0, 0), (0, pad))).astype(BF16)
        lw['w_vres_up'] = jnp.pad(w['w_vres_up'][l - 1], ((0, pad), (0, 0))).astype(BF16)
    return lw


def _from_pairs(h):
    lead = h.shape[:-3]
    n = len(lead)
    h = h.reshape(*lead, N_PAIRS, HEAD, 2, HEAD)
    return h.transpose(*range(n), n, n + 2, n + 3, n + 1).reshape(*lead, N_HEADS, HEAD, HEAD)


def _tile_rows(n_seq, seq_len, target):
    if seq_len >= target:
        return target
    return min(n_seq * seq_len, target)


def kernel(x_prompt, x_sample, state_wkv, state_shift, state_pool, norm_mix_pre, norm_mix_post, norm_ffn_pre, norm_ffn_post, w_in, mu_shift, decay_bias, w_decay_up, iclr_bias, w_iclr_up, w_gate_up, k_k, k_a, r_k, ln_x_w, ln_x_b, vres_bias, w_vres_down, w_vres_up, pool_map, pool_scale, w_branch_rwkv, w_branch_pool, w_out, w_ffn_in, w_ffn_out):
    weights = dict(norm_mix_pre=norm_mix_pre, norm_mix_post=norm_mix_post, norm_ffn_pre=norm_ffn_pre,
                   norm_ffn_post=norm_ffn_post, w_in=w_in, mu_shift=mu_shift, decay_bias=decay_bias,
                   w_decay_up=w_decay_up, iclr_bias=iclr_bias, w_iclr_up=w_iclr_up, w_gate_up=w_gate_up,
                   k_k=k_k, k_a=k_a, r_k=r_k, ln_x_w=ln_x_w, ln_x_b=ln_x_b, vres_bias=vres_bias,
                   w_vres_down=w_vres_down, w_vres_up=w_vres_up, pool_map=pool_map, pool_scale=pool_scale,
                   w_branch_rwkv=w_branch_rwkv, w_branch_pool=w_branch_pool, w_out=w_out,
                   w_ffn_in=w_ffn_in, w_ffn_out=w_ffn_out)
    depth = w_in.shape[0]
    wb = {k: weights[k].astype(BF16) for k in BIG_WEIGHTS}
    bp, tp, d_model = x_prompt.shape
    bs, ts, _ = x_sample.shape
    tm_p = _tile_rows(bp, tp, ROW_TILE)
    tm_s = _tile_rows(bs, ts, ROW_TILE)
    invc_p = _prompt_inv_count(tp)
    band_s, invc_s = _sample_pool_consts(tm_s, ts)

    state_t = state_wkv.transpose(0, 2, 3, 4, 1)
    yp = x_prompt.reshape(bp * tp, d_model)
    ys = x_sample.reshape(bs * ts, d_model)
    vf_p = vf_s = None
    outs = {k: [] for k in ('wkv_p', 'shift_p', 'pool_p', 'wkv_s', 'shift_s', 'pool_s')}
    for l in range(depth):
        lw = _layer_weights(l, weights, wb)

        (r, kp, v, kkn, a, ld, g, bonus, u, gates, last) = _proj_call(yp, tp, tm_p, lw, vf_p, None)
        if l == 0:
            vf_p = v
        o, hout = _wkv_call((r, kp, v, kkn, a, ld), bp, tp)
        x1 = _merge_call(o, bonus, g, u, u, POOL_HIST, gates, yp, tp, 2 * tm_p, None, invc_p, lw)
        yp = _ffn_call(x1, 2 * tm_p, lw)
        outs['wkv_p'].append(_from_pairs(hout))
        tiles = tp // tm_p
        outs['shift_p'].append(last.reshape(bp, tiles, CARRY, D_SHIFT)[:, -1, -1])
        outs['pool_p'].append(u.reshape(bp, tp, D_POOL)[:, -POOL_BUF:])

        first = jnp.zeros((bs, ts, D_SHIFT), F32).at[:, 0].set(state_shift[l]).reshape(bs * ts, D_SHIFT)
        (r, kp, v, kkn, a, ld, g, bonus, u, gates, ps_all) = _proj_call(ys, ts, tm_s, lw, vf_s, first)
        if l == 0:
            vf_s = v
        prev = outs['wkv_s'] if l == depth - 1 else []
        o, hout = _wkv_lanes_call((r, kp, v, kkn, a, ld), bs, ts, state_t, l, prev)
        outs['wkv_s'].append(hout[0] if l < depth - 1 else hout)
        hist = jnp.pad(state_pool[l], ((0, 0), (1, 0), (0, 0))).reshape(bs * POOL_SLOTS, D_POOL)
        ys = _merge_call(o, bonus, g, u, hist, (tm_s // ts) * POOL_SLOTS, gates, ys, ts, tm_s, band_s, invc_s,
                         lw, with_ffn=True)
        outs['shift_s'].append(ps_all.reshape(bs, ts, D_SHIFT)[:, -1])
        outs['pool_s'].append(jnp.concatenate(
            [state_pool[l], u.reshape(bs, ts, D_POOL)], axis=1)[:, -POOL_BUF:])

    return (yp.reshape(bp, tp, d_model), ys.reshape(bs, ts, d_model),
            jnp.stack(outs['wkv_p']), jnp.stack(outs['shift_p']), jnp.stack(outs['pool_p']),
            outs['wkv_s'][-1].transpose(0, 4, 1, 2, 3), jnp.stack(outs['shift_s']), jnp.stack(outs['pool_s']))
```

```python
import functools

import numpy as np
import jax
import jax.numpy as jnp
from jax.experimental import pallas as pl
from jax.experimental.pallas import tpu as pltpu

F32 = jnp.float32
BF16 = jnp.bfloat16

HEAD = 64
D_RWKV = 768
N_HEADS = D_RWKV // HEAD
PAIR = 2 * HEAD
N_PAIRS = D_RWKV // PAIR
D_POOL = 256
POOL_WINDOWS = (2, 4, 8, 16)
POOL_GROUP = D_POOL // len(POOL_WINDOWS)
POOL_BUF = max(POOL_WINDOWS) - 1
POOL_SLOTS = POOL_BUF + 1
LORA_DECAY = 64
LORA_ICLR = 64
LORA_GATE = 128
LORA_VRES = 32
D_SHIFT = 3 * D_RWKV + LORA_DECAY + LORA_ICLR + LORA_GATE
SAMPLE_START_POS = 16384
RMS_EPS = 1e-6
GN_EPS = 1e-5 * HEAD
DECAY_SCALE = float(np.exp(-0.5))
CHUNK = 64
WKV_SUB = 4
POOL_HIST = 32
CARRY = 8
LANES = 128
LANES_UNROLL = 16
SUM_TILE = 256
ROW_TILE = 256
VMEM_LIMIT = 56 * 1024 * 1024


def _dot(a, b):
    return jnp.dot(a.astype(BF16), b.astype(BF16), preferred_element_type=F32)


def _dot_nt(a, b):
    return jax.lax.dot_general(a.astype(BF16), b.astype(BF16), (((1,), (1,)), ((), ())),
                               preferred_element_type=F32)


def _split(x, parts):
    out = []
    for _ in range(parts - 1):
        hi = x.astype(BF16)
        out.append(hi)
        x = x - hi.astype(F32)
    out.append(x.astype(BF16))
    return out


def _dot_exact_lhs(sel, x, parts):
    acc = None
    for p in _split(x, parts):
        t = jnp.dot(sel, p, preferred_element_type=F32)
        acc = t if acc is None else acc + t
    return acc


def _head_sums(x, ones):
    return jnp.concatenate(
        [_dot(x[:, c:c + SUM_TILE], ones) for c in range(0, x.shape[1], SUM_TILE)], axis=1)


def _prefix_rows(x):
    rows, cols = x.shape
    tiles = rows // CARRY
    y = x.reshape(tiles, CARRY, cols)
    sub = jax.lax.broadcasted_iota(jnp.int32, (1, CARRY, 1), 1)
    d = 1
    while d < CARRY:
        y = y + jnp.where(sub >= d, pltpu.roll(y, d, 1), 0.0)
        d *= 2
    total = y[:, CARRY - 1:CARRY, :]
    run = jnp.zeros((1, 1, cols), F32)
    before = []
    for i in range(tiles):
        before.append(run)
        run = run + total[i:i + 1]
    return (y + jnp.concatenate(before, axis=0)).reshape(rows, cols)


def _sigmoid(x):
    return 1.0 / (1.0 + jnp.exp(-x))


def _rms(x, g):
    return x * jax.lax.rsqrt(jnp.mean(x * x, axis=-1, keepdims=True) + RMS_EPS) * g


def _proj_kernel(*refs, tm, seq_len, has_vres, has_first):
    it = iter(refs)
    x_ref, gpre_ref, win_ref, mu_ref, lora_ref, dbias_ref, ibias_ref, wgate_ref = (next(it) for _ in range(8))
    kk_ref, ka_ref, rk_ref, ones_ref = (next(it) for _ in range(4))
    if has_vres:
        vfirst_ref, vbias_ref, wvdown_ref, wvup_ref = (next(it) for _ in range(4))
    if has_first:
        first_ref = next(it)
    (r_out, kp_out, v_out, kkn_out, a_out, ld_out, g_out, bonus_out, u_out, gates_out,
     last_out, p_scr) = (next(it) for _ in range(12))

    gates_end = D_SHIFT + D_POOL + 2 * x_ref.shape[1]
    i = pl.program_id(0)
    if seq_len >= tm:
        tiles_per_seq = seq_len // tm
        new_seq = (i % tiles_per_seq) == 0
    else:
        new_seq = i == 0

    @pl.when(new_seq)
    def _():
        p_scr[0:CARRY, :] = jnp.zeros((CARRY, D_SHIFT), F32)

    xn = _rms(x_ref[...], gpre_ref[...]).astype(BF16)
    p_scr[CARRY:CARRY + tm, :] = jnp.dot(xn, win_ref[:, 0:D_SHIFT], preferred_element_type=F32)
    u_out[...] = jnp.dot(xn, win_ref[:, D_SHIFT:D_SHIFT + D_POOL], preferred_element_type=F32)
    gates_out[...] = _sigmoid(jnp.dot(xn, win_ref[:, D_SHIFT + D_POOL:gates_end],
                                      preferred_element_type=F32)).astype(BF16)

    if has_first:
        row = jax.lax.broadcasted_iota(jnp.int32, (tm, 1), 0)
        seq_start = jax.lax.rem(row, seq_len) == 0

    def mixed(c0, c1):
        ps = p_scr[CARRY:CARRY + tm, c0:c1]
        prev = p_scr[CARRY - 1:CARRY - 1 + tm, c0:c1]
        if has_first:
            prev = jnp.where(seq_start, first_ref[:, c0:c1], prev)
        return ps + (prev - ps) * mu_ref[:, c0:c1]

    o1, o2, o3 = D_RWKV, 2 * D_RWKV, 3 * D_RWKV
    o5 = o3 + LORA_DECAY + LORA_ICLR
    r = mixed(0, o1)
    k = mixed(o1, o2)
    v = mixed(o2, o3)
    xwa = mixed(o3, o5)
    xg = mixed(o5, D_SHIFT)

    lane = jax.lax.broadcasted_iota(jnp.int32, (1, LANES), 1)
    lora_in = jnp.where(lane < LORA_DECAY, jnp.tanh(xwa), xwa)
    lora = _dot(lora_in, lora_ref[...])
    ld_out[...] = -DECAY_SCALE * _sigmoid(dbias_ref[...] + lora[:, 0:D_RWKV])
    a = _sigmoid(ibias_ref[...] + lora[:, D_RWKV:2 * D_RWKV])
    a_out[...] = a
    g_out[...] = _dot(_sigmoid(xg), wgate_ref[...]).astype(BF16)

    if has_vres:
        vdown = jnp.dot(xn, wvdown_ref[...], preferred_element_type=F32)
        v = v + (vfirst_ref[...] - v) * _sigmoid(vbias_ref[...] + _dot(vdown, wvup_ref[...]))
    v_out[...] = v

    kk = k * kk_ref[...]
    norm = jnp.sqrt(_head_sums(kk * kk, ones_ref[...]))
    kkn_out[...] = kk / jnp.maximum(norm, 1e-12)
    kp = k * (1.0 + (a - 1.0) * ka_ref[...])
    kp_out[...] = kp
    r_out[...] = r
    bonus_out[...] = (_head_sums(r * kp * rk_ref[...], ones_ref[...]) * v).astype(BF16)

    if has_first:
        last_out[...] = p_scr[CARRY:CARRY + tm, :]
    else:
        tail = p_scr[tm:tm + CARRY, :]
        last_out[...] = tail
        p_scr[0:CARRY, :] = tail


def _full(shape):
    nd = len(shape)
    return pl.BlockSpec(shape, lambda *_: (0,) * nd, pipeline_mode=pl.Buffered(1))


def _const_spec(a):
    if isinstance(a, tuple):
        arr, l = a
        return pl.BlockSpec((None,) + arr.shape[1:], lambda *_: (l,) + (0,) * (arr.ndim - 1),
                            pipeline_mode=pl.Buffered(1))
    return _full(a.shape)


def _const_arg(a):
    return a[0] if isinstance(a, tuple) else a


def _proj_call(x2d, seq_len, tm, lw, vfirst, first):
    n = x2d.shape[0]
    d_model = x2d.shape[1]
    has_vres = vfirst is not None
    has_first = first is not None
    nt = n // tm
    row = lambda w: pl.BlockSpec((tm, w), lambda i: (i, 0))
    ins = [x2d, lw['g_mix_pre'], lw['w_in'], lw['mu'], lw['lora'], lw['decay_bias'], lw['iclr_bias'],
           lw['w_gate_up'], lw['k_k'], lw['k_a'], lw['r_k'], lw['ones']]
    specs = [row(d_model)] + [_const_spec(a) for a in ins[1:]]
    if has_vres:
        ins += [vfirst, lw['vres_bias'], lw['w_vres_down'], lw['w_vres_up']]
        specs += [row(D_RWKV)] + [_const_spec(lw[k]) for k in ('vres_bias', 'w_vres_down', 'w_vres_up')]
    if has_first:
        ins.append(first)
        specs.append(row(D_SHIFT))
    sds = lambda w, dt=F32: jax.ShapeDtypeStruct((n, w), dt)
    last_rows = tm if has_first else CARRY
    out_shape = [sds(D_RWKV)] * 6 + [sds(D_RWKV, BF16)] * 2 + [sds(D_POOL), sds(2 * d_model, BF16),
                                     jax.ShapeDtypeStruct((nt * last_rows, D_SHIFT), F32)]
    out_specs = [row(D_RWKV)] * 8 + [row(D_POOL), row(2 * d_model),
                                     pl.BlockSpec((last_rows, D_SHIFT), lambda i: (i, 0))]
    return pl.pallas_call(
        functools.partial(_proj_kernel, tm=tm, seq_len=seq_len, has_vres=has_vres, has_first=has_first),
        grid=(nt,), in_specs=specs, out_specs=out_specs, out_shape=out_shape,
        scratch_shapes=[pltpu.VMEM((tm + CARRY, D_SHIFT), F32)],
        compiler_params=pltpu.CompilerParams(dimension_semantics=("arbitrary",),
                                             vmem_limit_bytes=VMEM_LIMIT),
    )(*[_const_arg(a) for a in ins])


def _wkv_kernel(r_ref, kp_ref, v_ref, kk_ref, a_ref, ld_ref, o_ref, hout_ref, h_scr, *, n_sub):
    c_idx = pl.program_id(1)
    n_steps = pl.num_programs(1)
    C = CHUNK

    lane = jax.lax.broadcasted_iota(jnp.int32, (1, PAIR), 1)
    left = lane < HEAD
    ri = jax.lax.broadcasted_iota(jnp.int32, (PAIR, PAIR), 0)
    ci = jax.lax.broadcasted_iota(jnp.int32, (PAIR, PAIR), 1)
    same = (ri >= HEAD) == (ci >= HEAD)
    strict = same & (ci < ri)
    incl = same & (ci <= ri)
    eye = ri == ci
    eye_f = jnp.where(eye, 1.0, 0.0).astype(F32)

    def stack(x):
        return jnp.concatenate([jnp.where(left, x, 0.0), jnp.where(left, 0.0, x)], axis=0)

    def twice(x):
        return jnp.concatenate([x, x], axis=0)

    @pl.when(c_idx == 0)
    def _():
        h_scr[...] = jnp.zeros(h_scr.shape, F32)


    a_t, r_t, b_t, k_t, b_h, k_h, v_st, w_last = ([] for _ in range(8))
    for j in range(n_sub):
        rows = slice(j * C, (j + 1) * C)
        ld = ld_ref[rows, :]
        cum = _prefix_rows(ld)
        clast = cum[C - 1:C, :]
        e_cum = jnp.exp(cum)
        e_prev = jnp.exp(cum - ld)
        e_neg = jnp.exp(-cum)
        e_rem = jnp.exp(clast - cum)
        wl = jnp.exp(clast)
        kk = kk_ref[rows, :]
        b = kk * a_ref[rows, :]
        kp = kp_ref[rows, :]
        full = (-(kk * e_prev), r_ref[rows, :] * e_cum, b * e_neg, kp * e_neg, b * e_rem, kp * e_rem)
        v = v_ref[rows, :]
        for p in range(N_PAIRS):
            sl = slice(p * PAIR, (p + 1) * PAIR)
            for dst, src in zip((a_t, r_t, b_t, k_t, b_h, k_h), full):
                dst.append(src[:, sl])
            v_st.append(stack(v[:, sl]))
            w_last.append(wl[:, sl])

    units = range(n_sub * N_PAIRS)
    sc = [_dot_nt(jnp.concatenate([a_t[q], r_t[q]], axis=0),
                  jnp.concatenate([stack(b_t[q]), stack(k_t[q])], axis=0)) for q in units]
    n_ab = [jnp.where(strict, twice(s[0:C, 0:PAIR]), 0.0) for s in sc]
    n_ak = [jnp.where(strict, twice(s[0:C, PAIR:2 * PAIR]), 0.0) for s in sc]
    n_rb = [jnp.where(incl, twice(s[C:2 * C, 0:PAIR]), 0.0) for s in sc]
    n_rk = [jnp.where(incl, twice(s[C:2 * C, PAIR:2 * PAIR]), 0.0) for s in sc]

    inv = [eye_f + n for n in n_ab]
    power = n_ab
    y = [_dot(n_ak[q], v_st[q]) for q in units]
    for _ in range(int(np.log2(C)) - 1):
        power = [_dot(m, m) for m in power]
        inv = [inv[q] + _dot(inv[q], power[q]) for q in units]

    au = [_dot(inv[q], jnp.concatenate([stack(a_t[q]), y[q]], axis=1)) for q in units]
    zero = jnp.zeros((PAIR, PAIR), F32)
    big = []
    for q in units:
        rhs = jnp.concatenate([au[q], jnp.concatenate([zero, v_st[q]], axis=1)], axis=0)
        lhs = jnp.concatenate(
            [jnp.concatenate([n_rb[q], n_rk[q]], axis=1),
             jnp.concatenate([stack(b_h[q]).T, stack(k_h[q]).T], axis=1)], axis=0)
        big.append(_dot(lhs, rhs))

    h = [h_scr[p] for p in range(N_PAIRS)]
    for j in range(n_sub):
        rows = slice(j * C, (j + 1) * C)
        qs = [j * N_PAIRS + p for p in range(N_PAIRS)]
        sd = []
        for p, q in enumerate(qs):
            r_hat = stack(r_t[q]) + big[q][0:PAIR, 0:PAIR]
            m_low = big[q][PAIR:2 * PAIR, 0:PAIR]
            sd.append(_dot(jnp.concatenate([r_hat, m_low], axis=0), h[p]))
        for p, q in enumerate(qs):
            o_st = sd[p][0:PAIR] + big[q][0:PAIR, PAIR:2 * PAIR]
            o_ref[rows, p * PAIR:(p + 1) * PAIR] = o_st[0:C] + o_st[C:2 * C]
            w_col = jnp.sum(jnp.where(eye, jnp.broadcast_to(w_last[q], (PAIR, PAIR)), 0.0),
                            axis=1, keepdims=True)
            h[p] = w_col * h[p] + sd[p][PAIR:2 * PAIR] + big[q][PAIR:2 * PAIR, PAIR:2 * PAIR]
    for p in range(N_PAIRS):
        h_scr[p] = h[p]

    @pl.when(c_idx == n_steps - 1)
    def _():
        for p in range(N_PAIRS):
            hout_ref[0, p] = h_scr[p, 0:HEAD, :] + h_scr[p, HEAD:PAIR, :]


def _wkv_call(ops, n_seq, seq_len):
    n_sub = WKV_SUB if seq_len % (WKV_SUB * CHUNK) == 0 else 1
    rows = n_sub * CHUNK
    assert seq_len % rows == 0
    ns = seq_len // rows
    blk = pl.BlockSpec((rows, D_RWKV), lambda b, c: (b * ns + c, 0))
    hspec = pl.BlockSpec((1, N_PAIRS, HEAD, PAIR), lambda b, c: (b, 0, 0, 0))
    return pl.pallas_call(
        functools.partial(_wkv_kernel, n_sub=n_sub),
        grid=(n_seq, ns), in_specs=[blk] * 6, out_specs=[blk, hspec],
        out_shape=[jax.ShapeDtypeStruct((n_seq * seq_len, D_RWKV), F32),
                   jax.ShapeDtypeStruct((n_seq, N_PAIRS, HEAD, PAIR), F32)],
        scratch_shapes=[pltpu.VMEM((N_PAIRS, PAIR, PAIR), F32)],
        compiler_params=pltpu.CompilerParams(dimension_semantics=("arbitrary", "arbitrary"),
                                             vmem_limit_bytes=VMEM_LIMIT),
    )(*ops)


def _wkv_lanes_kernel(*refs, seq_len, n_prev):
    r_ref, kp_ref, v_ref, kk_ref, a_ref, ld_ref, s_ref = refs[:7]
    prev_refs = refs[7:7 + n_prev]
    o_ref, stacked_ref, op_scr, o_scr = refs[7 + n_prev:]
    for i, prev in enumerate(prev_refs):
        stacked_ref[i] = prev[...]
    sout_ref = stacked_ref.at[n_prev]
    T = seq_len
    nb = r_ref.shape[0] // T
    tok = lambda ref, t: ref[pl.ds(t, nb, stride=T), :]

    for t in range(T):
        kk = tok(kk_ref, t)
        op_scr[0, t] = (-kk).T
        op_scr[1, t] = jnp.exp(tok(ld_ref, t)).T
        op_scr[2, t] = (kk * tok(a_ref, t)).T
        op_scr[3, t] = tok(kp_ref, t).T
        op_scr[4, t] = tok(r_ref, t).T
        op_scr[5, t] = tok(v_ref, t).T

    def all_keys(x):
        y = jnp.sum(x.reshape(HEAD // CARRY, CARRY, nb), axis=0)
        for i in range(1, int(np.log2(CARRY)) + 1):
            y = y + pltpu.roll(y, CARRY >> i, 0)
        return y

    def over_keys(y, x):
        return (y[None] * x.reshape(HEAD // CARRY, CARRY, nb)).reshape(HEAD, nb)

    for hh in range(2):
        keys = slice(hh * HEAD, (hh + 1) * HEAD)

        def value_row(vi, carry, hh=hh, keys=keys):
            s = s_ref[hh, vi]
            row = hh * HEAD + vi
            for t in range(T):
                sa = all_keys(s * op_scr[0, t, keys, :])
                v_row = jnp.broadcast_to(op_scr[5, t, pl.ds(row, 1), :], (CARRY, nb))
                s = (s * op_scr[1, t, keys, :] + over_keys(sa, op_scr[2, t, keys, :])
                     + over_keys(v_row, op_scr[3, t, keys, :]))
                o_scr[t, pl.ds(row, 1), :] = all_keys(s * op_scr[4, t, keys, :])[0:1]
            sout_ref[hh, vi] = s
            return carry

        jax.lax.fori_loop(0, HEAD, value_row, 0, unroll=LANES_UNROLL)

    for t in range(T):
        o_ref[pl.ds(t, nb, stride=T), :] = o_scr[t].T


def _wkv_lanes_call(ops, n_seq, seq_len, state_t, layer, prev_states):
    assert n_seq == LANES
    n_prev = len(prev_states)
    rows = n_seq * seq_len
    blk = pl.BlockSpec((rows, PAIR), lambda p: (0, p))
    in_state = pl.BlockSpec((None, 2, HEAD, HEAD, n_seq), lambda p: (layer, p, 0, 0, 0))
    prev_spec = pl.BlockSpec((2, HEAD, HEAD, n_seq), lambda p: (p, 0, 0, 0))
    out_state = pl.BlockSpec((n_prev + 1, 2, HEAD, HEAD, n_seq), lambda p: (0, p, 0, 0, 0))
    return pl.pallas_call(
        functools.partial(_wkv_lanes_kernel, seq_len=seq_len, n_prev=n_prev),
        grid=(N_PAIRS,), in_specs=[blk] * 6 + [in_state] + [prev_spec] * n_prev, out_specs=[blk, out_state],
        out_shape=[jax.ShapeDtypeStruct((rows, D_RWKV), F32),
                   jax.ShapeDtypeStruct((n_prev + 1,) + state_t.shape[1:], F32)],
        scratch_shapes=[pltpu.VMEM((6, seq_len, PAIR, n_seq), F32), pltpu.VMEM((seq_len, PAIR, n_seq), F32)],
        compiler_params=pltpu.CompilerParams(dimension_semantics=("arbitrary",),
                                             vmem_limit_bytes=VMEM_LIMIT),
    )(*ops, state_t, *prev_states)


def _merge_kernel(*refs, tm, tiles_per_seq, banded, with_ffn):
    it = iter(refs)
    o_ref, bonus_ref, g_ref, u_ref, hist_ref, gates_ref, x_ref = (next(it) for _ in range(7))
    if banded:
        band_ref = next(it)
    (invc_ref, lnw_ref, lnb_ref, ones_ref, pmap_ref, pscale_ref, wba_ref, wbb_ref, wout_ref,
     gpost_ref) = (next(it) for _ in range(10))
    if with_ffn:
        ffn_refs = [next(it) for _ in range(4)]
    y_ref = next(it)
    if not banded:
        s_a, s_b = next(it), next(it)
    d_model = x_ref.shape[1]
    inv_h = 1.0 / HEAD

    o = o_ref[...]
    mean = _head_sums(o, ones_ref[...]) * inv_h
    cen = o - mean
    var = _head_sums(cen * cen, ones_ref[...]) * inv_h
    o_n = cen * jax.lax.rsqrt(var + GN_EPS) * lnw_ref[...] + lnb_ref[...]
    o_rwkv = (o_n + bonus_ref[...].astype(F32)) * g_ref[...].astype(F32)

    u = u_ref[...]
    lane = jax.lax.broadcasted_iota(jnp.int32, (1, D_POOL), 1)
    win_sum = jnp.zeros((tm, D_POOL), F32)
    if banded:
        full = jnp.concatenate([hist_ref[...], u], axis=0)
        for gi in range(len(POOL_WINDOWS)):
            s = _dot_exact_lhs(band_ref[gi], full, 2)
            in_group = (lane >= gi * POOL_GROUP) & (lane < (gi + 1) * POOL_GROUP)
            win_sum = jnp.where(in_group, s, win_sum)
    else:
        keep = jnp.where((pl.program_id(0) % tiles_per_seq) == 0, 0.0, 1.0)
        s_a[0:POOL_HIST, :] = hist_ref[...] * keep
        s_a[POOL_HIST:POOL_HIST + tm, :] = u
        src, dst = s_a, s_b
        total = POOL_HIST + tm
        w = 1
        for gi, win in enumerate(POOL_WINDOWS):
            while w < win:
                lo = CARRY * int(np.log2(2 * w))
                dst[lo:total, :] = src[lo:total, :] + src[lo - w:total - w, :]
                src, dst = dst, src
                w *= 2
            in_group = (lane >= gi * POOL_GROUP) & (lane < (gi + 1) * POOL_GROUP)
            win_sum = jnp.where(in_group, src[POOL_HIST:total, :], win_sum)
    diff = win_sum * invc_ref[...] - u
    o_pool = _dot(diff, pmap_ref[...]) * pscale_ref[...]

    gates = gates_ref[...].astype(F32)
    merged = (gates[:, 0:d_model] * _dot(o_rwkv, wba_ref[...])
              + gates[:, d_model:2 * d_model] * _dot(o_pool, wbb_ref[...]))
    x1 = x_ref[...] + _rms(_dot(merged, wout_ref[...]), gpost_ref[...])
    y_ref[...] = _ffn_value(x1, *ffn_refs) if with_ffn else x1


def _merge_call(o, bonus, g, u, hist, hist_rows, gates, x2d, seq_len, tm, band, invc, lw, with_ffn=False):
    n, d_model = x2d.shape
    nt = n // tm
    tiles_per_seq = max(seq_len // tm, 1)
    banded = band is not None
    row = lambda w: pl.BlockSpec((tm, w), lambda i: (i, 0))
    if banded:
        hist_spec = pl.BlockSpec((hist_rows, D_POOL), lambda i: (i, 0))
        invc_spec = _full(invc.shape)
        scratch = []
    else:
        per = tm // hist_rows
        hist_spec = pl.BlockSpec((hist_rows, D_POOL), lambda i: (jnp.maximum(i * per - 1, 0), 0))
        invc_spec = pl.BlockSpec((tm, D_POOL), lambda i: (i % tiles_per_seq, 0))
        scratch = [pltpu.VMEM((hist_rows + tm, D_POOL), F32)] * 2
    consts = [lw['ln_x_w'], lw['ln_x_b'], lw['ones'], lw['pool_map'], lw['pool_scale'],
              lw['w_branch_rwkv'], lw['w_branch_pool'], lw['w_out'], lw['g_mix_post']]
    if with_ffn:
        consts += [lw['g_ffn_pre'], lw['w_ffn_in'], lw['w_ffn_out'], lw['g_ffn_post']]
    ins = [o, bonus, g, u, hist, gates, x2d] + ([band] if banded else []) + [invc] + consts
    specs = ([row(D_RWKV)] * 3 + [row(D_POOL), hist_spec, row(2 * d_model), row(d_model)]
             + ([_full(band.shape)] if banded else []) + [invc_spec] + [_const_spec(a) for a in consts])
    return pl.pallas_call(
        functools.partial(_merge_kernel, tm=tm, tiles_per_seq=tiles_per_seq, banded=banded, with_ffn=with_ffn),
        grid=(nt,), in_specs=specs, out_specs=row(d_model),
        out_shape=jax.ShapeDtypeStruct((n, d_model), F32), scratch_shapes=scratch,
        compiler_params=pltpu.CompilerParams(dimension_semantics=("arbitrary",),
                                             vmem_limit_bytes=VMEM_LIMIT),
    )(*[_const_arg(a) for a in ins])


def _ffn_value(x, gpre_ref, win_ref, wout_ref, gpost_ref):
    d_ff = wout_ref.shape[0]
    h = _rms(x, gpre_ref[...]).astype(BF16)
    gu = jnp.dot(h, win_ref[...], preferred_element_type=F32)
    gt = gu[:, 0:d_ff]
    act = gt * _sigmoid(gt) * gu[:, d_ff:2 * d_ff]
    return x + _rms(_dot(act, wout_ref[...]), gpost_ref[...])


def _ffn_kernel(x_ref, gpre_ref, win_ref, wout_ref, gpost_ref, y_ref):
    y_ref[...] = _ffn_value(x_ref[...], gpre_ref, win_ref, wout_ref, gpost_ref)


def _ffn_call(x2d, tm, lw):
    n, d_model = x2d.shape
    row = pl.BlockSpec((tm, d_model), lambda i: (i, 0))
    consts = [lw['g_ffn_pre'], lw['w_ffn_in'], lw['w_ffn_out'], lw['g_ffn_post']]
    return pl.pallas_call(
        _ffn_kernel, grid=(n // tm,), in_specs=[row] + [_const_spec(a) for a in consts],
        out_specs=row, out_shape=jax.ShapeDtypeStruct((n, d_model), F32),
        compiler_params=pltpu.CompilerParams(dimension_semantics=("arbitrary",),
                                             vmem_limit_bytes=VMEM_LIMIT),
    )(x2d, *[_const_arg(a) for a in consts])


def _prompt_inv_count(seq_len):
    pos = np.arange(seq_len)[:, None]
    win = np.repeat(np.array(POOL_WINDOWS), POOL_GROUP)[None, :]
    return jnp.asarray(1.0 / np.minimum(pos + 1, win).astype(np.float32), F32)


def _sample_pool_consts(tm, seq_len):
    ns = tm // seq_len
    rs, rt = np.divmod(np.arange(tm), seq_len)
    hs, hj = np.divmod(np.arange(ns * POOL_SLOTS), POOL_SLOTS)
    cs = np.concatenate([hs, rs])[None, :]
    cpos = np.concatenate([hj - 1, POOL_BUF + rt])[None, :]
    dist = (POOL_BUF + rt)[:, None] - cpos
    same = (rs[:, None] == cs) & (cpos >= 0)
    band = np.stack([same & (dist >= 0) & (dist < w) for w in POOL_WINDOWS]).astype(np.float32)
    win = np.repeat(np.array(POOL_WINDOWS), POOL_GROUP)[None, :]
    pos = (SAMPLE_START_POS + rt)[:, None]
    invc = 1.0 / np.minimum(pos + 1, win).astype(np.float32)
    return jnp.asarray(band, BF16), jnp.asarray(invc, F32)


def _block_diag(blocks):
    n = len(blocks)
    rows = []
    for i, blk in enumerate(blocks):
        rows.append(jnp.concatenate(
            [blk if j == i else jnp.zeros((blk.shape[0], blocks[j].shape[1]), blk.dtype) for j in range(n)],
            axis=1))
    return jnp.concatenate(rows, axis=0)


BIG_WEIGHTS = ('w_in', 'w_gate_up', 'w_branch_rwkv', 'w_branch_pool', 'w_out', 'w_ffn_in', 'w_ffn_out')


def _layer_weights(l, w, wb):
    row = lambda a: a.reshape(1, -1).astype(F32)
    lw = {k: (wb[k], l) for k in BIG_WEIGHTS}
    lw.update({
        'g_mix_pre': row(w['norm_mix_pre'][l]), 'g_mix_post': row(w['norm_mix_post'][l]),
        'g_ffn_pre': row(w['norm_ffn_pre'][l]), 'g_ffn_post': row(w['norm_ffn_post'][l]),
        'mu': row(w['mu_shift'][l]), 'decay_bias': row(w['decay_bias'][l]), 'iclr_bias': row(w['iclr_bias'][l]),
        'lora': _block_diag([w['w_decay_up'][l], w['w_iclr_up'][l]]).astype(BF16),
        'k_k': row(w['k_k'][l]), 'k_a': row(w['k_a'][l]), 'r_k': row(w['r_k'][l]),
        'ln_x_w': row(w['ln_x_w'][l]), 'ln_x_b': row(w['ln_x_b'][l]),
        'pool_map': _block_diag([w['pool_map'][l, gi] for gi in range(len(POOL_WINDOWS))]).astype(BF16),
        'pool_scale': row(w['pool_scale'][l]),
        'ones': jnp.asarray(np.kron(np.eye(SUM_TILE // HEAD), np.ones((HEAD, HEAD))), BF16),
    })
    if l > 0:
        pad = LANES - LORA_VRES
        lw['vres_bias'] = row(w['vres_bias'][l - 1])
        lw['w_vres_down'] = jnp.pad(w['w_vres_down'][l - 1], ((0, 0), (0, pad))).astype(BF16)
        lw['w_vres_up'] = jnp.pad(w['w_vres_up'][l - 1], ((0, pad), (0, 0))).astype(BF16)
    return lw


def _from_pairs(h):
    lead = h.shape[:-3]
    n = len(lead)
    h = h.reshape(*lead, N_PAIRS, HEAD, 2, HEAD)
    return h.transpose(*range(n), n, n + 2, n + 3, n + 1).reshape(*lead, N_HEADS, HEAD, HEAD)


def _tile_rows(n_seq, seq_len, target):
    if seq_len >= target:
        return target
    return min(n_seq * seq_len, target)


def kernel(x_prompt, x_sample, state_wkv, state_shift, state_pool, norm_mix_pre, norm_mix_post, norm_ffn_pre, norm_ffn_post, w_in, mu_shift, decay_bias, w_decay_up, iclr_bias, w_iclr_up, w_gate_up, k_k, k_a, r_k, ln_x_w, ln_x_b, vres_bias, w_vres_down, w_vres_up, pool_map, pool_scale, w_branch_rwkv, w_branch_pool, w_out, w_ffn_in, w_ffn_out):
    weights = dict(norm_mix_pre=norm_mix_pre, norm_mix_post=norm_mix_post, norm_ffn_pre=norm_ffn_pre,
                   norm_ffn_post=norm_ffn_post, w_in=w_in, mu_shift=mu_shift, decay_bias=decay_bias,
                   w_decay_up=w_decay_up, iclr_bias=iclr_bias, w_iclr_up=w_iclr_up, w_gate_up=w_gate_up,
                   k_k=k_k, k_a=k_a, r_k=r_k, ln_x_w=ln_x_w, ln_x_b=ln_x_b, vres_bias=vres_bias,
                   w_vres_down=w_vres_down, w_vres_up=w_vres_up, pool_map=pool_map, pool_scale=pool_scale,
                   w_branch_rwkv=w_branch_rwkv, w_branch_pool=w_branch_pool, w_out=w_out,
                   w_ffn_in=w_ffn_in, w_ffn_out=w_ffn_out)
    depth = w_in.shape[0]
    wb = {k: weights[k].astype(BF16) for k in BIG_WEIGHTS}
    bp, tp, d_model = x_prompt.shape
    bs, ts, _ = x_sample.shape
    tm_p = _tile_rows(bp, tp, ROW_TILE)
    tm_s = _tile_rows(bs, ts, ROW_TILE)
    invc_p = _prompt_inv_count(tp)
    band_s, invc_s = _sample_pool_consts(tm_s, ts)

    state_t = state_wkv.transpose(0, 2, 3, 4, 1)
    yp = x_prompt.reshape(bp * tp, d_model)
    ys = x_sample.reshape(bs * ts, d_model)
    vf_p = vf_s = None
    outs = {k: [] for k in ('wkv_p', 'shift_p', 'pool_p', 'wkv_s', 'shift_s', 'pool_s')}
    for l in range(depth):
        lw = _layer_weights(l, weights, wb)

        (r, kp, v, kkn, a, ld, g, bonus, u, gates, last) = _proj_call(yp, tp, tm_p, lw, vf_p, None)
        if l == 0:
            vf_p = v
        o, hout = _wkv_call((r, kp, v, kkn, a, ld), bp, tp)
        yp = _merge_call(o, bonus, g, u, u, POOL_HIST, gates, yp, tp, tm_p, None, invc_p, lw, with_ffn=True)
        outs['wkv_p'].append(_from_pairs(hout))
        tiles = tp // tm_p
        outs['shift_p'].append(last.reshape(bp, tiles, CARRY, D_SHIFT)[:, -1, -1])
        outs['pool_p'].append(u.reshape(bp, tp, D_POOL)[:, -POOL_BUF:])

        first = jnp.zeros((bs, ts, D_SHIFT), F32).at[:, 0].set(state_shift[l]).reshape(bs * ts, D_SHIFT)
        (r, kp, v, kkn, a, ld, g, bonus, u, gates, ps_all) = _proj_call(ys, ts, tm_s, lw, vf_s, first)
        if l == 0:
            vf_s = v
        prev = outs['wkv_s'] if l == depth - 1 else []
        o, hout = _wkv_lanes_call((r, kp, v, kkn, a, ld), bs, ts, state_t, l, prev)
        outs['wkv_s'].append(hout[0] if l < depth - 1 else hout)
        hist = jnp.pad(state_pool[l], ((0, 0), (1, 0), (0, 0))).reshape(bs * POOL_SLOTS, D_POOL)
        ys = _merge_call(o, bonus, g, u, hist, (tm_s // ts) * POOL_SLOTS, gates, ys, ts, tm_s, band_s, invc_s,
                         lw, with_ffn=True)
        outs['shift_s'].append(ps_all.reshape(bs, ts, D_SHIFT)[:, -1])
        outs['pool_s'].append(jnp.concatenate(
            [state_pool[l], u.reshape(bs, ts, D_POOL)], axis=1)[:, -POOL_BUF:])

    return (yp.reshape(bp, tp, d_model), ys.reshape(bs, ts, d_model),
            jnp.stack(outs['wkv_p']), jnp.stack(outs['shift_p']), jnp.stack(outs['pool_p']),
            outs['wkv_s'][-1].transpose(0, 4, 1, 2, 3), jnp.stack(outs['shift_s']), jnp.stack(outs['pool_s']))
```
